```python
import jax
import jax.numpy as jnp
from jax import lax
import numpy as np

D_MODEL = 1024
BATCH = 4
SEQ = 8192
DEPTH = 2
DEC_BATCH = 128
DEC_SEQ = 4
PAST_LEN = 16384
PAGE_SIZE = 128

N_EVEN = (DEPTH + 1) // 2
N_ODD = DEPTH // 2
EPS = 1e-6
POOL_WIDTH = D_MODEL // 2
POOL_WINDOWS = (2, 4, 8, 16)
POOL_GROUPS = len(POOL_WINDOWS)
POOL_GROUP_DIM = POOL_WIDTH // POOL_GROUPS
POOL_BUF = max(POOL_WINDOWS) - 1
LRU_WIDTH = D_MODEL // 2
LRU_HEADS = 8
LRU_HEAD_DIM = LRU_WIDTH // LRU_HEADS
CONV_WIDTH = 4
LRU_C = 8.0
MLA_HEADS = 16
QK_NOPE = 64
QK_ROPE = 32
V_DIM = 64
Q_LORA = 384
KV_LORA = 256
ROPE_THETA = 10000.0
SM_SCALE = (QK_NOPE + QK_ROPE) ** -0.5
Q_BLOCK = 128
MOE_GROUPS = 4
EXPERTS_PER_GROUP = 8
N_EXPERTS = MOE_GROUPS * EXPERTS_PER_GROUP
TOP_K = 2
EXPERT_FF = 512
MOE_BLOCK = 128

kernel_name = 'hybrid_pool_rglru_mla_hmoe_step'


def rmsnorm(x, gain):
    xf = x.astype(jnp.float32)
    y = xf * lax.rsqrt(jnp.mean(xf * xf, axis=-1, keepdims=True) + EPS)
    return (y * gain.astype(jnp.float32)).astype(x.dtype)


def modulate(x, gain, shift, scale):
    return rmsnorm(x, gain) * (1.0 + scale[:, None]) + shift[:, None]


def apply_rope(x, pos):
    half = x.shape[-1] // 2
    inv_freq = ROPE_THETA ** (-jnp.arange(half, dtype=jnp.float32) / half)
    ang = pos.astype(jnp.float32)[:, None] * inv_freq[None, :]
    shape = (pos.shape[0],) + (1,) * (x.ndim - 3) + (half,)
    cos = jnp.cos(ang).reshape(shape)
    sin = jnp.sin(ang).reshape(shape)
    xf = x.astype(jnp.float32)
    x1, x2 = xf[..., :half], xf[..., half:]
    return jnp.concatenate([x1 * cos - x2 * sin, x2 * cos + x1 * sin], axis=-1).astype(x.dtype)


def pool_mixer(u, buf, pos, w_pool, pool_scale):
    bsz, t_len, _ = u.shape
    z = jnp.concatenate([buf, u], axis=1)
    zf = z.astype(jnp.float32)
    csum = jnp.concatenate([jnp.zeros_like(zf[:, :1]), jnp.cumsum(zf, axis=1)], axis=1)
    hi = csum[:, POOL_BUF + 1:POOL_BUF + 1 + t_len]
    means = []
    for g, w in enumerate(POOL_WINDOWS):
        ch = slice(g * POOL_GROUP_DIM, (g + 1) * POOL_GROUP_DIM)
        lo = csum[:, POOL_BUF + 1 - w:POOL_BUF + 1 - w + t_len, ch]
        count = jnp.minimum(pos + 1, w).astype(jnp.float32)[None, :, None]
        means.append((hi[..., ch] - lo) / count)
    pooled = jnp.concatenate(means, axis=-1) - u.astype(jnp.float32)
    pooled = pooled.astype(u.dtype).reshape(bsz, t_len, POOL_GROUPS, POOL_GROUP_DIM)
    y = jnp.einsum('btgi,gio->btgo', pooled, w_pool).reshape(bsz, t_len, POOL_WIDTH)
    return y * pool_scale, z[:, -POOL_BUF:]


def _linear_combine(left, right):
    a_l, b_l = left
    a_r, b_r = right
    return a_l * a_r, a_r * b_l + b_r


def rglru_mixer(u_x, u_g, conv_buf, h0, conv_w, conv_b, w_r, b_r, w_i, b_i, lam):
    bsz, t_len, width = u_x.shape
    z = jnp.concatenate([conv_buf, u_x], axis=1)
    xc = conv_b + sum(z[:, k:k + t_len] * conv_w[k] for k in range(CONV_WIDTH))
    xh = xc.reshape(bsz, t_len, LRU_HEADS, LRU_HEAD_DIM)
    r = jax.nn.sigmoid((jnp.einsum('bthi,hio->btho', xh, w_r).reshape(bsz, t_len, width) + b_r).astype(jnp.float32))
    gi = jax.nn.sigmoid((jnp.einsum('bthi,hio->btho', xh, w_i).reshape(bsz, t_len, width) + b_i).astype(jnp.float32))
    log_a = -LRU_C * r * jax.nn.softplus(-lam.astype(jnp.float32))
    a = jnp.exp(log_a)
    b = jnp.sqrt(-jnp.expm1(2.0 * log_a)) * gi * xc.astype(jnp.float32)
    b = b.at[:, 0].add(a[:, 0] * h0.astype(jnp.float32))
    _, h = lax.associative_scan(_linear_combine, (a, b), axis=1)
    y = h.astype(u_x.dtype) * jax.nn.gelu(u_g)
    return y, z[:, -(CONV_WIDTH - 1):], h[:, -1].astype(h0.dtype)


def mla_project(h, pos, w_in, q_norm, w_uq, kv_norm):
    bsz, t_len, _ = h.shape
    proj = h @ w_in
    c_kv = rmsnorm(proj[..., Q_LORA:Q_LORA + KV_LORA], kv_norm)
    k_rope = apply_rope(proj[..., Q_LORA + KV_LORA:], pos)
    q = (rmsnorm(proj[..., :Q_LORA], q_norm) @ w_uq).reshape(bsz, t_len, MLA_HEADS, QK_NOPE + QK_ROPE)
    q_rope = apply_rope(q[..., QK_NOPE:], pos)
    return q[..., :QK_NOPE], q_rope, c_kv, k_rope


def mla_prompt_attend(q_nope, q_rope, c_kv, k_rope, w_uk, w_uv):
    bsz, s_len = c_kv.shape[:2]
    k_nope = jnp.einsum('bsc,chn->bshn', c_kv, w_uk)
    v = jnp.einsum('bsc,chv->bshv', c_kv, w_uv)
    n_blk = s_len // Q_BLOCK
    qn = q_nope.reshape(bsz, n_blk, Q_BLOCK, MLA_HEADS, QK_NOPE).swapaxes(0, 1)
    qr = q_rope.reshape(bsz, n_blk, Q_BLOCK, MLA_HEADS, QK_ROPE).swapaxes(0, 1)
    k_pos = jnp.arange(s_len)

    def block(args):
        qn_b, qr_b, q_start = args
        s = (jnp.einsum('bqhn,bkhn->bhqk', qn_b, k_nope)
             + jnp.einsum('bqhr,bkr->bhqk', qr_b, k_rope)).astype(jnp.float32) * SM_SCALE
        q_pos = q_start + jnp.arange(Q_BLOCK)
        s = jnp.where(k_pos[None, :] <= q_pos[:, None], s, -jnp.inf)
        p = jax.nn.softmax(s, axis=-1).astype(v.dtype)
        return jnp.einsum('bhqk,bkhv->bqhv', p, v)

    o = lax.map(block, (qn, qr, jnp.arange(n_blk) * Q_BLOCK))
    return o.swapaxes(0, 1).reshape(bsz, s_len, MLA_HEADS * V_DIM)


def _online_update(carry, s, c_kv):
    m, l, acc = carry
    m_new = jnp.maximum(m, jnp.max(s, axis=-1))
    corr = jnp.exp(m - m_new)
    p = jnp.exp(s - m_new[..., None])
    l = l * corr + jnp.sum(p, axis=-1)
    acc = acc * corr[..., None] + jnp.einsum('bthk,bkc->bthc', p, c_kv.astype(jnp.float32))
    return (m_new, l, acc)


def mla_sample_attend(q_nope, q_rope, c_kv, k_rope, cache_kv_latent, cache_k_rope, layer, page_table, w_uk, w_uv):
    bsz, t_len = q_nope.shape[:2]
    q_lat = jnp.einsum('bthn,chn->bthc', q_nope, w_uk)

    def scores(ck, kr):
        return (jnp.einsum('bthc,bkc->bthk', q_lat, ck)
                + jnp.einsum('bthr,bkr->bthk', q_rope, kr)).astype(jnp.float32) * SM_SCALE

    def page_step(carry, pages):
        ck = cache_kv_latent[layer, pages]
        kr = cache_k_rope[layer, pages]
        return _online_update(carry, scores(ck, kr), ck), None

    init = (jnp.full((bsz, t_len, MLA_HEADS), -jnp.inf, jnp.float32),
            jnp.zeros((bsz, t_len, MLA_HEADS), jnp.float32),
            jnp.zeros((bsz, t_len, MLA_HEADS, KV_LORA), jnp.float32))
    carry, _ = lax.scan(page_step, init, page_table.T)
    causal = jnp.tril(jnp.ones((t_len, t_len), bool))
    s_new = jnp.where(causal[None, :, None, :], scores(c_kv, k_rope), -jnp.inf)
    _, l, acc = _online_update(carry, s_new, c_kv)
    o_lat = (acc / l[..., None]).astype(w_uv.dtype)
    o = jnp.einsum('bthc,chv->bthv', o_lat, w_uv)
    return o.reshape(bsz, t_len, MLA_HEADS * V_DIM)


def hier_route(h, w_group, b_group, w_expert, b_expert):
    hf = h.astype(jnp.float32)
    group_logits = hf @ w_group.astype(jnp.float32) + b_group.astype(jnp.float32)
    group = jnp.argmax(group_logits, axis=-1)
    p_group = jnp.take_along_axis(jax.nn.softmax(group_logits, axis=-1), group[:, None], axis=-1)
    expert_logits = jnp.einsum('nd,gde->nge', hf, w_expert.astype(jnp.float32)) + b_expert.astype(jnp.float32)
    in_group = jnp.take_along_axis(expert_logits, group[:, None, None], axis=1)[:, 0]
    top_val, top_idx = lax.top_k(in_group, TOP_K)
    gate = jax.nn.softmax(top_val, axis=-1) * p_group
    expert = group[:, None] * EXPERTS_PER_GROUP + top_idx
    return expert.astype(jnp.int32), gate


def moe_ffn(h, expert, gate, w_gate, w_up, w_down):
    n_tok, d = h.shape
    n_assign = n_tok * TOP_K
    flat_e = expert.reshape(n_assign)
    order = jnp.argsort(flat_e)
    sorted_e = flat_e[order]
    counts = jnp.bincount(flat_e, length=N_EXPERTS)
    padded = (counts + MOE_BLOCK - 1) // MOE_BLOCK * MOE_BLOCK
    pad_end = jnp.cumsum(padded)
    pad_start = pad_end - padded
    start = jnp.cumsum(counts) - counts
    dest = pad_start[sorted_e] + jnp.arange(n_assign) - start[sorted_e]
    n_blocks = -(-n_assign // MOE_BLOCK) + N_EXPERTS
    slot_tok = jnp.full((n_blocks * MOE_BLOCK,), n_tok, jnp.int32).at[dest].set((order // TOP_K).astype(jnp.int32))
    block_e = jnp.minimum(jnp.searchsorted(pad_end, jnp.arange(n_blocks) * MOE_BLOCK, side='right'), N_EXPERTS - 1)
    h_pad = jnp.concatenate([h, jnp.zeros((1, d), h.dtype)], axis=0)

    def block(args):
        tok, e = args
        xs = h_pad[tok]
        return (jax.nn.silu(xs @ w_gate[e]) * (xs @ w_up[e])) @ w_down[e]

    ys = lax.map(block, (slot_tok.reshape(n_blocks, MOE_BLOCK), block_e)).reshape(-1, d)
    y_assign = jnp.zeros((n_assign, d), ys.dtype).at[order].set(ys[dest])
    return jnp.einsum('nkd,nk->nd', y_assign.reshape(n_tok, TOP_K, d), gate.astype(ys.dtype))


def setup_inputs(seed: int = 0) -> dict:
    key = jax.random.key(seed)
    keys = iter(jax.random.split(key, 48))

    def nrm(shape, scale):
        return jax.random.normal(next(keys), shape, jnp.float32) * scale

    n_pages = PAST_LEN // PAGE_SIZE
    n_phys = (DEC_BATCH * n_pages * 5) // 4
    page_table = jax.random.permutation(next(keys), n_phys)[:DEC_BATCH * n_pages].reshape(DEC_BATCH, n_pages).astype(jnp.int32)
    a0 = jax.random.uniform(next(keys), (N_EVEN, LRU_WIDTH), jnp.float32, 0.9, 0.999)
    lru_lambda = jnp.log(a0) - jnp.log1p(-a0)
    d = D_MODEL
    x_prompt = nrm((BATCH, SEQ, d), 1.0)
    x_sample = nrm((DEC_BATCH, DEC_SEQ, d), 1.0)
    state_pool = nrm((N_EVEN, DEC_BATCH, POOL_BUF, POOL_WIDTH), 1.0)
    state_conv = nrm((N_EVEN, DEC_BATCH, CONV_WIDTH - 1, LRU_WIDTH), 1.0)
    state_lru = nrm((N_EVEN, DEC_BATCH, LRU_WIDTH), 0.5)
    cache_kv_latent = nrm((N_ODD, n_phys, PAGE_SIZE, KV_LORA), 1.0)
    cache_k_rope = nrm((N_ODD, n_phys, PAGE_SIZE, QK_ROPE), 1.0)
    c_prompt = nrm((BATCH, d), 1.0)
    c_sample = nrm((DEC_BATCH, d), 1.0)
    ada_w = nrm((DEPTH, d, 6 * d), 0.5 * d ** -0.5)
    ada_b = nrm((DEPTH, 6 * d), 0.02)
    norm_mix = 1.0 + nrm((DEPTH, d), 0.02)
    norm_ffn = 1.0 + nrm((DEPTH, d), 0.02)
    ab_w_in = nrm((N_EVEN, d, POOL_WIDTH + 2 * LRU_WIDTH), d ** -0.5)
    ab_w_out = nrm((N_EVEN, POOL_WIDTH + LRU_WIDTH, d), (POOL_WIDTH + LRU_WIDTH) ** -0.5)
    pool_w = nrm((N_EVEN, POOL_GROUPS, POOL_GROUP_DIM, POOL_GROUP_DIM), POOL_GROUP_DIM ** -0.5)
    pool_scale = 1.0 + nrm((N_EVEN, POOL_WIDTH), 0.1)
    conv_w = nrm((N_EVEN, CONV_WIDTH, LRU_WIDTH), 0.5)
    conv_b = nrm((N_EVEN, LRU_WIDTH), 0.02)
    lru_w_r = nrm((N_EVEN, LRU_HEADS, LRU_HEAD_DIM, LRU_HEAD_DIM), LRU_HEAD_DIM ** -0.5)
    lru_b_r = nrm((N_EVEN, LRU_WIDTH), 0.02)
    lru_w_i = nrm((N_EVEN, LRU_HEADS, LRU_HEAD_DIM, LRU_HEAD_DIM), LRU_HEAD_DIM ** -0.5)
    lru_b_i = nrm((N_EVEN, LRU_WIDTH), 0.02)
    mla_w_in = nrm((N_ODD, d, Q_LORA + KV_LORA + QK_ROPE), d ** -0.5)
    mla_q_norm = 1.0 + nrm((N_ODD, Q_LORA), 0.02)
    mla_w_uq = nrm((N_ODD, Q_LORA, MLA_HEADS * (QK_NOPE + QK_ROPE)), Q_LORA ** -0.5)
    mla_kv_norm = 1.0 + nrm((N_ODD, KV_LORA), 0.02)
    mla_w_uk = nrm((N_ODD, KV_LORA, MLA_HEADS, QK_NOPE), KV_LORA ** -0.5)
    mla_w_uv = nrm((N_ODD, KV_LORA, MLA_HEADS, V_DIM), KV_LORA ** -0.5)
    mla_w_out = nrm((N_ODD, MLA_HEADS * V_DIM, d), (MLA_HEADS * V_DIM) ** -0.5)
    router_w_group = nrm((DEPTH, d, MOE_GROUPS), d ** -0.5)
    router_b_group = nrm((DEPTH, MOE_GROUPS), 0.01)
    router_w_expert = nrm((DEPTH, MOE_GROUPS, d, EXPERTS_PER_GROUP), d ** -0.5)
    router_b_expert = nrm((DEPTH, MOE_GROUPS, EXPERTS_PER_GROUP), 0.01)
    moe_w_gate = nrm((DEPTH, N_EXPERTS, d, EXPERT_FF), d ** -0.5)
    moe_w_up = nrm((DEPTH, N_EXPERTS, d, EXPERT_FF), d ** -0.5)
    moe_w_down = nrm((DEPTH, N_EXPERTS, EXPERT_FF, d), EXPERT_FF ** -0.5)
    final_norm = 1.0 + nrm((d,), 0.02)
    return {
        'x_prompt': x_prompt, 'x_sample': x_sample,
        'state_pool': state_pool, 'state_conv': state_conv, 'state_lru': state_lru,
        'cache_kv_latent': cache_kv_latent, 'cache_k_rope': cache_k_rope, 'page_table': page_table,
        'c_prompt': c_prompt, 'c_sample': c_sample,
        'ada_w': ada_w, 'ada_b': ada_b, 'norm_mix': norm_mix, 'norm_ffn': norm_ffn,
        'ab_w_in': ab_w_in, 'ab_w_out': ab_w_out, 'pool_w': pool_w, 'pool_scale': pool_scale,
        'conv_w': conv_w, 'conv_b': conv_b, 'lru_w_r': lru_w_r, 'lru_b_r': lru_b_r,
        'lru_w_i': lru_w_i, 'lru_b_i': lru_b_i, 'lru_lambda': lru_lambda,
        'mla_w_in': mla_w_in, 'mla_q_norm': mla_q_norm, 'mla_w_uq': mla_w_uq, 'mla_kv_norm': mla_kv_norm,
        'mla_w_uk': mla_w_uk, 'mla_w_uv': mla_w_uv, 'mla_w_out': mla_w_out,
        'router_w_group': router_w_group, 'router_b_group': router_b_group,
        'router_w_expert': router_w_expert, 'router_b_expert': router_b_expert,
        'moe_w_gate': moe_w_gate, 'moe_w_up': moe_w_up, 'moe_w_down': moe_w_down,
        'final_norm': final_norm,
    }


def reference(x_prompt, x_sample, state_pool, state_conv, state_lru, cache_kv_latent, cache_k_rope, page_table,
              c_prompt, c_sample, ada_w, ada_b, norm_mix, norm_ffn, ab_w_in, ab_w_out, pool_w, pool_scale,
              conv_w, conv_b, lru_w_r, lru_b_r, lru_w_i, lru_b_i, lru_lambda,
              mla_w_in, mla_q_norm, mla_w_uq, mla_kv_norm, mla_w_uk, mla_w_uv, mla_w_out,
              router_w_group, router_b_group, router_w_expert, router_b_expert,
              moe_w_gate, moe_w_up, moe_w_down, final_norm):

    def trunk(x, c, start, pool_state, conv_state, lru_state, attend):
        bsz, t_len, _ = x.shape
        pos = start + jnp.arange(t_len, dtype=jnp.int32)
        pools, convs, lrus, latents, ropes = [], [], [], [], []
        for layer in range(DEPTH):
            mod = jax.nn.silu(c) @ ada_w[layer] + ada_b[layer]
            sh_m, sc_m, gt_m, sh_f, sc_f, gt_f = jnp.split(mod, 6, axis=-1)
            h = modulate(x, norm_mix[layer], sh_m, sc_m)
            if layer % 2 == 0:
                e = layer // 2
                proj = h @ ab_w_in[e]
                u_pool = proj[..., :POOL_WIDTH]
                u_x = proj[..., POOL_WIDTH:POOL_WIDTH + LRU_WIDTH]
                u_g = proj[..., POOL_WIDTH + LRU_WIDTH:]
                y_a, pool_new = pool_mixer(u_pool, pool_state[e], pos, pool_w[e], pool_scale[e])
                y_b, conv_new, h_last = rglru_mixer(u_x, u_g, conv_state[e], lru_state[e], conv_w[e], conv_b[e],
                                                    lru_w_r[e], lru_b_r[e], lru_w_i[e], lru_b_i[e], lru_lambda[e])
                mix = jnp.concatenate([y_a, y_b], axis=-1) @ ab_w_out[e]
                pools.append(pool_new)
                convs.append(conv_new)
                lrus.append(h_last)
            else:
                o = layer // 2
                q_nope, q_rope, c_kv, k_rope = mla_project(h, pos, mla_w_in[o], mla_q_norm[o], mla_w_uq[o], mla_kv_norm[o])
                mix = attend(o, q_nope, q_rope, c_kv, k_rope) @ mla_w_out[o]
                latents.append(c_kv)
                ropes.append(k_rope)
            x = x + gt_m[:, None] * mix
            hf = modulate(x, norm_ffn[layer], sh_f, sc_f).reshape(bsz * t_len, D_MODEL)
            expert, gate = hier_route(hf, router_w_group[layer], router_b_group[layer],
                                      router_w_expert[layer], router_b_expert[layer])
            ffn = moe_ffn(hf, expert, gate, moe_w_gate[layer], moe_w_up[layer], moe_w_down[layer])
            x = x + gt_f[:, None] * ffn.reshape(bsz, t_len, D_MODEL)
        return (rmsnorm(x, final_norm), jnp.stack(pools), jnp.stack(convs), jnp.stack(lrus),
                jnp.stack(latents), jnp.stack(ropes))

    def prompt_attend(o, q_nope, q_rope, c_kv, k_rope):
        return mla_prompt_attend(q_nope, q_rope, c_kv, k_rope, mla_w_uk[o], mla_w_uv[o])

    def sample_attend(o, q_nope, q_rope, c_kv, k_rope):
        return mla_sample_attend(q_nope, q_rope, c_kv, k_rope, cache_kv_latent, cache_k_rope, o, page_table,
                                 mla_w_uk[o], mla_w_uv[o])

    bp = x_prompt.shape[0]
    dt = x_prompt.dtype
    y_prompt, pool_p, conv_p, lru_p, lat_p, rope_p = trunk(
        x_prompt, c_prompt, 0,
        jnp.zeros((N_EVEN, bp, POOL_BUF, POOL_WIDTH), dt),
        jnp.zeros((N_EVEN, bp, CONV_WIDTH - 1, LRU_WIDTH), dt),
        jnp.zeros((N_EVEN, bp, LRU_WIDTH), dt),
        prompt_attend)
    past_len = page_table.shape[1] * PAGE_SIZE
    y_sample, pool_s, conv_s, lru_s, lat_s, rope_s = trunk(
        x_sample, c_sample, past_len, state_pool, state_conv, state_lru, sample_attend)
    return (y_prompt, y_sample, pool_p, pool_s, conv_p, conv_s, lru_p, lru_s, lat_p, lat_s, rope_p, rope_s)
```

```python
import functools

import jax
import jax.numpy as jnp
from jax import lax
from jax.experimental import pallas as pl
from jax.experimental.pallas import tpu as pltpu

F32 = jnp.float32
BF16 = jnp.bfloat16

EPS = 1e-6
POOL_WINDOWS = (2, 4, 8, 16)
POOL_HALO = 16
CONV_WIDTH = 4
CONV_HALO = 8
LRU_C = 8.0
MLA_HEADS = 16
QK_NOPE = 64
QK_ROPE = 32
V_DIM = 64
HEAD_PAD = 128
ROPE_THETA = 10000.0
SM_SCALE = (QK_NOPE + QK_ROPE) ** -0.5
MOE_GROUPS = 4
EXPERTS_PER_GROUP = 8
N_EXPERTS = MOE_GROUPS * EXPERTS_PER_GROUP
TOP_K = 2
LANES = 128
VMEM_LIMIT = 56 * 1024 * 1024


def _cparams(n_axes):
    return pltpu.CompilerParams(dimension_semantics=("arbitrary",) * n_axes,
                                vmem_limit_bytes=VMEM_LIMIT)


def _dot(a, b):
    return jnp.dot(a, b, preferred_element_type=F32)


def _dot_nt(a, b):
    return lax.dot_general(a, b, (((1,), (1,)), ((), ())), preferred_element_type=F32)


def _rms(x, gain):
    return x * lax.rsqrt(jnp.mean(x * x, axis=-1, keepdims=True) + EPS) * gain


def _silu(x):
    return x * jax.nn.sigmoid(x)


def _ada_kernel(c_ref, w_ref, b_ref, o_ref):
    a = _silu(c_ref[...]).astype(BF16)
    o_ref[0] = _dot(a, w_ref[0].astype(BF16)) + b_ref[0]


def _ada(c_all, ada_w, ada_b):
    depth, d, n6 = ada_w.shape
    rows = c_all.shape[0]
    tn = 1024
    return pl.pallas_call(
        _ada_kernel,
        grid=(depth, n6 // tn),
        in_specs=[pl.BlockSpec((rows, d), lambda l, j: (0, 0)),
                  pl.BlockSpec((1, d, tn), lambda l, j: (l, 0, j)),
                  pl.BlockSpec((1, 1, tn), lambda l, j: (l, 0, j))],
        out_specs=pl.BlockSpec((1, rows, tn), lambda l, j: (l, 0, j)),
        out_shape=jax.ShapeDtypeStruct((depth, rows, n6), F32),
        compiler_params=_cparams(2),
        name="ada_mod",
    )(c_all, ada_w, ada_b.reshape(depth, 1, n6))


def _mix0_kernel(x_ref, sh_ref, sc_ref, gt_ref, pool0_ref, conv0_ref, lru0_ref,
                 gain_ref, win_ref, poolw_ref, pscale_ref, convw_ref, convb_ref,
                 wri_ref, bri_ref, lam_ref, wout_ref,
                 x1_ref, pooln_ref, convn_ref, lrun_ref,
                 zp_ref, zc_ref, h_ref, *, tt, bb, start, n_t):
    t = pl.program_id(1)
    rows = tt * bb
    pw = zp_ref.shape[1]
    p0 = POOL_HALO * bb
    c0 = CONV_HALO * bb

    @pl.when(t == 0)
    def _():
        zp_ref[0:bb, :] = jnp.zeros((bb, pw), F32)
        zp_ref[bb:p0, :] = pool0_ref[0]
        zc_ref[0:c0 - (CONV_WIDTH - 1) * bb, :] = jnp.zeros((c0 - (CONV_WIDTH - 1) * bb, pw), F32)
        zc_ref[c0 - (CONV_WIDTH - 1) * bb:c0, :] = conv0_ref[0]
        h_ref[...] = lru0_ref[0]

    def per_row(v):
        return v if bb == 1 else jnp.concatenate([v] * tt, axis=0)

    x = x_ref[0]
    h = _rms(x, gain_ref[...]) * (1.0 + per_row(sc_ref[0])) + per_row(sh_ref[0])
    proj = _dot(h.astype(BF16), win_ref[...])
    u_pool = proj[:, :pw]
    u_x = proj[:, pw:2 * pw]
    u_g = proj[:, 2 * pw:]
    zp_ref[p0:p0 + rows, :] = u_pool
    zc_ref[c0:c0 + rows, :] = u_x

    if bb == 1:
        tix = lax.broadcasted_iota(jnp.int32, (rows, 1), 0)
    else:
        tix = jnp.concatenate([jnp.full((bb, 1), i, jnp.int32) for i in range(tt)], axis=0)
    pos = start + t * tt + tix

    gd = pw // len(POOL_WINDOWS)
    means = []
    for g, w in enumerate(POOL_WINDOWS):
        cols = slice(g * gd, (g + 1) * gd)
        acc = zp_ref[p0:p0 + rows, cols]
        for i in range(1, w):
            acc = acc + zp_ref[p0 - i * bb:p0 - i * bb + rows, cols]
        cnt = jnp.minimum(pos + 1, w).astype(F32)
        means.append(acc / cnt)
    pooled = jnp.concatenate(means, axis=-1) - u_pool
    y_a = _dot(pooled.astype(BF16), poolw_ref[...]) * pscale_ref[...]

    xc = convb_ref[...]
    for k in range(CONV_WIDTH):
        off = c0 - (CONV_WIDTH - 1 - k) * bb
        xc = xc + zc_ref[off:off + rows, :] * convw_ref[k:k + 1, :]
    pre = _dot(xc.astype(BF16), wri_ref[...]) + bri_ref[...]
    r = jax.nn.sigmoid(pre[:, :pw])
    gi = jax.nn.sigmoid(pre[:, pw:])
    lam = lam_ref[...]
    softplus_neg = jnp.maximum(-lam, 0.0) + jnp.log1p(jnp.exp(-jnp.abs(lam)))
    log_a = -LRU_C * r * softplus_neg
    a = jnp.exp(log_a)
    b = jnp.sqrt(1.0 - a * a) * gi * xc

    if bb == 1:
        rowi = lax.broadcasted_iota(jnp.int32, (rows, 1), 0)
        s = 1
        while s < rows:
            a_sh = pltpu.roll(a, s, 0)
            b_sh = pltpu.roll(b, s, 0)
            keep = rowi >= s
            b = jnp.where(keep, a * b_sh + b, b)
            a = jnp.where(keep, a * a_sh, a)
            s *= 2
        hs = b + a * h_ref[...]
        h_ref[...] = hs[rows - 1:rows, :]
    else:
        hprev = h_ref[...]
        parts = []
        for i in range(tt):
            hprev = a[i * bb:(i + 1) * bb] * hprev + b[i * bb:(i + 1) * bb]
            parts.append(hprev)
        hs = jnp.concatenate(parts, axis=0)
        h_ref[...] = hprev
    y_b = hs * jax.nn.gelu(u_g)

    mix = _dot(y_a.astype(BF16), wout_ref[0:pw, :]) + _dot(y_b.astype(BF16), wout_ref[pw:2 * pw, :])
    x1_ref[0] = x + per_row(gt_ref[0]) * mix

    @pl.when(t == n_t - 1)
    def _():
        pooln_ref[0] = zp_ref[p0 + rows - (POOL_HALO - 1) * bb:p0 + rows, :]
        convn_ref[0] = zc_ref[c0 + rows - (CONV_WIDTH - 1) * bb:c0 + rows, :]
        lrun_ref[0] = h_ref[...]

    if n_t > 1:
        @pl.when(t < n_t - 1)
        def _():
            zp_ref[bb:p0, :] = zp_ref[rows + bb:rows + p0, :]
            zc_ref[c0 - (CONV_WIDTH - 1) * bb:c0, :] = zc_ref[c0 + rows - (CONV_WIDTH - 1) * bb:c0 + rows, :]


def _mix0(x, sh, sc, gt, pool0, conv0, lru0, wts, *, tt, bb, start):
    nbb, tot, d = x.shape
    n_t = tot // (tt * bb)
    rows = tt * bb
    pw = pool0.shape[-1]
    hp = (POOL_HALO - 1) * bb
    hc = (CONV_WIDTH - 1) * bb

    def const(a):
        nd = a.ndim
        return pl.BlockSpec(a.shape, lambda i, j: (0,) * nd)

    def per_b(n_rows, width):
        return pl.BlockSpec((1, n_rows, width), lambda i, j: (i, 0, 0))

    kern = functools.partial(_mix0_kernel, tt=tt, bb=bb, start=start, n_t=n_t)
    return pl.pallas_call(
        kern,
        grid=(nbb, n_t),
        in_specs=[pl.BlockSpec((1, rows, d), lambda i, j: (i, j, 0)),
                  per_b(bb, d), per_b(bb, d), per_b(bb, d),
                  per_b(hp, pw), per_b(hc, pw), per_b(bb, pw)] + [const(w) for w in wts],
        out_specs=[pl.BlockSpec((1, rows, d), lambda i, j: (i, j, 0)),
                   per_b(hp, pw), per_b(hc, pw), per_b(bb, pw)],
        out_shape=[jax.ShapeDtypeStruct((nbb, tot, d), F32),
                   jax.ShapeDtypeStruct((nbb, hp, pw), F32),
                   jax.ShapeDtypeStruct((nbb, hc, pw), F32),
                   jax.ShapeDtypeStruct((nbb, bb, pw), F32)],
        scratch_shapes=[pltpu.VMEM(((POOL_HALO + tt) * bb, pw), F32),
                        pltpu.VMEM(((CONV_HALO + tt) * bb, pw), F32),
                        pltpu.VMEM((bb, pw), F32)],
        compiler_params=_cparams(2),
        name="mix0",
    )(x, sh, sc, gt, pool0, conv0, lru0, *wts)


def _route_kernel(*refs, with_proj):
    if with_proj:
        (x_ref, o_ref, wout_ref, gtm_ref, shf_ref, scf_ref, gain_ref, wrh_ref, wrl_ref, br_ref,
         x1_ref, hf_ref, info_ref, cnt_ref, run_ref) = refs
    else:
        (x_ref, shf_ref, scf_ref, gain_ref, wrh_ref, wrl_ref, br_ref,
         hf_ref, info_ref, cnt_ref, run_ref) = refs

    @pl.when(pl.program_id(0) == 0)
    def _():
        run_ref[...] = jnp.zeros(run_ref.shape, F32)

    x = x_ref[...]
    if with_proj:
        x = x + gtm_ref[0] * _dot(o_ref[...], wout_ref[...])
        x1_ref[...] = x
    hf = _rms(x, gain_ref[...]) * (1.0 + scf_ref[0]) + shf_ref[0]
    hf_ref[...] = hf

    hi = hf.astype(BF16)
    lo = (hf - hi.astype(F32)).astype(BF16)
    logits = _dot(hi, wrh_ref[...]) + _dot(lo, wrh_ref[...]) + _dot(hi, wrl_ref[...]) + br_ref[...]

    tq = logits.shape[0]
    lane = lax.broadcasted_iota(jnp.int32, (tq, LANES), 1).astype(F32)
    neg = -jnp.inf
    big = float(LANES)
    is_g = lane < MOE_GROUPS
    gl = jnp.where(is_g, logits, neg)
    mg = jnp.max(gl, axis=1, keepdims=True)
    gidx = jnp.min(jnp.where(gl == mg, lane, big), axis=1, keepdims=True)
    p_group = 1.0 / jnp.sum(jnp.where(is_g, jnp.exp(gl - mg), 0.0), axis=1, keepdims=True)
    first = MOE_GROUPS + gidx * EXPERTS_PER_GROUP
    el = jnp.where((lane >= first) & (lane < first + EXPERTS_PER_GROUP), logits, neg)
    v1 = jnp.max(el, axis=1, keepdims=True)
    i1 = jnp.min(jnp.where(el == v1, lane, big), axis=1, keepdims=True)
    el2 = jnp.where(lane == i1, neg, el)
    v2 = jnp.max(el2, axis=1, keepdims=True)
    i2 = jnp.min(jnp.where(el2 == v2, lane, big), axis=1, keepdims=True)
    ex = jnp.exp(v2 - v1)
    g1 = p_group / (1.0 + ex)
    g2 = p_group * ex / (1.0 + ex)
    e1 = i1 - MOE_GROUPS
    e2 = i2 - MOE_GROUPS

    oh1 = jnp.where(lane == e1, 1.0, 0.0)
    oh2 = jnp.where(lane == e2, 1.0, 0.0)
    oh = oh1 + oh2
    ri = lax.broadcasted_iota(jnp.int32, (tq, tq), 0)
    ci = lax.broadcasted_iota(jnp.int32, (tq, tq), 1)
    lower = jnp.where(ri > ci, 1.0, 0.0).astype(BF16)
    before = _dot(lower, oh.astype(BF16)) + run_ref[...]
    r1 = jnp.sum(before * oh1, axis=1, keepdims=True)
    r2 = jnp.sum(before * oh2, axis=1, keepdims=True)
    run_ref[...] = run_ref[...] + jnp.sum(oh, axis=0, keepdims=True)
    cnt_ref[...] = run_ref[...]

    info = jnp.where(lane == 0, e1, jnp.where(lane == 1, e2, jnp.where(lane == 2, g1, jnp.where(
        lane == 3, g2, jnp.where(lane == 4, r1, jnp.where(lane == 5, r2, 0.0))))))
    info_ref[...] = info


def _route(x, shf, scf, gain, wrh, wrl, br, *, tile, rows_per_mod, proj=None):
    n, d = x.shape
    nt = n // tile
    mrows = shf.shape[1]
    mod_spec = pl.BlockSpec((1, mrows, d), lambda i: ((i * tile) // rows_per_mod, 0, 0))
    row_spec = pl.BlockSpec((tile, d), lambda i: (i, 0))

    def const(a):
        nd = a.ndim
        return pl.BlockSpec(a.shape, lambda i: (0,) * nd)

    info_spec = pl.BlockSpec((tile, LANES), lambda i: (i, 0))
    cnt_spec = pl.BlockSpec((1, LANES), lambda i: (0, 0))
    outs_shape = [jax.ShapeDtypeStruct((n, d), F32), jax.ShapeDtypeStruct((n, LANES), F32),
                  jax.ShapeDtypeStruct((1, LANES), F32)]
    outs_spec = [row_spec, info_spec, cnt_spec]
    if proj is None:
        ins = [x, shf, scf, gain, wrh, wrl, br]
        in_specs = [row_spec, mod_spec, mod_spec, const(gain), const(wrh), const(wrl), const(br)]
    else:
        o, wout, gtm = proj
        ins = [x, o, wout, gtm, shf, scf, gain, wrh, wrl, br]
        in_specs = [row_spec, pl.BlockSpec((tile, o.shape[1]), lambda i: (i, 0)), const(wout), mod_spec,
                    mod_spec, mod_spec, const(gain), const(wrh), const(wrl), const(br)]
        outs_shape = [jax.ShapeDtypeStruct((n, d), F32)] + outs_shape
        outs_spec = [row_spec] + outs_spec
    return pl.pallas_call(
        functools.partial(_route_kernel, with_proj=proj is not None),
        grid=(nt,),
        in_specs=in_specs,
        out_specs=outs_spec,
        out_shape=outs_shape,
        scratch_shapes=[pltpu.VMEM((1, LANES), F32)],
        compiler_params=_cparams(1),
        name="route",
    )(*ins)


def _dispatch_kernel(slot_ref, hf_ref, xs_ref, sem, *, tile):
    def row_copy(r, s):
        return pltpu.make_async_copy(hf_ref.at[pl.ds(r, 1), :], xs_ref.at[pl.ds(s, 1), :], sem)

    def issue(r, c):
        for k in range(TOP_K):
            row_copy(r, slot_ref[0, 0, TOP_K * r + k]).start()
        return c

    lax.fori_loop(0, tile, issue, 0)

    def drain(r, c):
        for k in range(TOP_K):
            row_copy(r, slot_ref[0, 0, TOP_K * r + k]).wait()
        return c

    lax.fori_loop(0, tile, drain, 0)


def _dispatch(hf, slots, *, tile):
    n, d = hf.shape
    nt = n // tile
    return pl.pallas_call(
        functools.partial(_dispatch_kernel, tile=tile),
        grid=(nt,),
        in_specs=[pl.BlockSpec((1, 1, TOP_K * tile), lambda i: (i, 0, 0), memory_space=pltpu.SMEM),
                  pl.BlockSpec((tile, d), lambda i: (i, 0))],
        out_specs=pl.BlockSpec(memory_space=pl.ANY),
        out_shape=jax.ShapeDtypeStruct((TOP_K * n, d), F32),
        scratch_shapes=[pltpu.SemaphoreType.DMA(())],
        compiler_params=_cparams(1),
        name="moe_dispatch",
    )(slots.reshape(nt, 1, TOP_K * tile), hf)


def _gmm_kernel(wt_ref, we_ref, wlo_ref, whi_ref, xs_ref, wg_ref, wu_ref, wd_ref, y_ref,
                wgb_ref, wub_ref, wdb_ref):
    w = pl.program_id(0)
    prev = jnp.maximum(w - 1, 0)
    new_expert = (w == 0) | (we_ref[w] != we_ref[prev])
    new_tile = (w == 0) | (wt_ref[w] != wt_ref[prev])

    @pl.when(new_expert)
    def _():
        wgb_ref[...] = wg_ref[0].astype(BF16)
        wub_ref[...] = wu_ref[0].astype(BF16)
        wdb_ref[...] = wd_ref[0].astype(BF16)

    @pl.when(new_tile)
    def _():
        y_ref[...] = jnp.zeros(y_ref.shape, F32)

    lo = wlo_ref[w]
    hi = whi_ref[w]

    @pl.when(hi > lo)
    def _():
        x = xs_ref[...].astype(BF16)
        g = _dot(x, wgb_ref[...])
        u = _dot(x, wub_ref[...])
        yv = _dot((_silu(g) * u).astype(BF16), wdb_ref[...])
        row = lax.broadcasted_iota(jnp.int32, (x.shape[0], 1), 0)
        y_ref[...] = y_ref[...] + jnp.where((row >= lo) & (row < hi), yv, 0.0)


def _gmm(xs, work, w_gate, w_up, w_down, *, tile):
    m, d = xs.shape
    ff = w_gate.shape[-1]
    n_work = work[0].shape[0]
    grid_spec = pltpu.PrefetchScalarGridSpec(
        num_scalar_prefetch=4,
        grid=(n_work,),
        in_specs=[pl.BlockSpec((tile, d), lambda w, wt, we, wlo, whi: (wt[w], 0)),
                  pl.BlockSpec((1, d, ff), lambda w, wt, we, wlo, whi: (we[w], 0, 0)),
                  pl.BlockSpec((1, d, ff), lambda w, wt, we, wlo, whi: (we[w], 0, 0)),
                  pl.BlockSpec((1, ff, d), lambda w, wt, we, wlo, whi: (we[w], 0, 0))],
        out_specs=pl.BlockSpec((tile, d), lambda w, wt, we, wlo, whi: (wt[w], 0)),
        scratch_shapes=[pltpu.VMEM((d, ff), BF16), pltpu.VMEM((d, ff), BF16), pltpu.VMEM((ff, d), BF16)],
    )
    return pl.pallas_call(
        _gmm_kernel,
        grid_spec=grid_spec,
        out_shape=jax.ShapeDtypeStruct((m, d), F32),
        compiler_params=_cparams(1),
        name="moe_gmm",
    )(*work, xs, w_gate, w_up, w_down)


def _work_list(counts, n_slots, tile):
    n_tiles = n_slots // tile
    n_work = n_tiles + N_EXPERTS - 1
    ends = jnp.cumsum(counts)
    starts = ends - counts
    first_tile = starts // tile
    last_tile = jnp.maximum(ends - 1, 0) // tile
    n_items = jnp.where(counts > 0, last_tile - first_tile + 1, 0)
    item_end = jnp.cumsum(n_items)
    item_start = item_end - n_items
    w = jnp.arange(n_work, dtype=jnp.int32)
    used = w < item_end[-1]
    e = jnp.minimum(jnp.searchsorted(item_end, w, side="right"), N_EXPERTS - 1).astype(jnp.int32)
    t = first_tile[e] + (w - item_start[e])
    lo = jnp.maximum(starts[e], t * tile) - t * tile
    hi = jnp.minimum(ends[e], (t + 1) * tile) - t * tile
    last_e = jnp.max(jnp.where(counts > 0, jnp.arange(N_EXPERTS), 0)).astype(jnp.int32)
    wt = jnp.where(used, t, n_tiles - 1).astype(jnp.int32)
    we = jnp.where(used, e, last_e).astype(jnp.int32)
    wlo = jnp.where(used, lo, 0).astype(jnp.int32)
    whi = jnp.where(used, hi, 0).astype(jnp.int32)
    return (wt, we, wlo, whi), starts


def _combine_kernel(*refs, tile, final):
    if final:
        slot_ref, x_ref, gt_ref, info_ref, fg_ref, y_ref, out_ref, ybuf, sem = refs
    else:
        slot_ref, x_ref, gt_ref, info_ref, y_ref, out_ref, ybuf, sem = refs

    def row_copy(r, k, s):
        return pltpu.make_async_copy(y_ref.at[pl.ds(s, 1), :], ybuf.at[k, pl.ds(r, 1), :], sem)

    def issue(r, c):
        for k in range(TOP_K):
            row_copy(r, k, slot_ref[0, 0, TOP_K * r + k]).start()
        return c

    lax.fori_loop(0, tile, issue, 0)

    def drain(r, c):
        for k in range(TOP_K):
            row_copy(r, k, slot_ref[0, 0, TOP_K * r + k]).wait()
        return c

    lax.fori_loop(0, tile, drain, 0)

    info = info_ref[...]
    ffn = info[:, 2:3] * ybuf[0] + info[:, 3:4] * ybuf[1]
    out = x_ref[...] + gt_ref[0] * ffn
    if final:
        out = _rms(out, fg_ref[...])
    out_ref[...] = out


def _combine(x, gt, info, y, slots, *, tile, rows_per_mod, final_gain=None):
    n, d = x.shape
    nt = n // tile
    mrows = gt.shape[1]
    final = final_gain is not None
    in_specs = [pl.BlockSpec((1, 1, TOP_K * tile), lambda i: (i, 0, 0), memory_space=pltpu.SMEM),
                pl.BlockSpec((tile, d), lambda i: (i, 0)),
                pl.BlockSpec((1, mrows, d), lambda i: ((i * tile) // rows_per_mod, 0, 0)),
                pl.BlockSpec((tile, LANES), lambda i: (i, 0))]
    ins = [slots.reshape(nt, 1, TOP_K * tile), x, gt, info]
    if final:
        in_specs.append(pl.BlockSpec(final_gain.shape, lambda i: (0, 0)))
        ins.append(final_gain)
    in_specs.append(pl.BlockSpec(memory_space=pl.ANY))
    ins.append(y)
    return pl.pallas_call(
        functools.partial(_combine_kernel, tile=tile, final=final),
        grid=(nt,),
        in_specs=in_specs,
        out_specs=pl.BlockSpec((tile, d), lambda i: (i, 0)),
        out_shape=jax.ShapeDtypeStruct((n, d), F32),
        scratch_shapes=[pltpu.VMEM((TOP_K, tile, d), F32), pltpu.SemaphoreType.DMA(())],
        compiler_params=_cparams(1),
        name="moe_combine",
    )(*ins)


def _moe(x, mods, rw, experts, *, tile, gmm_tile, rows_per_mod, proj=None, final_gain=None):
    shf, scf, gtf = mods
    gain, wrh, wrl, br = rw
    res = _route(x, shf, scf, gain, wrh, wrl, br, tile=tile, rows_per_mod=rows_per_mod, proj=proj)
    if proj is not None:
        x, hf, info, cnt = res
    else:
        hf, info, cnt = res
    n = x.shape[0]
    counts = cnt[0, :N_EXPERTS].astype(jnp.int32)
    work, starts = _work_list(counts, TOP_K * n, gmm_tile)
    eid = info[:, 0:TOP_K].astype(jnp.int32)
    rank = info[:, 4:4 + TOP_K].astype(jnp.int32)
    slots = (starts.astype(jnp.int32)[eid] + rank).reshape(-1)
    xs = _dispatch(hf, slots, tile=tile)
    y = _gmm(xs, work, *experts, tile=gmm_tile)
    return _combine(x, gtf, info, y, slots, tile=tile, rows_per_mod=rows_per_mod, final_gain=final_gain)


def _rope_turn(blk, c, s):
    return blk * c + pltpu.roll(blk, LANES - QK_ROPE, 1) * s


def _mla_proj_kernel(*refs, sample):
    if sample:
        (x_ref, sh_ref, sc_ref, gain_ref, win_ref, qg_ref, wuq_ref, kvg_ref, c_ref, s_ref,
         ckv_ref, kr_ref, q_ref) = refs
    else:
        (x_ref, sh_ref, sc_ref, gain_ref, win_ref, qg_ref, wuq_ref, kvg_ref, c_ref, s_ref, wuk_ref, wuv_ref,
         ckv_ref, kr_ref, q_ref, k_ref, v_ref) = refs
    q_lora = qg_ref.shape[1]
    kv_lora = kvg_ref.shape[1]
    h = _rms(x_ref[...], gain_ref[...]) * (1.0 + sc_ref[0]) + sh_ref[0]
    proj = _dot(h.astype(BF16), win_ref[...])
    qn = _rms(proj[:, :q_lora], qg_ref[...])
    ckv = _rms(proj[:, q_lora:q_lora + kv_lora], kvg_ref[...])
    c = c_ref[...]
    s = s_ref[...]
    kf = _rope_turn(proj[:, q_lora + kv_lora:], c, s)
    ckv_ref[...] = ckv
    kr_ref[...] = kf[:, QK_NOPE:QK_NOPE + QK_ROPE]
    q = _dot(qn.astype(BF16), wuq_ref[...])
    if sample:
        for hh in range(MLA_HEADS):
            cols = slice(hh * HEAD_PAD, (hh + 1) * HEAD_PAD)
            q_ref[:, cols] = _rope_turn(q[:, cols], c, s) * SM_SCALE
    else:
        ckv_b = ckv.astype(BF16)
        kn = _dot(ckv_b, wuk_ref[...])
        v_ref[...] = _dot(ckv_b, wuv_ref[...]).astype(BF16)
        for hh in range(MLA_HEADS):
            cols = slice(hh * HEAD_PAD, (hh + 1) * HEAD_PAD)
            q_ref[:, cols] = (_rope_turn(q[:, cols], c, s) * SM_SCALE).astype(BF16)
            k_ref[:, cols] = (kn[:, cols] + kf).astype(BF16)


def _mla_proj(x, sh, sc, gain, win, qg, wuq, kvg, ctab, stab, *, tile, rows_per_mod, tab_tiles, kv_w=None):
    n, d = x.shape
    nt = n // tile
    mrows = sh.shape[1]
    sample = kv_w is None
    kv_lora = kvg.shape[1]

    def const(a):
        nd = a.ndim
        return pl.BlockSpec(a.shape, lambda i: (0,) * nd)

    row = lambda width: pl.BlockSpec((tile, width), lambda i: (i, 0))
    mod_spec = pl.BlockSpec((1, mrows, d), lambda i: ((i * tile) // rows_per_mod, 0, 0))
    tab_spec = pl.BlockSpec((tile, LANES), lambda i: (i % tab_tiles, 0))
    ins = [x, sh, sc, gain, win, qg, wuq, kvg, ctab, stab]
    in_specs = [row(d), mod_spec, mod_spec, const(gain), const(win), const(qg), const(wuq), const(kvg),
                tab_spec, tab_spec]
    qw = MLA_HEADS * HEAD_PAD
    out_shape = [jax.ShapeDtypeStruct((n, kv_lora), F32), jax.ShapeDtypeStruct((n, QK_ROPE), F32)]
    out_specs = [row(kv_lora), row(QK_ROPE)]
    if sample:
        out_shape.append(jax.ShapeDtypeStruct((n, qw), F32))
        out_specs.append(row(qw))
    else:
        wuk, wuv = kv_w
        ins += [wuk, wuv]
        in_specs += [const(wuk), const(wuv)]
        out_shape += [jax.ShapeDtypeStruct((n, qw), BF16), jax.ShapeDtypeStruct((n, qw), BF16),
                      jax.ShapeDtypeStruct((n, MLA_HEADS * V_DIM), BF16)]
        out_specs += [row(qw), row(qw), row(MLA_HEADS * V_DIM)]
    return pl.pallas_call(
        functools.partial(_mla_proj_kernel, sample=sample),
        grid=(nt,),
        in_specs=in_specs,
        out_specs=out_specs,
        out_shape=out_shape,
        compiler_params=_cparams(1),
        name="mla_proj",
    )(*ins)


def _attn_kernel(q_ref, k_ref, v_ref, o_ref, *, tq, tk):
    i = pl.program_id(2)
    v_lane = lax.broadcasted_iota(jnp.int32, (tq, 2 * V_DIM), 1)
    outs = []
    for hh in range(2):
        cols = slice(hh * HEAD_PAD, (hh + 1) * HEAD_PAD)
        q = q_ref[0, :, cols]

        def step(j, carry, masked):
            m, l, acc = carry
            k = k_ref[0, pl.ds(pl.multiple_of(j * tk, tk), tk), cols]
            v = v_ref[0, pl.ds(pl.multiple_of(j * tk, tk), tk), :]
            s = _dot_nt(q, k)
            if masked:
                qpos = i * tq + lax.broadcasted_iota(jnp.int32, (tq, tk), 0)
                kpos = j * tk + lax.broadcasted_iota(jnp.int32, (tq, tk), 1)
                s = jnp.where(kpos <= qpos, s, -jnp.inf)
            m_new = jnp.maximum(m, jnp.max(s, axis=1, keepdims=True))
            alpha = jnp.exp(m - m_new)
            p = jnp.exp(s - m_new)
            l = alpha * l + jnp.sum(p, axis=1, keepdims=True)
            acc = alpha * acc + _dot(p.astype(BF16), v)
            return m_new, l, acc

        init = (jnp.full((tq, 1), -jnp.inf, F32), jnp.zeros((tq, 1), F32), jnp.zeros((tq, 2 * V_DIM), F32))
        n_full = (i * tq) // tk
        carry = lax.fori_loop(0, n_full, lambda j, c: step(j, c, False), init)
        for jj in range(tq // tk):
            carry = step(n_full + jj, carry, True)
        _, l, acc = carry
        outs.append(acc / l)
    o_ref[0] = jnp.where(v_lane < V_DIM, outs[0], outs[1]).astype(o_ref.dtype)


def _attention(q, k, v, *, tq, tk):
    b, s, _ = q.shape
    pairs = MLA_HEADS // 2
    return pl.pallas_call(
        functools.partial(_attn_kernel, tq=tq, tk=tk),
        grid=(b, pairs, s // tq),
        in_specs=[pl.BlockSpec((1, tq, 2 * HEAD_PAD), lambda bi, p, i: (bi, i, p)),
                  pl.BlockSpec((1, s, 2 * HEAD_PAD), lambda bi, p, i: (bi, 0, p)),
                  pl.BlockSpec((1, s, 2 * V_DIM), lambda bi, p, i: (bi, 0, p))],
        out_specs=pl.BlockSpec((1, tq, 2 * V_DIM), lambda bi, p, i: (bi, i, p)),
        out_shape=jax.ShapeDtypeStruct((b, s, MLA_HEADS * V_DIM), BF16),
        compiler_params=_cparams(3),
        name="prompt_attn",
    )(q, k, v)


def _qabs_kernel(q_ref, m_ref, o_ref):
    o_ref[0] = _dot(q_ref[...].astype(BF16), m_ref[0])


def _qabs(q, mats):
    n = q.shape[0]
    width = mats.shape[-1]
    return pl.pallas_call(
        _qabs_kernel,
        grid=(MLA_HEADS,),
        in_specs=[pl.BlockSpec((n, HEAD_PAD), lambda h: (0, h)),
                  pl.BlockSpec((1, HEAD_PAD, width), lambda h: (h, 0, 0))],
        out_specs=pl.BlockSpec((1, n, width), lambda h: (h, 0, 0)),
        out_shape=jax.ShapeDtypeStruct((MLA_HEADS, n, width), F32),
        compiler_params=_cparams(1),
        name="sample_qabs",
    )(q, mats)


def _paged_attn_kernel(pt_ref, q_ref, cn_ref, rn_ref, ck_hbm, kr_hbm, o_ref,
                       ckbuf, krbuf, m_ref, l_ref, acc_ref, sem, *, pages, page, n_chunks, t_new, t_pad):
    b = pl.program_id(0)
    c = pl.program_id(1)
    n_b = pl.num_programs(0)
    step = b * n_chunks + c
    slot = step % 2
    kv_lora = ckbuf.shape[-1]

    def page_copies(bi, ci, sl, p):
        phys = pt_ref[bi, ci * pages + p]
        return (pltpu.make_async_copy(ck_hbm.at[phys], ckbuf.at[sl, pl.ds(p * page, page), :], sem.at[0, sl]),
                pltpu.make_async_copy(kr_hbm.at[phys], krbuf.at[sl, pl.ds(p * page, page), :], sem.at[1, sl]))

    def start_chunk(bi, ci, sl):
        for p in range(pages):
            for cp in page_copies(bi, ci, sl, p):
                cp.start()

    @pl.when(step == 0)
    def _():
        start_chunk(b, c, slot)

    nxt = step + 1

    @pl.when(nxt < n_b * n_chunks)
    def _():
        start_chunk(nxt // n_chunks, nxt % n_chunks, 1 - slot)

    for p in range(pages):
        for cp in page_copies(b, c, slot, p):
            cp.wait()

    @pl.when(c == 0)
    def _():
        m_ref[...] = jnp.full(m_ref.shape, -jnp.inf, F32)
        l_ref[...] = jnp.zeros(l_ref.shape, F32)
        acc_ref[...] = jnp.zeros(acc_ref.shape, F32)

    qa = q_ref[0]
    q_lat = qa[:, :kv_lora].astype(BF16)
    q_rope = qa[:, kv_lora:kv_lora + QK_ROPE].astype(BF16)

    def update(ck, kr, mask):
        s = _dot_nt(q_lat, ck) + _dot_nt(q_rope, kr)
        if mask is not None:
            s = jnp.where(mask, s, -jnp.inf)
        m = m_ref[...]
        m_new = jnp.maximum(m, jnp.max(s, axis=1, keepdims=True))
        alpha = jnp.exp(m - m_new)
        p = jnp.exp(s - m_new)
        l_ref[...] = alpha * l_ref[...] + jnp.sum(p, axis=1, keepdims=True)
        acc_ref[...] = alpha * acc_ref[...] + _dot(p.astype(BF16), ck)
        m_ref[...] = m_new

    update(ckbuf[slot].astype(BF16), krbuf[slot].astype(BF16), None)

    @pl.when(c == n_chunks - 1)
    def _():
        rows = qa.shape[0]
        t_q = lax.broadcasted_iota(jnp.int32, (rows, t_pad), 0) % t_new
        t_k = lax.broadcasted_iota(jnp.int32, (rows, t_pad), 1)
        update(cn_ref[0].astype(BF16), rn_ref[0].astype(BF16), t_k <= t_q)
        o_ref[0] = acc_ref[...] / l_ref[...]


def _paged_attn(page_table, qabs, ckv_new, kr_new, cache_ck, cache_kr, *, pages, t_new):
    b, rows, width = qabs.shape
    n_pages = page_table.shape[1]
    n_chunks = n_pages // pages
    page, kv_lora = cache_ck.shape[1:]
    t_pad = ckv_new.shape[1]
    grid_spec = pltpu.PrefetchScalarGridSpec(
        num_scalar_prefetch=1,
        grid=(b, n_chunks),
        in_specs=[pl.BlockSpec((1, rows, width), lambda bi, ci, pt: (bi, 0, 0)),
                  pl.BlockSpec((1, t_pad, kv_lora), lambda bi, ci, pt: (bi, 0, 0)),
                  pl.BlockSpec((1, t_pad, QK_ROPE), lambda bi, ci, pt: (bi, 0, 0)),
                  pl.BlockSpec(memory_space=pl.ANY),
                  pl.BlockSpec(memory_space=pl.ANY)],
        out_specs=pl.BlockSpec((1, rows, kv_lora), lambda bi, ci, pt: (bi, 0, 0)),
        scratch_shapes=[pltpu.VMEM((2, pages * page, kv_lora), F32),
                        pltpu.VMEM((2, pages * page, QK_ROPE), F32),
                        pltpu.VMEM((rows, 1), F32), pltpu.VMEM((rows, 1), F32),
                        pltpu.VMEM((rows, kv_lora), F32),
                        pltpu.SemaphoreType.DMA((2, 2))],
    )
    return pl.pallas_call(
        functools.partial(_paged_attn_kernel, pages=pages, page=page, n_chunks=n_chunks, t_new=t_new, t_pad=t_pad),
        grid_spec=grid_spec,
        out_shape=jax.ShapeDtypeStruct((b, rows, kv_lora), F32),
        compiler_params=_cparams(2),
        name="paged_attn",
    )(page_table, qabs, ckv_new, kr_new, cache_ck, cache_kr)


def _vup_kernel(o_ref, w_ref, out_ref):
    lat = jnp.concatenate([o_ref[0], o_ref[1]], axis=-1).astype(BF16)
    out_ref[...] = _dot(lat, w_ref[0]).astype(out_ref.dtype)


def _vup(o_lat, w_pairs):
    h, n, c = o_lat.shape
    return pl.pallas_call(
        _vup_kernel,
        grid=(h // 2,),
        in_specs=[pl.BlockSpec((2, n, c), lambda p: (p, 0, 0)),
                  pl.BlockSpec((1, 2 * c, 2 * V_DIM), lambda p: (p, 0, 0))],
        out_specs=pl.BlockSpec((n, 2 * V_DIM), lambda p: (0, p)),
        out_shape=jax.ShapeDtypeStruct((n, h * V_DIM), BF16),
        compiler_params=_cparams(1),
        name="sample_vup",
    )(o_lat, w_pairs)


def _block_diag(w):
    g, a, b = w.shape
    out = jnp.zeros((g * a, g * b), w.dtype)
    for i in range(g):
        out = out.at[i * a:(i + 1) * a, i * b:(i + 1) * b].set(w[i])
    return out


def _swap_halves(w):
    half = w.shape[-1] // 2
    return jnp.concatenate([w[..., half:], w[..., :half]], axis=-1)


def _rope_tables(pos):
    half = QK_ROPE // 2
    inv_freq = ROPE_THETA ** (-jnp.arange(half, dtype=F32) / half)
    ang = pos.astype(F32)[:, None] * inv_freq[None, :]
    cos, sin = jnp.cos(ang), jnp.sin(ang)
    n = pos.shape[0]
    ctab = jnp.concatenate([jnp.ones((n, QK_NOPE), F32), cos, cos, jnp.zeros((n, QK_ROPE), F32)], axis=1)
    stab = jnp.concatenate([jnp.zeros((n, QK_NOPE), F32), -sin, sin, jnp.zeros((n, QK_ROPE), F32)], axis=1)
    return ctab, stab


def _router_weights(w_group, b_group, w_expert, b_expert):
    d = w_group.shape[0]
    we = jnp.transpose(w_expert, (1, 0, 2)).reshape(d, N_EXPERTS)
    w = jnp.concatenate([w_group, we, jnp.zeros((d, LANES - MOE_GROUPS - N_EXPERTS), F32)], axis=1)
    bias = jnp.concatenate([b_group, b_expert.reshape(-1), jnp.zeros((LANES - MOE_GROUPS - N_EXPERTS,), F32)])
    hi = w.astype(BF16)
    lo = (w - hi.astype(F32)).astype(BF16)
    return hi, lo, bias.reshape(1, LANES)


def _pick(n, pref):
    t = min(n, pref)
    while n % t:
        t //= 2
    return t


def kernel(x_prompt, x_sample, state_pool, state_conv, state_lru, cache_kv_latent, cache_k_rope, page_table,
           c_prompt, c_sample, ada_w, ada_b, norm_mix, norm_ffn, ab_w_in, ab_w_out, pool_w, pool_scale,
           conv_w, conv_b, lru_w_r, lru_b_r, lru_w_i, lru_b_i, lru_lambda,
           mla_w_in, mla_q_norm, mla_w_uq, mla_kv_norm, mla_w_uk, mla_w_uv, mla_w_out,
           router_w_group, router_b_group, router_w_expert, router_b_expert,
           moe_w_gate, moe_w_up, moe_w_down, final_norm):
    bp, seq, d = x_prompt.shape
    bs, t_new, _ = x_sample.shape
    depth = ada_w.shape[0]
    pw = pool_scale.shape[-1]
    page = cache_kv_latent.shape[2]
    past_len = page_table.shape[1] * page
    q_lora = mla_q_norm.shape[-1]
    kv_lora = mla_kv_norm.shape[-1]

    n_c = bp + bs
    n_c_pad = -(-n_c // 8) * 8
    c_all = jnp.concatenate([c_prompt, c_sample, jnp.zeros((n_c_pad - n_c, d), F32)], axis=0)
    mod = _ada(c_all, ada_w, ada_b)

    def mods(layer, lo, hi):
        return [mod[layer, lo:hi, k * d:(k + 1) * d] for k in range(6)]

    n_p = bp * seq
    n_s = bs * t_new
    tile_p = _pick(seq, 512)
    tile_s = _pick(n_s, 512)
    gmm_tile_p = _pick(TOP_K * n_p, 256)
    gmm_tile_s = _pick(TOP_K * n_s, 256)

    xp = x_prompt.reshape(n_p, d)
    xs = x_sample.reshape(n_s, d)
    outs = {}

    for layer in range(depth):
        sh_m, sc_m, gt_m, sh_f, sc_f, gt_f = mods(layer, 0, bp)
        sh_ms, sc_ms, gt_ms, sh_fs, sc_fs, gt_fs = mods(layer, bp, bp + bs)
        rep = lambda v: jnp.repeat(v, t_new, axis=0)[None]
        per_seq = lambda v: v[:, None, :]
        rw = (norm_ffn[layer].reshape(1, d),) + _router_weights(
            router_w_group[layer], router_b_group[layer], router_w_expert[layer], router_b_expert[layer])
        experts = (moe_w_gate[layer], moe_w_up[layer], moe_w_down[layer])
        final_gain = final_norm.reshape(1, d) if layer == depth - 1 else None
        ffn_mods_p = (per_seq(sh_f), per_seq(sc_f), per_seq(gt_f))
        ffn_mods_s = (rep(sh_fs), rep(sc_fs), rep(gt_fs))

        if layer % 2 == 0:
            e = layer // 2
            wts = (norm_mix[layer].reshape(1, d), ab_w_in[e].astype(BF16), _block_diag(pool_w[e]).astype(BF16),
                   pool_scale[e].reshape(1, pw), conv_w[e], conv_b[e].reshape(1, pw),
                   jnp.concatenate([_block_diag(lru_w_r[e]), _block_diag(lru_w_i[e])], axis=1).astype(BF16),
                   jnp.concatenate([lru_b_r[e], lru_b_i[e]]).reshape(1, 2 * pw),
                   lru_lambda[e].reshape(1, pw), ab_w_out[e].astype(BF16))
            x1p, pool_p, conv_p, lru_p = _mix0(
                xp.reshape(bp, seq, d), per_seq(sh_m), per_seq(sc_m), per_seq(gt_m),
                jnp.zeros((bp, POOL_HALO - 1, pw), F32), jnp.zeros((bp, CONV_WIDTH - 1, pw), F32),
                jnp.zeros((bp, 1, pw), F32), wts, tt=tile_p, bb=1, start=0)
            tm = lambda a: jnp.swapaxes(a, 0, 1).reshape(1, -1, a.shape[-1])
            x1s, pool_s, conv_s, lru_s = _mix0(
                tm(xs.reshape(bs, t_new, d)), sh_ms[None], sc_ms[None], gt_ms[None],
                tm(state_pool[e]), tm(state_conv[e]), state_lru[e][None], wts, tt=t_new, bb=bs, start=past_len)
            bm = lambda a, n: jnp.swapaxes(a.reshape(n, bs, a.shape[-1]), 0, 1)
            outs.setdefault("pool_p", []).append(pool_p)
            outs.setdefault("pool_s", []).append(bm(pool_s, POOL_HALO - 1))
            outs.setdefault("conv_p", []).append(conv_p)
            outs.setdefault("conv_s", []).append(bm(conv_s, CONV_WIDTH - 1))
            outs.setdefault("lru_p", []).append(lru_p.reshape(bp, pw))
            outs.setdefault("lru_s", []).append(lru_s.reshape(bs, pw))
            xp = x1p.reshape(n_p, d)
            xs = bm(x1s, t_new).reshape(n_s, d)
            proj_p = proj_s = None
        else:
            o = layer // 2
            gain = norm_mix[layer].reshape(1, d)
            w_in = mla_w_in[o]
            w_kr = w_in[:, q_lora + kv_lora:]
            win = jnp.concatenate([w_in[:, :q_lora + kv_lora], jnp.zeros((d, QK_NOPE), F32), w_kr,
                                   _swap_halves(w_kr)], axis=1).astype(BF16)
            wq = mla_w_uq[o].reshape(q_lora, MLA_HEADS, QK_NOPE + QK_ROPE)
            wuq = jnp.concatenate([wq, _swap_halves(wq[..., QK_NOPE:])], axis=-1).reshape(
                q_lora, MLA_HEADS * HEAD_PAD).astype(BF16)
            wuk = jnp.concatenate([mla_w_uk[o], jnp.zeros((kv_lora, MLA_HEADS, HEAD_PAD - QK_NOPE), F32)],
                                  axis=-1).reshape(kv_lora, MLA_HEADS * HEAD_PAD).astype(BF16)
            wuv = mla_w_uv[o].reshape(kv_lora, MLA_HEADS * V_DIM).astype(BF16)
            qg = mla_q_norm[o].reshape(1, q_lora)
            kvg = mla_kv_norm[o].reshape(1, kv_lora)
            wout = mla_w_out[o].astype(BF16)

            ctab, stab = _rope_tables(jnp.arange(seq, dtype=jnp.int32))
            ckv_p, kr_p, q_p, k_p, v_p = _mla_proj(
                xp, per_seq(sh_m), per_seq(sc_m), gain, win, qg, wuq, kvg, ctab, stab,
                tile=tile_p, rows_per_mod=seq, tab_tiles=seq // tile_p, kv_w=(wuk, wuv))
            qw = MLA_HEADS * HEAD_PAD
            ta = _pick(seq, 512)
            o_p = _attention(q_p.reshape(bp, seq, qw), k_p.reshape(bp, seq, qw),
                             v_p.reshape(bp, seq, MLA_HEADS * V_DIM), tq=ta, tk=ta)
            proj_p = (o_p.reshape(n_p, MLA_HEADS * V_DIM), wout, per_seq(gt_m))
            outs.setdefault("lat_p", []).append(ckv_p.reshape(bp, seq, kv_lora))
            outs.setdefault("rope_p", []).append(kr_p.reshape(bp, seq, QK_ROPE))

            pos_s = jnp.tile(past_len + jnp.arange(t_new, dtype=jnp.int32), bs)
            ctab_s, stab_s = _rope_tables(pos_s)
            ckv_s, kr_s, q_s = _mla_proj(
                xs, rep(sh_ms), rep(sc_ms), gain, win, qg, wuq, kvg, ctab_s, stab_s,
                tile=tile_s, rows_per_mod=n_s, tab_tiles=n_s // tile_s)
            aw = kv_lora + HEAD_PAD
            wk_t = jnp.transpose(mla_w_uk[o], (1, 2, 0))
            mats = jnp.zeros((MLA_HEADS, HEAD_PAD, aw), F32)
            mats = mats.at[:, :QK_NOPE, :kv_lora].set(wk_t)
            mats = mats.at[:, QK_NOPE:QK_NOPE + QK_ROPE, kv_lora:kv_lora + QK_ROPE].set(
                jnp.broadcast_to(jnp.eye(QK_ROPE, dtype=F32), (MLA_HEADS, QK_ROPE, QK_ROPE)))
            qabs = _qabs(q_s, mats.astype(BF16))
            qabs = jnp.transpose(qabs.reshape(MLA_HEADS, bs, t_new, aw), (1, 0, 2, 3)).reshape(
                bs, MLA_HEADS * t_new, aw)
            t_pad = 8
            pad_t = lambda a: jnp.pad(a.reshape(bs, t_new, -1), ((0, 0), (0, t_pad - t_new), (0, 0)))
            o_lat = _paged_attn(page_table, qabs, pad_t(ckv_s), pad_t(kr_s), cache_kv_latent[o], cache_k_rope[o],
                                pages=_pick(page_table.shape[1], 16), t_new=t_new)
            o_lat = jnp.transpose(o_lat.reshape(bs, MLA_HEADS, t_new, kv_lora), (1, 0, 2, 3)).reshape(
                MLA_HEADS, n_s, kv_lora)
            wv = jnp.transpose(mla_w_uv[o], (1, 0, 2))
            w_pairs = jnp.zeros((MLA_HEADS // 2, 2 * kv_lora, 2 * V_DIM), F32)
            w_pairs = w_pairs.at[:, :kv_lora, :V_DIM].set(wv[0::2]).at[:, kv_lora:, V_DIM:].set(wv[1::2])
            o_s = _vup(o_lat, w_pairs.astype(BF16))
            proj_s = (o_s, wout, rep(gt_ms))
            outs.setdefault("lat_s", []).append(ckv_s.reshape(bs, t_new, kv_lora))
            outs.setdefault("rope_s", []).append(kr_s.reshape(bs, t_new, QK_ROPE))

        xp = _moe(xp, ffn_mods_p, rw, experts, tile=tile_p, gmm_tile=gmm_tile_p, rows_per_mod=seq,
                  proj=proj_p, final_gain=final_gain)
        xs = _moe(xs, ffn_mods_s, rw, experts, tile=tile_s, gmm_tile=gmm_tile_s, rows_per_mod=n_s,
                  proj=proj_s, final_gain=final_gain)

    st = lambda k: jnp.stack(outs[k])
    return (xp.reshape(bp, seq, d), xs.reshape(bs, t_new, d),
            st("pool_p"), st("pool_s"), st("conv_p"), st("conv_s"), st("lru_p"), st("lru_s"),
            st("lat_p"), st("lat_s"), st("rope_p"), st("rope_s"))
```

```python
import functools

import jax
import jax.numpy as jnp
from jax import lax
from jax.experimental import pallas as pl
from jax.experimental.pallas import tpu as pltpu

F32 = jnp.float32
BF16 = jnp.bfloat16

EPS = 1e-6
POOL_WINDOWS = (2, 4, 8, 16)
POOL_HALO = 16
CONV_WIDTH = 4
CONV_HALO = 8
LRU_C = 8.0
MLA_HEADS = 16
QK_NOPE = 64
QK_ROPE = 32
V_DIM = 64
HEAD_PAD = 128
ROPE_THETA = 10000.0
SM_SCALE = (QK_NOPE + QK_ROPE) ** -0.5
MOE_GROUPS = 4
EXPERTS_PER_GROUP = 8
N_EXPERTS = MOE_GROUPS * EXPERTS_PER_GROUP
TOP_K = 2
LANES = 128
ROW_DMA_UNROLL = 8
LOG2E = 1.4426950408889634
VMEM_LIMIT = 56 * 1024 * 1024


def _cparams(n_axes):
    return pltpu.CompilerParams(dimension_semantics=("arbitrary",) * n_axes,
                                vmem_limit_bytes=VMEM_LIMIT)


def _dot(a, b):
    return jnp.dot(a, b, preferred_element_type=F32)


def _dot_nt(a, b):
    return lax.dot_general(a, b, (((1,), (1,)), ((), ())), preferred_element_type=F32)


def _rms(x, gain):
    return x * lax.rsqrt(jnp.mean(x * x, axis=-1, keepdims=True) + EPS) * gain


def _silu(x):
    return x * jax.nn.sigmoid(x)


def _ada_kernel(c_ref, w_ref, b_ref, o_ref):
    a = _silu(c_ref[...]).astype(BF16)
    o_ref[0] = _dot(a, w_ref[0].astype(BF16)) + b_ref[0]


def _ada(c_all, ada_w, ada_b):
    depth, d, n6 = ada_w.shape
    rows = c_all.shape[0]
    tn = 1024
    return pl.pallas_call(
        _ada_kernel,
        grid=(depth, n6 // tn),
        in_specs=[pl.BlockSpec((rows, d), lambda l, j: (0, 0)),
                  pl.BlockSpec((1, d, tn), lambda l, j: (l, 0, j)),
                  pl.BlockSpec((1, 1, tn), lambda l, j: (l, 0, j))],
        out_specs=pl.BlockSpec((1, rows, tn), lambda l, j: (l, 0, j)),
        out_shape=jax.ShapeDtypeStruct((depth, rows, n6), F32),
        compiler_params=_cparams(2),
        name="ada_mod",
    )(c_all, ada_w, ada_b.reshape(depth, 1, n6))


def _mix0_kernel(x_ref, sh_ref, sc_ref, gt_ref, pool0_ref, conv0_ref, lru0_ref,
                 gain_ref, win_ref, poolw_ref, pscale_ref, convw_ref, convb_ref,
                 wri_ref, bri_ref, lam_ref, wout_ref,
                 x1_ref, pooln_ref, convn_ref, lrun_ref,
                 zp_ref, zc_ref, h_ref, *, tt, bb, start, n_t):
    t = pl.program_id(1)
    rows = tt * bb
    pw = zp_ref.shape[1]
    p0 = POOL_HALO * bb
    c0 = CONV_HALO * bb

    @pl.when(t == 0)
    def _():
        zp_ref[0:bb, :] = jnp.zeros((bb, pw), F32)
        zp_ref[bb:p0, :] = pool0_ref[0]
        zc_ref[0:c0 - (CONV_WIDTH - 1) * bb, :] = jnp.zeros((c0 - (CONV_WIDTH - 1) * bb, pw), F32)
        zc_ref[c0 - (CONV_WIDTH - 1) * bb:c0, :] = conv0_ref[0]
        h_ref[...] = lru0_ref[0]

    def per_row(v):
        return v if bb == 1 else jnp.concatenate([v] * tt, axis=0)

    x = x_ref[0]
    h = _rms(x, gain_ref[...]) * (1.0 + per_row(sc_ref[0])) + per_row(sh_ref[0])
    proj = _dot(h.astype(BF16), win_ref[...])
    u_pool = proj[:, :pw]
    u_x = proj[:, pw:2 * pw]
    u_g = proj[:, 2 * pw:]
    zp_ref[p0:p0 + rows, :] = u_pool
    zc_ref[c0:c0 + rows, :] = u_x

    if bb == 1:
        tix = lax.broadcasted_iota(jnp.int32, (rows, 1), 0)
    else:
        tix = jnp.concatenate([jnp.full((bb, 1), i, jnp.int32) for i in range(tt)], axis=0)
    pos = start + t * tt + tix

    gd = pw // len(POOL_WINDOWS)
    means = []
    for g, w in enumerate(POOL_WINDOWS):
        cols = slice(g * gd, (g + 1) * gd)
        acc = zp_ref[p0:p0 + rows, cols]
        for i in range(1, w):
            acc = acc + zp_ref[p0 - i * bb:p0 - i * bb + rows, cols]
        cnt = jnp.minimum(pos + 1, w).astype(F32)
        means.append(acc / cnt)
    pooled = jnp.concatenate(means, axis=-1) - u_pool
    y_a = _dot(pooled.astype(BF16), poolw_ref[...]) * pscale_ref[...]

    xc = convb_ref[...]
    for k in range(CONV_WIDTH):
        off = c0 - (CONV_WIDTH - 1 - k) * bb
        xc = xc + zc_ref[off:off + rows, :] * convw_ref[k:k + 1, :]
    pre = _dot(xc.astype(BF16), wri_ref[...]) + bri_ref[...]
    r = jax.nn.sigmoid(pre[:, :pw])
    gi = jax.nn.sigmoid(pre[:, pw:])
    lam = lam_ref[...]
    softplus_neg = jnp.maximum(-lam, 0.0) + jnp.log1p(jnp.exp(-jnp.abs(lam)))
    log_a = -LRU_C * r * softplus_neg
    a = jnp.exp(log_a)
    b = jnp.sqrt(1.0 - a * a) * gi * xc

    if bb == 1:
        rowi = lax.broadcasted_iota(jnp.int32, (rows, 1), 0)
        s = 1
        while s < rows:
            a_sh = pltpu.roll(a, s, 0)
            b_sh = pltpu.roll(b, s, 0)
            keep = rowi >= s
            b = jnp.where(keep, a * b_sh + b, b)
            a = jnp.where(keep, a * a_sh, a)
            s *= 2
        hs = b + a * h_ref[...]
        h_ref[...] = hs[rows - 1:rows, :]
    else:
        hprev = h_ref[...]
        parts = []
        for i in range(tt):
            hprev = a[i * bb:(i + 1) * bb] * hprev + b[i * bb:(i + 1) * bb]
            parts.append(hprev)
        hs = jnp.concatenate(parts, axis=0)
        h_ref[...] = hprev
    y_b = hs * jax.nn.gelu(u_g)

    mix = _dot(y_a.astype(BF16), wout_ref[0:pw, :]) + _dot(y_b.astype(BF16), wout_ref[pw:2 * pw, :])
    x1_ref[0] = x + per_row(gt_ref[0]) * mix

    @pl.when(t == n_t - 1)
    def _():
        pooln_ref[0] = zp_ref[p0 + rows - (POOL_HALO - 1) * bb:p0 + rows, :]
        convn_ref[0] = zc_ref[c0 + rows - (CONV_WIDTH - 1) * bb:c0 + rows, :]
        lrun_ref[0] = h_ref[...]

    if n_t > 1:
        @pl.when(t < n_t - 1)
        def _():
            zp_ref[bb:p0, :] = zp_ref[rows + bb:rows + p0, :]
            zc_ref[c0 - (CONV_WIDTH - 1) * bb:c0, :] = zc_ref[c0 + rows - (CONV_WIDTH - 1) * bb:c0 + rows, :]


def _mix0(x, sh, sc, gt, pool0, conv0, lru0, wts, *, tt, bb, start):
    nbb, tot, d = x.shape
    n_t = tot // (tt * bb)
    rows = tt * bb
    pw = pool0.shape[-1]
    hp = (POOL_HALO - 1) * bb
    hc = (CONV_WIDTH - 1) * bb

    def const(a):
        nd = a.ndim
        return pl.BlockSpec(a.shape, lambda i, j: (0,) * nd)

    def per_b(n_rows, width):
        return pl.BlockSpec((1, n_rows, width), lambda i, j: (i, 0, 0))

    kern = functools.partial(_mix0_kernel, tt=tt, bb=bb, start=start, n_t=n_t)
    return pl.pallas_call(
        kern,
        grid=(nbb, n_t),
        in_specs=[pl.BlockSpec((1, rows, d), lambda i, j: (i, j, 0)),
                  per_b(bb, d), per_b(bb, d), per_b(bb, d),
                  per_b(hp, pw), per_b(hc, pw), per_b(bb, pw)] + [const(w) for w in wts],
        out_specs=[pl.BlockSpec((1, rows, d), lambda i, j: (i, j, 0)),
                   per_b(hp, pw), per_b(hc, pw), per_b(bb, pw)],
        out_shape=[jax.ShapeDtypeStruct((nbb, tot, d), F32),
                   jax.ShapeDtypeStruct((nbb, hp, pw), F32),
                   jax.ShapeDtypeStruct((nbb, hc, pw), F32),
                   jax.ShapeDtypeStruct((nbb, bb, pw), F32)],
        scratch_shapes=[pltpu.VMEM(((POOL_HALO + tt) * bb, pw), F32),
                        pltpu.VMEM(((CONV_HALO + tt) * bb, pw), F32),
                        pltpu.VMEM((bb, pw), F32)],
        compiler_params=_cparams(2),
        name="mix0",
    )(x, sh, sc, gt, pool0, conv0, lru0, *wts)


def _route_kernel(*refs, with_proj):
    if with_proj:
        (x_ref, o_ref, wout_ref, gtm_ref, shf_ref, scf_ref, gain_ref, wrh_ref, wrl_ref, br_ref,
         x1_ref, hf_ref, info_ref, cnt_ref, run_ref) = refs
    else:
        (x_ref, shf_ref, scf_ref, gain_ref, wrh_ref, wrl_ref, br_ref,
         hf_ref, info_ref, cnt_ref, run_ref) = refs

    @pl.when(pl.program_id(0) == 0)
    def _():
        run_ref[...] = jnp.zeros(run_ref.shape, F32)

    x = x_ref[...]
    if with_proj:
        x = x + gtm_ref[0] * _dot(o_ref[...], wout_ref[...])
        x1_ref[...] = x
    hf = _rms(x, gain_ref[...]) * (1.0 + scf_ref[0]) + shf_ref[0]
    hf_ref[...] = hf

    hi = hf.astype(BF16)
    lo = (hf - hi.astype(F32)).astype(BF16)
    logits = _dot(hi, wrh_ref[...]) + _dot(lo, wrh_ref[...]) + _dot(hi, wrl_ref[...]) + br_ref[...]

    tq = logits.shape[0]
    lane = lax.broadcasted_iota(jnp.int32, (tq, LANES), 1).astype(F32)
    neg = -jnp.inf
    big = float(LANES)
    is_g = lane < MOE_GROUPS
    gl = jnp.where(is_g, logits, neg)
    mg = jnp.max(gl, axis=1, keepdims=True)
    gidx = jnp.min(jnp.where(gl == mg, lane, big), axis=1, keepdims=True)
    p_group = 1.0 / jnp.sum(jnp.where(is_g, jnp.exp(gl - mg), 0.0), axis=1, keepdims=True)
    first = MOE_GROUPS + gidx * EXPERTS_PER_GROUP
    el = jnp.where((lane >= first) & (lane < first + EXPERTS_PER_GROUP), logits, neg)
    v1 = jnp.max(el, axis=1, keepdims=True)
    i1 = jnp.min(jnp.where(el == v1, lane, big), axis=1, keepdims=True)
    el2 = jnp.where(lane == i1, neg, el)
    v2 = jnp.max(el2, axis=1, keepdims=True)
    i2 = jnp.min(jnp.where(el2 == v2, lane, big), axis=1, keepdims=True)
    ex = jnp.exp(v2 - v1)
    g1 = p_group / (1.0 + ex)
    g2 = p_group * ex / (1.0 + ex)
    e1 = i1 - MOE_GROUPS
    e2 = i2 - MOE_GROUPS

    oh1 = jnp.where(lane == e1, 1.0, 0.0)
    oh2 = jnp.where(lane == e2, 1.0, 0.0)
    oh = oh1 + oh2
    ri = lax.broadcasted_iota(jnp.int32, (tq, tq), 0)
    ci = lax.broadcasted_iota(jnp.int32, (tq, tq), 1)
    lower = jnp.where(ri > ci, 1.0, 0.0).astype(BF16)
    before = _dot(lower, oh.astype(BF16)) + run_ref[...]
    r1 = jnp.sum(before * oh1, axis=1, keepdims=True)
    r2 = jnp.sum(before * oh2, axis=1, keepdims=True)
    run_ref[...] = run_ref[...] + jnp.sum(oh, axis=0, keepdims=True)
    cnt_ref[...] = run_ref[...]

    info = jnp.where(lane == 0, e1, jnp.where(lane == 1, e2, jnp.where(lane == 2, g1, jnp.where(
        lane == 3, g2, jnp.where(lane == 4, r1, jnp.where(lane == 5, r2, 0.0))))))
    info_ref[...] = info


def _route(x, shf, scf, gain, wrh, wrl, br, *, tile, rows_per_mod, proj=None):
    n, d = x.shape
    nt = n // tile
    mrows = shf.shape[1]
    mod_spec = pl.BlockSpec((1, mrows, d), lambda i: ((i * tile) // rows_per_mod, 0, 0))
    row_spec = pl.BlockSpec((tile, d), lambda i: (i, 0))

    def const(a):
        nd = a.ndim
        return pl.BlockSpec(a.shape, lambda i: (0,) * nd)

    info_spec = pl.BlockSpec((tile, LANES), lambda i: (i, 0))
    cnt_spec = pl.BlockSpec((1, LANES), lambda i: (0, 0))
    outs_shape = [jax.ShapeDtypeStruct((n, d), F32), jax.ShapeDtypeStruct((n, LANES), F32),
                  jax.ShapeDtypeStruct((1, LANES), F32)]
    outs_spec = [row_spec, info_spec, cnt_spec]
    if proj is None:
        ins = [x, shf, scf, gain, wrh, wrl, br]
        in_specs = [row_spec, mod_spec, mod_spec, const(gain), const(wrh), const(wrl), const(br)]
    else:
        o, wout, gtm = proj
        ins = [x, o, wout, gtm, shf, scf, gain, wrh, wrl, br]
        in_specs = [row_spec, pl.BlockSpec((tile, o.shape[1]), lambda i: (i, 0)), const(wout), mod_spec,
                    mod_spec, mod_spec, const(gain), const(wrh), const(wrl), const(br)]
        outs_shape = [jax.ShapeDtypeStruct((n, d), F32)] + outs_shape
        outs_spec = [row_spec] + outs_spec
    return pl.pallas_call(
        functools.partial(_route_kernel, with_proj=proj is not None),
        grid=(nt,),
        in_specs=in_specs,
        out_specs=outs_spec,
        out_shape=outs_shape,
        scratch_shapes=[pltpu.VMEM((1, LANES), F32)],
        compiler_params=_cparams(1),
        name="route",
    )(*ins)


def _dispatch_kernel(slot_ref, hf_ref, xs_ref, sem, *, tile):
    def issue(g, c):
        for u in range(ROW_DMA_UNROLL):
            r = g * ROW_DMA_UNROLL + u
            for k in range(TOP_K):
                s = slot_ref[0, 0, TOP_K * r + k]
                pltpu.make_async_copy(hf_ref.at[pl.ds(r, 1), :], xs_ref.at[pl.ds(s, 1), :], sem).start()
        return c

    lax.fori_loop(0, tile // ROW_DMA_UNROLL, issue, 0)
    for k in range(TOP_K):
        pltpu.make_async_copy(hf_ref, xs_ref.at[pl.ds(0, tile), :], sem).wait()


def _dispatch(hf, slots, *, tile):
    n, d = hf.shape
    nt = n // tile
    return pl.pallas_call(
        functools.partial(_dispatch_kernel, tile=tile),
        grid=(nt,),
        in_specs=[pl.BlockSpec((1, 1, TOP_K * tile), lambda i: (i, 0, 0), memory_space=pltpu.SMEM),
                  pl.BlockSpec((tile, d), lambda i: (i, 0))],
        out_specs=pl.BlockSpec(memory_space=pl.ANY),
        out_shape=jax.ShapeDtypeStruct((TOP_K * n, d), F32),
        scratch_shapes=[pltpu.SemaphoreType.DMA(())],
        compiler_params=_cparams(1),
        name="moe_dispatch",
    )(slots.reshape(nt, 1, TOP_K * tile), hf)


def _gmm_kernel(wt_ref, we_ref, wlo_ref, whi_ref, xs_ref, wg_ref, wu_ref, wd_ref, y_ref,
                wgb_ref, wub_ref, wdb_ref):
    w = pl.program_id(0)
    prev = jnp.maximum(w - 1, 0)
    new_expert = (w == 0) | (we_ref[w] != we_ref[prev])
    new_tile = (w == 0) | (wt_ref[w] != wt_ref[prev])

    @pl.when(new_expert)
    def _():
        wgb_ref[...] = wg_ref[0, 0].astype(BF16)
        wub_ref[...] = wu_ref[0, 0].astype(BF16)
        wdb_ref[...] = wd_ref[0, 0].astype(BF16)

    @pl.when(new_tile)
    def _():
        y_ref[...] = jnp.zeros(y_ref.shape, F32)

    lo = wlo_ref[w]
    hi = whi_ref[w]

    @pl.when(hi > lo)
    def _():
        x = xs_ref[...].astype(BF16)
        g = _dot(x, wgb_ref[...])
        u = _dot(x, wub_ref[...])
        yv = _dot((_silu(g) * u).astype(BF16), wdb_ref[...])
        row = lax.broadcasted_iota(jnp.int32, (x.shape[0], 1), 0)
        y_ref[...] = y_ref[...] + jnp.where((row >= lo) & (row < hi), yv, 0.0)


def _gmm(xs, work, w_gate, w_up, w_down, *, tile, layer):
    m, d = xs.shape
    ff = w_gate.shape[-1]
    n_work = work[0].shape[0]
    grid_spec = pltpu.PrefetchScalarGridSpec(
        num_scalar_prefetch=4,
        grid=(n_work,),
        in_specs=[pl.BlockSpec((tile, d), lambda w, wt, we, wlo, whi: (wt[w], 0)),
                  pl.BlockSpec((1, 1, d, ff), lambda w, wt, we, wlo, whi: (layer, we[w], 0, 0)),
                  pl.BlockSpec((1, 1, d, ff), lambda w, wt, we, wlo, whi: (layer, we[w], 0, 0)),
                  pl.BlockSpec((1, 1, ff, d), lambda w, wt, we, wlo, whi: (layer, we[w], 0, 0))],
        out_specs=pl.BlockSpec((tile, d), lambda w, wt, we, wlo, whi: (wt[w], 0)),
        scratch_shapes=[pltpu.VMEM((d, ff), BF16), pltpu.VMEM((d, ff), BF16), pltpu.VMEM((ff, d), BF16)],
    )
    return pl.pallas_call(
        _gmm_kernel,
        grid_spec=grid_spec,
        out_shape=jax.ShapeDtypeStruct((m, d), F32),
        compiler_params=_cparams(1),
        name="moe_gmm",
    )(*work, xs, w_gate, w_up, w_down)


def _work_list(counts, n_slots, tile):
    n_tiles = n_slots // tile
    n_work = n_tiles + N_EXPERTS - 1
    ends = jnp.cumsum(counts)
    starts = ends - counts
    first_tile = starts // tile
    last_tile = jnp.maximum(ends - 1, 0) // tile
    n_items = jnp.where(counts > 0, last_tile - first_tile + 1, 0)
    item_end = jnp.cumsum(n_items)
    item_start = item_end - n_items
    w = jnp.arange(n_work, dtype=jnp.int32)
    used = w < item_end[-1]
    e = jnp.minimum(jnp.sum(w[:, None] >= item_end[None, :], axis=1), N_EXPERTS - 1).astype(jnp.int32)
    t = first_tile[e] + (w - item_start[e])
    lo = jnp.maximum(starts[e], t * tile) - t * tile
    hi = jnp.minimum(ends[e], (t + 1) * tile) - t * tile
    last_e = jnp.max(jnp.where(counts > 0, jnp.arange(N_EXPERTS), 0)).astype(jnp.int32)
    wt = jnp.where(used, t, n_tiles - 1).astype(jnp.int32)
    we = jnp.where(used, e, last_e).astype(jnp.int32)
    wlo = jnp.where(used, lo, 0).astype(jnp.int32)
    whi = jnp.where(used, hi, 0).astype(jnp.int32)
    return (wt, we, wlo, whi), starts


def _combine_kernel(*refs, tile, final):
    if final:
        slot_ref, x_ref, gt_ref, info_ref, fg_ref, y_ref, out_ref, ybuf, sem = refs
    else:
        slot_ref, x_ref, gt_ref, info_ref, y_ref, out_ref, ybuf, sem = refs

    def issue(g, c):
        for u in range(ROW_DMA_UNROLL):
            r = g * ROW_DMA_UNROLL + u
            for k in range(TOP_K):
                s = slot_ref[0, 0, TOP_K * r + k]
                pltpu.make_async_copy(y_ref.at[pl.ds(s, 1), :], ybuf.at[k, pl.ds(r, 1), :], sem).start()
        return c

    lax.fori_loop(0, tile // ROW_DMA_UNROLL, issue, 0)
    for k in range(TOP_K):
        pltpu.make_async_copy(y_ref.at[pl.ds(0, tile), :], ybuf.at[k], sem).wait()

    info = info_ref[...]
    ffn = info[:, 2:3] * ybuf[0] + info[:, 3:4] * ybuf[1]
    out = x_ref[...] + gt_ref[0] * ffn
    if final:
        out = _rms(out, fg_ref[...])
    out_ref[...] = out


def _combine(x, gt, info, y, slots, *, tile, rows_per_mod, final_gain=None):
    n, d = x.shape
    nt = n // tile
    mrows = gt.shape[1]
    final = final_gain is not None
    in_specs = [pl.BlockSpec((1, 1, TOP_K * tile), lambda i: (i, 0, 0), memory_space=pltpu.SMEM),
                pl.BlockSpec((tile, d), lambda i: (i, 0)),
                pl.BlockSpec((1, mrows, d), lambda i: ((i * tile) // rows_per_mod, 0, 0)),
                pl.BlockSpec((tile, LANES), lambda i: (i, 0))]
    ins = [slots.reshape(nt, 1, TOP_K * tile), x, gt, info]
    if final:
        in_specs.append(pl.BlockSpec(final_gain.shape, lambda i: (0, 0)))
        ins.append(final_gain)
    in_specs.append(pl.BlockSpec(memory_space=pl.ANY))
    ins.append(y)
    return pl.pallas_call(
        functools.partial(_combine_kernel, tile=tile, final=final),
        grid=(nt,),
        in_specs=in_specs,
        out_specs=pl.BlockSpec((tile, d), lambda i: (i, 0)),
        out_shape=jax.ShapeDtypeStruct((n, d), F32),
        scratch_shapes=[pltpu.VMEM((TOP_K, tile, d), F32), pltpu.SemaphoreType.DMA(())],
        compiler_params=_cparams(1),
        name="moe_combine",
    )(*ins)


def _moe(x, mods, rw, experts, *, layer, tile, gmm_tile, rows_per_mod, proj=None, final_gain=None):
    shf, scf, gtf = mods
    gain, wrh, wrl, br = rw
    res = _route(x, shf, scf, gain, wrh, wrl, br, tile=tile, rows_per_mod=rows_per_mod, proj=proj)
    if proj is not None:
        x, hf, info, cnt = res
    else:
        hf, info, cnt = res
    n = x.shape[0]
    counts = cnt[0, :N_EXPERTS].astype(jnp.int32)
    work, starts = _work_list(counts, TOP_K * n, gmm_tile)
    eid = info[:, 0:TOP_K].astype(jnp.int32)
    rank = info[:, 4:4 + TOP_K].astype(jnp.int32)
    is_e = eid[:, :, None] == jnp.arange(N_EXPERTS, dtype=jnp.int32)
    slots = (jnp.sum(jnp.where(is_e, starts.astype(jnp.int32), 0), axis=-1) + rank).reshape(-1)
    xs = _dispatch(hf, slots, tile=tile)
    y = _gmm(xs, work, *experts, tile=gmm_tile, layer=layer)
    return _combine(x, gtf, info, y, slots, tile=tile, rows_per_mod=rows_per_mod, final_gain=final_gain)


def _rope_turn(blk, c, s):
    return blk * c + pltpu.roll(blk, LANES - QK_ROPE, 1) * s


def _mla_proj_kernel(*refs, sample):
    if sample:
        (x_ref, sh_ref, sc_ref, gain_ref, win_ref, qg_ref, wuq_ref, kvg_ref, c_ref, s_ref,
         ckv_ref, kr_ref, q_ref) = refs
    else:
        (x_ref, sh_ref, sc_ref, gain_ref, win_ref, qg_ref, wuq_ref, kvg_ref, c_ref, s_ref, wuk_ref, wuv_ref,
         vone_ref, ckv_ref, kr_ref, q_ref, k_ref, v_ref) = refs
    q_lora = qg_ref.shape[1]
    kv_lora = kvg_ref.shape[1]
    h = _rms(x_ref[...], gain_ref[...]) * (1.0 + sc_ref[0]) + sh_ref[0]
    proj = _dot(h.astype(BF16), win_ref[...])
    qn = _rms(proj[:, :q_lora], qg_ref[...])
    ckv = _rms(proj[:, q_lora:q_lora + kv_lora], kvg_ref[...])
    c = c_ref[...]
    s = s_ref[...]
    kf = _rope_turn(proj[:, q_lora + kv_lora:], c, s)
    ckv_ref[...] = ckv
    kr_ref[...] = kf[:, QK_NOPE:QK_NOPE + QK_ROPE]
    q = _dot(qn.astype(BF16), wuq_ref[...])
    if sample:
        for hh in range(MLA_HEADS):
            cols = slice(hh * HEAD_PAD, (hh + 1) * HEAD_PAD)
            q_ref[:, cols] = _rope_turn(q[:, cols], c, s) * SM_SCALE
    else:
        ckv_b = ckv.astype(BF16)
        kn = _dot(ckv_b, wuk_ref[...])
        v_ref[...] = (_dot(ckv_b, wuv_ref[...]) + vone_ref[...]).astype(BF16)
        for hh in range(MLA_HEADS):
            cols = slice(hh * HEAD_PAD, (hh + 1) * HEAD_PAD)
            q_ref[:, cols] = (_rope_turn(q[:, cols], c, s) * (SM_SCALE * LOG2E)).astype(BF16)
            k_ref[:, cols] = (kn[:, cols] + kf).astype(BF16)


def _mla_proj(x, sh, sc, gain, win, qg, wuq, kvg, ctab, stab, *, tile, rows_per_mod, tab_tiles, kv_w=None):
    n, d = x.shape
    nt = n // tile
    mrows = sh.shape[1]
    sample = kv_w is None
    kv_lora = kvg.shape[1]

    def const(a):
        nd = a.ndim
        return pl.BlockSpec(a.shape, lambda i: (0,) * nd)

    row = lambda width: pl.BlockSpec((tile, width), lambda i: (i, 0))
    mod_spec = pl.BlockSpec((1, mrows, d), lambda i: ((i * tile) // rows_per_mod, 0, 0))
    tab_spec = pl.BlockSpec((tile, LANES), lambda i: (i % tab_tiles, 0))
    ins = [x, sh, sc, gain, win, qg, wuq, kvg, ctab, stab]
    in_specs = [row(d), mod_spec, mod_spec, const(gain), const(win), const(qg), const(wuq), const(kvg),
                tab_spec, tab_spec]
    qw = MLA_HEADS * HEAD_PAD
    out_shape = [jax.ShapeDtypeStruct((n, kv_lora), F32), jax.ShapeDtypeStruct((n, QK_ROPE), F32)]
    out_specs = [row(kv_lora), row(QK_ROPE)]
    if sample:
        out_shape.append(jax.ShapeDtypeStruct((n, qw), F32))
        out_specs.append(row(qw))
    else:
        ins += list(kv_w)
        in_specs += [const(a) for a in kv_w]
        out_shape += [jax.ShapeDtypeStruct((n, qw), BF16)] * 3
        out_specs += [row(qw)] * 3
    return pl.pallas_call(
        functools.partial(_mla_proj_kernel, sample=sample),
        grid=(nt,),
        in_specs=in_specs,
        out_specs=out_specs,
        out_shape=out_shape,
        compiler_params=_cparams(1),
        name="mla_proj",
    )(*ins)


def _attn_kernel(q_ref, k_ref, v_ref, o_ref, sa_ref, sb_ref, m_ref, acc_ref, *, tq, tk):
    i = pl.program_id(2)
    heads = range(2)

    def cols(hh):
        return slice(hh * HEAD_PAD, (hh + 1) * HEAD_PAD)

    def scores(j, dst_ref):
        rows = pl.ds(pl.multiple_of(j * tk, tk), tk)
        for hh in heads:
            dst_ref[hh] = _dot_nt(q_ref[0, :, cols(hh)], k_ref[0, rows, cols(hh)])

    def absorb(j, src_ref, masked):
        rows = pl.ds(pl.multiple_of(j * tk, tk), tk)
        if masked:
            keep = lax.broadcasted_iota(jnp.int32, (tq, tk), 1) <= lax.broadcasted_iota(jnp.int32, (tq, tk), 0)
        for hh in heads:
            s = src_ref[hh]
            if masked:
                s = jnp.where(keep, s, -jnp.inf)
            m = m_ref[hh]
            m_new = jnp.maximum(m, jnp.max(s, axis=1, keepdims=True))
            p = jnp.exp2(s - jnp.concatenate([m_new] * (tk // LANES), axis=1))
            acc_ref[hh] = jnp.exp2(m - m_new) * acc_ref[hh] + _dot(p.astype(BF16), v_ref[0, rows, cols(hh)])
            m_ref[hh] = m_new

    m_ref[...] = jnp.full(m_ref.shape, -jnp.inf, F32)
    acc_ref[...] = jnp.zeros(acc_ref.shape, F32)
    scores(0, sa_ref)

    def pair(jj, c):
        j = 2 * jj
        scores(j + 1, sb_ref)
        absorb(j, sa_ref, False)
        scores(j + 2, sa_ref)
        absorb(j + 1, sb_ref, False)
        return c

    lax.fori_loop(0, i // 2, pair, 0)

    @pl.when(i % 2 == 1)
    def _():
        scores(i, sb_ref)
        absorb(i - 1, sa_ref, False)
        absorb(i, sb_ref, True)

    @pl.when(i % 2 == 0)
    def _():
        absorb(i, sa_ref, True)

    res = [acc_ref[hh] / acc_ref[hh][:, V_DIM:V_DIM + 1] for hh in heads]
    lane = lax.broadcasted_iota(jnp.int32, (tq, HEAD_PAD), 1)
    o_ref[0] = jnp.where(lane < V_DIM, res[0], pltpu.roll(res[1], V_DIM, 1)).astype(o_ref.dtype)


def _attention(q, k, v, *, tq, tk):
    assert tq == tk, "one diagonal key tile per query tile"
    b, s, _ = q.shape
    pairs = MLA_HEADS // 2
    return pl.pallas_call(
        functools.partial(_attn_kernel, tq=tq, tk=tk),
        grid=(b, pairs, s // tq),
        in_specs=[pl.BlockSpec((1, tq, 2 * HEAD_PAD), lambda bi, p, i: (bi, i, p)),
                  pl.BlockSpec((1, s, 2 * HEAD_PAD), lambda bi, p, i: (bi, 0, p)),
                  pl.BlockSpec((1, s, 2 * HEAD_PAD), lambda bi, p, i: (bi, 0, p))],
        out_specs=pl.BlockSpec((1, tq, 2 * V_DIM), lambda bi, p, i: (bi, i, p)),
        out_shape=jax.ShapeDtypeStruct((b, s, MLA_HEADS * V_DIM), BF16),
        scratch_shapes=[pltpu.VMEM((2, tq, tk), F32), pltpu.VMEM((2, tq, tk), F32),
                        pltpu.VMEM((2, tq, LANES), F32), pltpu.VMEM((2, tq, HEAD_PAD), F32)],
        compiler_params=_cparams(3),
        name="prompt_attn",
    )(q, k, v)


def _qabs_kernel(q_ref, m_ref, o_ref):
    o_ref[0] = _dot(q_ref[...].astype(BF16), m_ref[0])


def _qabs(q, mats):
    n = q.shape[0]
    width = mats.shape[-1]
    return pl.pallas_call(
        _qabs_kernel,
        grid=(MLA_HEADS,),
        in_specs=[pl.BlockSpec((n, HEAD_PAD), lambda h: (0, h)),
                  pl.BlockSpec((1, HEAD_PAD, width), lambda h: (h, 0, 0))],
        out_specs=pl.BlockSpec((1, n, width), lambda h: (h, 0, 0)),
        out_shape=jax.ShapeDtypeStruct((MLA_HEADS, n, width), F32),
        compiler_params=_cparams(1),
        name="sample_qabs",
    )(q, mats)


def _paged_attn_kernel(pt_ref, q_ref, cn_ref, rn_ref, ck_hbm, kr_hbm, o_ref,
                       ckbuf, krbuf, m_ref, l_ref, acc_ref, sem, *, group, pages, page, n_chunks, t_new, t_pad):
    bg = pl.program_id(0)
    c = pl.program_id(1)
    step = bg * n_chunks + c
    slot = step % 2
    kv_lora = ckbuf.shape[-1]

    def start_chunk(bgi, ci, sl):
        for g in range(group):
            for p in range(pages):
                phys = pt_ref[bgi * group + g, ci * pages + p]
                pltpu.make_async_copy(ck_hbm.at[phys], ckbuf.at[sl, g, pl.ds(p * page, page), :],
                                      sem.at[0, sl]).start()
                pltpu.make_async_copy(kr_hbm.at[phys], krbuf.at[sl, g, p], sem.at[1, sl]).start()

    @pl.when(step == 0)
    def _():
        start_chunk(bg, c, slot)

    pltpu.make_async_copy(ckbuf.at[slot], ckbuf.at[slot], sem.at[0, slot]).wait()
    pltpu.make_async_copy(krbuf.at[slot], krbuf.at[slot], sem.at[1, slot]).wait()

    @pl.when(c == 0)
    def _():
        m_ref[...] = jnp.full(m_ref.shape, -jnp.inf, F32)
        l_ref[...] = jnp.zeros(l_ref.shape, F32)
        acc_ref[...] = jnp.zeros(acc_ref.shape, F32)

    def absorb(g, s, values):
        m = m_ref[g]
        m_new = jnp.maximum(m, jnp.max(s, axis=1, keepdims=True))
        alpha = jnp.exp(m - m_new)
        p = jnp.exp(s - m_new)
        l_ref[g] = alpha * l_ref[g] + jnp.sum(p, axis=1, keepdims=True)
        acc_ref[g] = alpha * acc_ref[g] + _dot(p.astype(BF16), values)
        m_ref[g] = m_new

    cks, scores = [], []
    for g in range(group):
        qa = q_ref[g]
        q_lat = qa[:, :kv_lora].astype(BF16)
        q_rope = qa[:, kv_lora:kv_lora + QK_ROPE].astype(BF16)
        ck = ckbuf[slot, g].astype(BF16)
        kr_t = jnp.concatenate([krbuf[slot, g, p] for p in range(pages)], axis=1).astype(BF16)
        cks.append(ck)
        scores.append(_dot_nt(q_lat, ck) + _dot(q_rope, kr_t))

    nxt = step + 1

    @pl.when(nxt < pl.num_programs(0) * n_chunks)
    def _():
        start_chunk(nxt // n_chunks, nxt % n_chunks, 1 - slot)

    for g in range(group):
        absorb(g, scores[g], cks[g])

    @pl.when(c == n_chunks - 1)
    def _():
        rows = q_ref.shape[1]
        t_q = lax.broadcasted_iota(jnp.int32, (rows, t_pad), 0) % t_new
        t_k = lax.broadcasted_iota(jnp.int32, (rows, t_pad), 1)
        for g in range(group):
            qa = q_ref[g]
            q_lat = qa[:, :kv_lora].astype(BF16)
            q_rope = qa[:, kv_lora:kv_lora + QK_ROPE].astype(BF16)
            cn = cn_ref[g].astype(BF16)
            s = _dot_nt(q_lat, cn) + _dot_nt(q_rope, rn_ref[g].astype(BF16))
            absorb(g, jnp.where(t_k <= t_q, s, -jnp.inf), cn)
            o_ref[g] = acc_ref[g] / l_ref[g]


def _paged_attn(page_table, qabs, ckv_new, kr_new, cache_ck, cache_kr_t, *, group, pages, t_new):
    b, rows, width = qabs.shape
    n_pages = page_table.shape[1]
    n_chunks = n_pages // pages
    page, kv_lora = cache_ck.shape[1:]
    t_pad = ckv_new.shape[1]
    grid_spec = pltpu.PrefetchScalarGridSpec(
        num_scalar_prefetch=1,
        grid=(b // group, n_chunks),
        in_specs=[pl.BlockSpec((group, rows, width), lambda bi, ci, pt: (bi, 0, 0)),
                  pl.BlockSpec((group, t_pad, kv_lora), lambda bi, ci, pt: (bi, 0, 0)),
                  pl.BlockSpec((group, t_pad, QK_ROPE), lambda bi, ci, pt: (bi, 0, 0)),
                  pl.BlockSpec(memory_space=pl.ANY),
                  pl.BlockSpec(memory_space=pl.ANY)],
        out_specs=pl.BlockSpec((group, rows, kv_lora), lambda bi, ci, pt: (bi, 0, 0)),
        scratch_shapes=[pltpu.VMEM((2, group, pages * page, kv_lora), F32),
                        pltpu.VMEM((2, group, pages, QK_ROPE, page), F32),
                        pltpu.VMEM((group, rows, 1), F32), pltpu.VMEM((group, rows, 1), F32),
                        pltpu.VMEM((group, rows, kv_lora), F32),
                        pltpu.SemaphoreType.DMA((2, 2))],
    )
    return pl.pallas_call(
        functools.partial(_paged_attn_kernel, group=group, pages=pages, page=page, n_chunks=n_chunks, t_new=t_new,
                          t_pad=t_pad),
        grid_spec=grid_spec,
        out_shape=jax.ShapeDtypeStruct((b, rows, kv_lora), F32),
        compiler_params=_cparams(2),
        name="paged_attn",
    )(page_table, qabs, ckv_new, kr_new, cache_ck, cache_kr_t)


def _vup_kernel(o_ref, w_ref, out_ref):
    lat = jnp.concatenate([o_ref[0], o_ref[1]], axis=-1).astype(BF16)
    out_ref[...] = _dot(lat, w_ref[0]).astype(out_ref.dtype)


def _vup(o_lat, w_pairs):
    h, n, c = o_lat.shape
    return pl.pallas_call(
        _vup_kernel,
        grid=(h // 2,),
        in_specs=[pl.BlockSpec((2, n, c), lambda p: (p, 0, 0)),
                  pl.BlockSpec((1, 2 * c, 2 * V_DIM), lambda p: (p, 0, 0))],
        out_specs=pl.BlockSpec((n, 2 * V_DIM), lambda p: (0, p)),
        out_shape=jax.ShapeDtypeStruct((n, h * V_DIM), BF16),
        compiler_params=_cparams(1),
        name="sample_vup",
    )(o_lat, w_pairs)


def _block_diag(w):
    g, a, b = w.shape
    out = jnp.zeros((g * a, g * b), w.dtype)
    for i in range(g):
        out = out.at[i * a:(i + 1) * a, i * b:(i + 1) * b].set(w[i])
    return out


def _swap_halves(w):
    half = w.shape[-1] // 2
    return jnp.concatenate([w[..., half:], w[..., :half]], axis=-1)


def _rope_tables(pos):
    half = QK_ROPE // 2
    inv_freq = ROPE_THETA ** (-jnp.arange(half, dtype=F32) / half)
    ang = pos.astype(F32)[:, None] * inv_freq[None, :]
    cos, sin = jnp.cos(ang), jnp.sin(ang)
    n = pos.shape[0]
    ctab = jnp.concatenate([jnp.ones((n, QK_NOPE), F32), cos, cos, jnp.zeros((n, QK_ROPE), F32)], axis=1)
    stab = jnp.concatenate([jnp.zeros((n, QK_NOPE), F32), -sin, sin, jnp.zeros((n, QK_ROPE), F32)], axis=1)
    return ctab, stab


def _router_weights(w_group, b_group, w_expert, b_expert):
    d = w_group.shape[0]
    we = jnp.transpose(w_expert, (1, 0, 2)).reshape(d, N_EXPERTS)
    w = jnp.concatenate([w_group, we, jnp.zeros((d, LANES - MOE_GROUPS - N_EXPERTS), F32)], axis=1)
    bias = jnp.concatenate([b_group, b_expert.reshape(-1), jnp.zeros((LANES - MOE_GROUPS - N_EXPERTS,), F32)])
    hi = w.astype(BF16)
    lo = (w - hi.astype(F32)).astype(BF16)
    return hi, lo, bias.reshape(1, LANES)


def _pick(n, pref):
    t = min(n, pref)
    while n % t:
        t //= 2
    return t


def kernel(x_prompt, x_sample, state_pool, state_conv, state_lru, cache_kv_latent, cache_k_rope, page_table,
           c_prompt, c_sample, ada_w, ada_b, norm_mix, norm_ffn, ab_w_in, ab_w_out, pool_w, pool_scale,
           conv_w, conv_b, lru_w_r, lru_b_r, lru_w_i, lru_b_i, lru_lambda,
           mla_w_in, mla_q_norm, mla_w_uq, mla_kv_norm, mla_w_uk, mla_w_uv, mla_w_out,
           router_w_group, router_b_group, router_w_expert, router_b_expert,
           moe_w_gate, moe_w_up, moe_w_down, final_norm):
    bp, seq, d = x_prompt.shape
    bs, t_new, _ = x_sample.shape
    depth = ada_w.shape[0]
    pw = pool_scale.shape[-1]
    page = cache_kv_latent.shape[2]
    past_len = page_table.shape[1] * page
    q_lora = mla_q_norm.shape[-1]
    kv_lora = mla_kv_norm.shape[-1]

    n_c = bp + bs
    n_c_pad = -(-n_c // 8) * 8
    c_all = jnp.concatenate([c_prompt, c_sample, jnp.zeros((n_c_pad - n_c, d), F32)], axis=0)
    mod = _ada(c_all, ada_w, ada_b)

    def mods(layer, lo, hi):
        return [mod[layer, lo:hi, k * d:(k + 1) * d] for k in range(6)]

    n_p = bp * seq
    n_s = bs * t_new
    tile_p = _pick(seq, 512)
    tile_s = _pick(n_s, 512)
    gmm_tile_p = _pick(TOP_K * n_p, 256)
    gmm_tile_s = _pick(TOP_K * n_s, 256)

    xp = x_prompt.reshape(n_p, d)
    xs = x_sample.reshape(n_s, d)
    outs = {}

    for layer in range(depth):
        sh_m, sc_m, gt_m, sh_f, sc_f, gt_f = mods(layer, 0, bp)
        sh_ms, sc_ms, gt_ms, sh_fs, sc_fs, gt_fs = mods(layer, bp, bp + bs)
        rep = lambda v: jnp.repeat(v, t_new, axis=0)[None]
        per_seq = lambda v: v[:, None, :]
        rw = (norm_ffn[layer].reshape(1, d),) + _router_weights(
            router_w_group[layer], router_b_group[layer], router_w_expert[layer], router_b_expert[layer])
        experts = (moe_w_gate, moe_w_up, moe_w_down)
        final_gain = final_norm.reshape(1, d) if layer == depth - 1 else None
        ffn_mods_p = (per_seq(sh_f), per_seq(sc_f), per_seq(gt_f))
        ffn_mods_s = (rep(sh_fs), rep(sc_fs), rep(gt_fs))

        if layer % 2 == 0:
            e = layer // 2
            wts = (norm_mix[layer].reshape(1, d), ab_w_in[e].astype(BF16), _block_diag(pool_w[e]).astype(BF16),
                   pool_scale[e].reshape(1, pw), conv_w[e], conv_b[e].reshape(1, pw),
                   jnp.concatenate([_block_diag(lru_w_r[e]), _block_diag(lru_w_i[e])], axis=1).astype(BF16),
                   jnp.concatenate([lru_b_r[e], lru_b_i[e]]).reshape(1, 2 * pw),
                   lru_lambda[e].reshape(1, pw), ab_w_out[e].astype(BF16))
            x1p, pool_p, conv_p, lru_p = _mix0(
                xp.reshape(bp, seq, d), per_seq(sh_m), per_seq(sc_m), per_seq(gt_m),
                jnp.zeros((bp, POOL_HALO - 1, pw), F32), jnp.zeros((bp, CONV_WIDTH - 1, pw), F32),
                jnp.zeros((bp, 1, pw), F32), wts, tt=tile_p, bb=1, start=0)
            tm = lambda a: jnp.swapaxes(a, 0, 1).reshape(1, -1, a.shape[-1])
            x1s, pool_s, conv_s, lru_s = _mix0(
                tm(xs.reshape(bs, t_new, d)), sh_ms[None], sc_ms[None], gt_ms[None],
                tm(state_pool[e]), tm(state_conv[e]), state_lru[e][None], wts, tt=t_new, bb=bs, start=past_len)
            bm = lambda a, n: jnp.swapaxes(a.reshape(n, bs, a.shape[-1]), 0, 1)
            outs.setdefault("pool_p", []).append(pool_p)
            outs.setdefault("pool_s", []).append(bm(pool_s, POOL_HALO - 1))
            outs.setdefault("conv_p", []).append(conv_p)
            outs.setdefault("conv_s", []).append(bm(conv_s, CONV_WIDTH - 1))
            outs.setdefault("lru_p", []).append(lru_p.reshape(bp, pw))
            outs.setdefault("lru_s", []).append(lru_s.reshape(bs, pw))
            xp = x1p.reshape(n_p, d)
            xs = bm(x1s, t_new).reshape(n_s, d)
            proj_p = proj_s = None
        else:
            o = layer // 2
            gain = norm_mix[layer].reshape(1, d)
            w_in = mla_w_in[o]
            w_kr = w_in[:, q_lora + kv_lora:]
            win = jnp.concatenate([w_in[:, :q_lora + kv_lora], jnp.zeros((d, QK_NOPE), F32), w_kr,
                                   _swap_halves(w_kr)], axis=1).astype(BF16)
            wq = mla_w_uq[o].reshape(q_lora, MLA_HEADS, QK_NOPE + QK_ROPE)
            wuq = jnp.concatenate([wq, _swap_halves(wq[..., QK_NOPE:])], axis=-1).reshape(
                q_lora, MLA_HEADS * HEAD_PAD).astype(BF16)
            wuk = jnp.concatenate([mla_w_uk[o], jnp.zeros((kv_lora, MLA_HEADS, HEAD_PAD - QK_NOPE), F32)],
                                  axis=-1).reshape(kv_lora, MLA_HEADS * HEAD_PAD).astype(BF16)
            wuv = jnp.concatenate([mla_w_uv[o], jnp.zeros((kv_lora, MLA_HEADS, HEAD_PAD - V_DIM), F32)],
                                  axis=-1).reshape(kv_lora, MLA_HEADS * HEAD_PAD).astype(BF16)
            vone = jnp.tile((jnp.arange(HEAD_PAD) == V_DIM).astype(F32), MLA_HEADS).reshape(1, -1)
            qg = mla_q_norm[o].reshape(1, q_lora)
            kvg = mla_kv_norm[o].reshape(1, kv_lora)
            wout = mla_w_out[o].astype(BF16)

            ctab, stab = _rope_tables(jnp.arange(seq, dtype=jnp.int32))
            ckv_p, kr_p, q_p, k_p, v_p = _mla_proj(
                xp, per_seq(sh_m), per_seq(sc_m), gain, win, qg, wuq, kvg, ctab, stab,
                tile=tile_p, rows_per_mod=seq, tab_tiles=seq // tile_p, kv_w=(wuk, wuv, vone))
            qw = MLA_HEADS * HEAD_PAD
            ta = _pick(seq, 512)
            o_p = _attention(q_p.reshape(bp, seq, qw), k_p.reshape(bp, seq, qw), v_p.reshape(bp, seq, qw),
                             tq=ta, tk=ta)
            proj_p = (o_p.reshape(n_p, MLA_HEADS * V_DIM), wout, per_seq(gt_m))
            outs.setdefault("lat_p", []).append(ckv_p.reshape(bp, seq, kv_lora))
            outs.setdefault("rope_p", []).append(kr_p.reshape(bp, seq, QK_ROPE))

            pos_s = jnp.tile(past_len + jnp.arange(t_new, dtype=jnp.int32), bs)
            ctab_s, stab_s = _rope_tables(pos_s)
            ckv_s, kr_s, q_s = _mla_proj(
                xs, rep(sh_ms), rep(sc_ms), gain, win, qg, wuq, kvg, ctab_s, stab_s,
                tile=tile_s, rows_per_mod=n_s, tab_tiles=n_s // tile_s)
            aw = kv_lora + HEAD_PAD
            wk_t = jnp.transpose(mla_w_uk[o], (1, 2, 0))
            mats = jnp.zeros((MLA_HEADS, HEAD_PAD, aw), F32)
            mats = mats.at[:, :QK_NOPE, :kv_lora].set(wk_t)
            mats = mats.at[:, QK_NOPE:QK_NOPE + QK_ROPE, kv_lora:kv_lora + QK_ROPE].set(
                jnp.broadcast_to(jnp.eye(QK_ROPE, dtype=F32), (MLA_HEADS, QK_ROPE, QK_ROPE)))
            qabs = _qabs(q_s, mats.astype(BF16))
            qabs = jnp.transpose(qabs.reshape(MLA_HEADS, bs, t_new, aw), (1, 0, 2, 3)).reshape(
                bs, MLA_HEADS * t_new, aw)
            t_pad = 8
            pad_t = lambda a: jnp.pad(a.reshape(bs, t_new, -1), ((0, 0), (0, t_pad - t_new), (0, 0)))
            o_lat = _paged_attn(page_table, qabs, pad_t(ckv_s), pad_t(kr_s), cache_kv_latent[o],
                                jnp.swapaxes(cache_k_rope[o], 1, 2),
                                group=_pick(bs, 4), pages=_pick(page_table.shape[1], 8), t_new=t_new)
            o_lat = jnp.transpose(o_lat.reshape(bs, MLA_HEADS, t_new, kv_lora), (1, 0, 2, 3)).reshape(
                MLA_HEADS, n_s, kv_lora)
            wv = jnp.transpose(mla_w_uv[o], (1, 0, 2))
            w_pairs = jnp.zeros((MLA_HEADS // 2, 2 * kv_lora, 2 * V_DIM), F32)
            w_pairs = w_pairs.at[:, :kv_lora, :V_DIM].set(wv[0::2]).at[:, kv_lora:, V_DIM:].set(wv[1::2])
            o_s = _vup(o_lat, w_pairs.astype(BF16))
            proj_s = (o_s, wout, rep(gt_ms))
            outs.setdefault("lat_s", []).append(ckv_s.reshape(bs, t_new, kv_lora))
            outs.setdefault("rope_s", []).append(kr_s.reshape(bs, t_new, QK_ROPE))

        xp = _moe(xp, ffn_mods_p, rw, experts, layer=layer, tile=tile_p, gmm_tile=gmm_tile_p, rows_per_mod=seq,
                  proj=proj_p, final_gain=final_gain)
        xs = _moe(xs, ffn_mods_s, rw, experts, layer=layer, tile=tile_s, gmm_tile=gmm_tile_s, rows_per_mod=n_s,
                  proj=proj_s, final_gain=final_gain)

    st = lambda k: jnp.stack(outs[k])
    return (xp.reshape(bp, seq, d), xs.reshape(bs, t_new, d),
            st("pool_p"), st("pool_s"), st("conv_p"), st("conv_s"), st("lru_p"), st("lru_s"),
            st("lat_p"), st("lat_s"), st("rope_p"), st("rope_s"))
```

```python
import functools

import jax
import jax.numpy as jnp
from jax import lax
from jax.experimental import pallas as pl
from jax.experimental.pallas import tpu as pltpu

F32 = jnp.float32
BF16 = jnp.bfloat16

EPS = 1e-6
POOL_WINDOWS = (2, 4, 8, 16)
POOL_HALO = 16
CONV_WIDTH = 4
CONV_HALO = 8
LRU_C = 8.0
MLA_HEADS = 16
QK_NOPE = 64
QK_ROPE = 32
V_DIM = 64
HEAD_PAD = 128
ROPE_THETA = 10000.0
SM_SCALE = (QK_NOPE + QK_ROPE) ** -0.5
MOE_GROUPS = 4
EXPERTS_PER_GROUP = 8
N_EXPERTS = MOE_GROUPS * EXPERTS_PER_GROUP
TOP_K = 2
LANES = 128
ROW_DMA_UNROLL = 8
LOG2E = 1.4426950408889634
VMEM_LIMIT = 56 * 1024 * 1024


def _cparams(n_axes):
    return pltpu.CompilerParams(dimension_semantics=("arbitrary",) * n_axes,
                                vmem_limit_bytes=VMEM_LIMIT)


def _dot(a, b):
    return jnp.dot(a, b, preferred_element_type=F32)


def _dot_nt(a, b):
    return lax.dot_general(a, b, (((1,), (1,)), ((), ())), preferred_element_type=F32)


def _rms(x, gain):
    return x * lax.rsqrt(jnp.mean(x * x, axis=-1, keepdims=True) + EPS) * gain


def _silu(x):
    return x * jax.nn.sigmoid(x)


def _ada_kernel(c_ref, w_ref, b_ref, o_ref):
    a = _silu(c_ref[...]).astype(BF16)
    o_ref[0] = _dot(a, w_ref[0].astype(BF16)) + b_ref[0]


def _ada(c_all, ada_w, ada_b):
    depth, d, n6 = ada_w.shape
    rows = c_all.shape[0]
    tn = 1024
    return pl.pallas_call(
        _ada_kernel,
        grid=(depth, n6 // tn),
        in_specs=[pl.BlockSpec((rows, d), lambda l, j: (0, 0)),
                  pl.BlockSpec((1, d, tn), lambda l, j: (l, 0, j)),
                  pl.BlockSpec((1, 1, tn), lambda l, j: (l, 0, j))],
        out_specs=pl.BlockSpec((1, rows, tn), lambda l, j: (l, 0, j)),
        out_shape=jax.ShapeDtypeStruct((depth, rows, n6), F32),
        compiler_params=_cparams(2),
        name="ada_mod",
    )(c_all, ada_w, ada_b.reshape(depth, 1, n6))


def _mix0_kernel(x_ref, sh_ref, sc_ref, gt_ref, pool0_ref, conv0_ref, lru0_ref,
                 gain_ref, win_ref, poolw_ref, pscale_ref, convw_ref, convb_ref,
                 wri_ref, bri_ref, lam_ref, wout_ref,
                 x1_ref, pooln_ref, convn_ref, lrun_ref,
                 zp_ref, zc_ref, h_ref, *, tt, bb, start, n_t):
    t = pl.program_id(1)
    rows = tt * bb
    pw = zp_ref.shape[1]
    p0 = POOL_HALO * bb
    c0 = CONV_HALO * bb

    @pl.when(t == 0)
    def _():
        zp_ref[0:bb, :] = jnp.zeros((bb, pw), F32)
        zp_ref[bb:p0, :] = pool0_ref[0]
        zc_ref[0:c0 - (CONV_WIDTH - 1) * bb, :] = jnp.zeros((c0 - (CONV_WIDTH - 1) * bb, pw), F32)
        zc_ref[c0 - (CONV_WIDTH - 1) * bb:c0, :] = conv0_ref[0]
        h_ref[...] = lru0_ref[0]

    def per_row(v):
        return v if bb == 1 else jnp.concatenate([v] * tt, axis=0)

    x = x_ref[0]
    h = _rms(x, gain_ref[...]) * (1.0 + per_row(sc_ref[0])) + per_row(sh_ref[0])
    proj = _dot(h.astype(BF16), win_ref[...])
    u_pool = proj[:, :pw]
    u_x = proj[:, pw:2 * pw]
    u_g = proj[:, 2 * pw:]
    zp_ref[p0:p0 + rows, :] = u_pool
    zc_ref[c0:c0 + rows, :] = u_x

    if bb == 1:
        tix = lax.broadcasted_iota(jnp.int32, (rows, 1), 0)
    else:
        tix = jnp.concatenate([jnp.full((bb, 1), i, jnp.int32) for i in range(tt)], axis=0)
    pos = start + t * tt + tix

    gd = pw // len(POOL_WINDOWS)
    means = []
    for g, w in enumerate(POOL_WINDOWS):
        cols = slice(g * gd, (g + 1) * gd)
        acc = zp_ref[p0:p0 + rows, cols]
        for i in range(1, w):
            acc = acc + zp_ref[p0 - i * bb:p0 - i * bb + rows, cols]
        cnt = jnp.minimum(pos + 1, w).astype(F32)
        means.append(acc / cnt)
    pooled = jnp.concatenate(means, axis=-1) - u_pool
    y_a = _dot(pooled.astype(BF16), poolw_ref[...]) * pscale_ref[...]

    xc = convb_ref[...]
    for k in range(CONV_WIDTH):
        off = c0 - (CONV_WIDTH - 1 - k) * bb
        xc = xc + zc_ref[off:off + rows, :] * convw_ref[k:k + 1, :]
    pre = _dot(xc.astype(BF16), wri_ref[...]) + bri_ref[...]
    r = jax.nn.sigmoid(pre[:, :pw])
    gi = jax.nn.sigmoid(pre[:, pw:])
    lam = lam_ref[...]
    softplus_neg = jnp.maximum(-lam, 0.0) + jnp.log1p(jnp.exp(-jnp.abs(lam)))
    log_a = -LRU_C * r * softplus_neg
    a = jnp.exp(log_a)
    b = jnp.sqrt(1.0 - a * a) * gi * xc

    if bb == 1:
        rowi = lax.broadcasted_iota(jnp.int32, (rows, 1), 0)
        s = 1
        while s < rows:
            a_sh = pltpu.roll(a, s, 0)
            b_sh = pltpu.roll(b, s, 0)
            keep = rowi >= s
            b = jnp.where(keep, a * b_sh + b, b)
            a = jnp.where(keep, a * a_sh, a)
            s *= 2
        hs = b + a * h_ref[...]
        h_ref[...] = hs[rows - 1:rows, :]
    else:
        hprev = h_ref[...]
        parts = []
        for i in range(tt):
            hprev = a[i * bb:(i + 1) * bb] * hprev + b[i * bb:(i + 1) * bb]
            parts.append(hprev)
        hs = jnp.concatenate(parts, axis=0)
        h_ref[...] = hprev
    y_b = hs * jax.nn.gelu(u_g)

    mix = _dot(y_a.astype(BF16), wout_ref[0:pw, :]) + _dot(y_b.astype(BF16), wout_ref[pw:2 * pw, :])
    x1_ref[0] = x + per_row(gt_ref[0]) * mix

    @pl.when(t == n_t - 1)
    def _():
        pooln_ref[0] = zp_ref[p0 + rows - (POOL_HALO - 1) * bb:p0 + rows, :]
        convn_ref[0] = zc_ref[c0 + rows - (CONV_WIDTH - 1) * bb:c0 + rows, :]
        lrun_ref[0] = h_ref[...]

    if n_t > 1:
        @pl.when(t < n_t - 1)
        def _():
            zp_ref[bb:p0, :] = zp_ref[rows + bb:rows + p0, :]
            zc_ref[c0 - (CONV_WIDTH - 1) * bb:c0, :] = zc_ref[c0 + rows - (CONV_WIDTH - 1) * bb:c0 + rows, :]


def _mix0(x, sh, sc, gt, pool0, conv0, lru0, wts, *, tt, bb, start):
    nbb, tot, d = x.shape
    n_t = tot // (tt * bb)
    rows = tt * bb
    pw = pool0.shape[-1]
    hp = (POOL_HALO - 1) * bb
    hc = (CONV_WIDTH - 1) * bb

    def const(a):
        nd = a.ndim
        return pl.BlockSpec(a.shape, lambda i, j: (0,) * nd)

    def per_b(n_rows, width):
        return pl.BlockSpec((1, n_rows, width), lambda i, j: (i, 0, 0))

    kern = functools.partial(_mix0_kernel, tt=tt, bb=bb, start=start, n_t=n_t)
    return pl.pallas_call(
        kern,
        grid=(nbb, n_t),
        in_specs=[pl.BlockSpec((1, rows, d), lambda i, j: (i, j, 0)),
                  per_b(bb, d), per_b(bb, d), per_b(bb, d),
                  per_b(hp, pw), per_b(hc, pw), per_b(bb, pw)] + [const(w) for w in wts],
        out_specs=[pl.BlockSpec((1, rows, d), lambda i, j: (i, j, 0)),
                   per_b(hp, pw), per_b(hc, pw), per_b(bb, pw)],
        out_shape=[jax.ShapeDtypeStruct((nbb, tot, d), F32),
                   jax.ShapeDtypeStruct((nbb, hp, pw), F32),
                   jax.ShapeDtypeStruct((nbb, hc, pw), F32),
                   jax.ShapeDtypeStruct((nbb, bb, pw), F32)],
        scratch_shapes=[pltpu.VMEM(((POOL_HALO + tt) * bb, pw), F32),
                        pltpu.VMEM(((CONV_HALO + tt) * bb, pw), F32),
                        pltpu.VMEM((bb, pw), F32)],
        compiler_params=_cparams(2),
        name="mix0",
    )(x, sh, sc, gt, pool0, conv0, lru0, *wts)


def _route_kernel(*refs, with_proj):
    if with_proj:
        (x_ref, o_ref, wout_ref, gtm_ref, shf_ref, scf_ref, gain_ref, wrh_ref, wrl_ref, br_ref,
         x1_ref, hf_ref, info_ref, cnt_ref, run_ref) = refs
    else:
        (x_ref, shf_ref, scf_ref, gain_ref, wrh_ref, wrl_ref, br_ref,
         hf_ref, info_ref, cnt_ref, run_ref) = refs

    @pl.when(pl.program_id(0) == 0)
    def _():
        run_ref[...] = jnp.zeros(run_ref.shape, F32)

    x = x_ref[...]
    if with_proj:
        x = x + gtm_ref[0] * _dot(o_ref[...], wout_ref[...])
        x1_ref[...] = x
    hf = _rms(x, gain_ref[...]) * (1.0 + scf_ref[0]) + shf_ref[0]
    hf_ref[...] = hf

    hi = hf.astype(BF16)
    lo = (hf - hi.astype(F32)).astype(BF16)
    logits = _dot(hi, wrh_ref[...]) + _dot(lo, wrh_ref[...]) + _dot(hi, wrl_ref[...]) + br_ref[...]

    tq = logits.shape[0]
    lane = lax.broadcasted_iota(jnp.int32, (tq, LANES), 1).astype(F32)
    neg = -jnp.inf
    big = float(LANES)
    is_g = lane < MOE_GROUPS
    gl = jnp.where(is_g, logits, neg)
    mg = jnp.max(gl, axis=1, keepdims=True)
    gidx = jnp.min(jnp.where(gl == mg, lane, big), axis=1, keepdims=True)
    p_group = 1.0 / jnp.sum(jnp.where(is_g, jnp.exp(gl - mg), 0.0), axis=1, keepdims=True)
    first = MOE_GROUPS + gidx * EXPERTS_PER_GROUP
    el = jnp.where((lane >= first) & (lane < first + EXPERTS_PER_GROUP), logits, neg)
    v1 = jnp.max(el, axis=1, keepdims=True)
    i1 = jnp.min(jnp.where(el == v1, lane, big), axis=1, keepdims=True)
    el2 = jnp.where(lane == i1, neg, el)
    v2 = jnp.max(el2, axis=1, keepdims=True)
    i2 = jnp.min(jnp.where(el2 == v2, lane, big), axis=1, keepdims=True)
    ex = jnp.exp(v2 - v1)
    g1 = p_group / (1.0 + ex)
    g2 = p_group * ex / (1.0 + ex)
    e1 = i1 - MOE_GROUPS
    e2 = i2 - MOE_GROUPS

    oh1 = jnp.where(lane == e1, 1.0, 0.0)
    oh2 = jnp.where(lane == e2, 1.0, 0.0)
    oh = oh1 + oh2
    ri = lax.broadcasted_iota(jnp.int32, (tq, tq), 0)
    ci = lax.broadcasted_iota(jnp.int32, (tq, tq), 1)
    lower = jnp.where(ri > ci, 1.0, 0.0).astype(BF16)
    before = _dot(lower, oh.astype(BF16)) + run_ref[...]
    r1 = jnp.sum(before * oh1, axis=1, keepdims=True)
    r2 = jnp.sum(before * oh2, axis=1, keepdims=True)
    run_ref[...] = run_ref[...] + jnp.sum(oh, axis=0, keepdims=True)
    cnt_ref[...] = run_ref[...]

    info = jnp.where(lane == 0, e1, jnp.where(lane == 1, e2, jnp.where(lane == 2, g1, jnp.where(
        lane == 3, g2, jnp.where(lane == 4, r1, jnp.where(lane == 5, r2, 0.0))))))
    info_ref[...] = info


def _route(x, shf, scf, gain, wrh, wrl, br, *, tile, rows_per_mod, proj=None):
    n, d = x.shape
    nt = n // tile
    mrows = shf.shape[1]
    mod_spec = pl.BlockSpec((1, mrows, d), lambda i: ((i * tile) // rows_per_mod, 0, 0))
    row_spec = pl.BlockSpec((tile, d), lambda i: (i, 0))

    def const(a):
        nd = a.ndim
        return pl.BlockSpec(a.shape, lambda i: (0,) * nd)

    info_spec = pl.BlockSpec((tile, LANES), lambda i: (i, 0))
    cnt_spec = pl.BlockSpec((1, LANES), lambda i: (0, 0))
    outs_shape = [jax.ShapeDtypeStruct((n, d), F32), jax.ShapeDtypeStruct((n, LANES), F32),
                  jax.ShapeDtypeStruct((1, LANES), F32)]
    outs_spec = [row_spec, info_spec, cnt_spec]
    if proj is None:
        ins = [x, shf, scf, gain, wrh, wrl, br]
        in_specs = [row_spec, mod_spec, mod_spec, const(gain), const(wrh), const(wrl), const(br)]
    else:
        o, wout, gtm = proj
        ins = [x, o, wout, gtm, shf, scf, gain, wrh, wrl, br]
        in_specs = [row_spec, pl.BlockSpec((tile, o.shape[1]), lambda i: (i, 0)), const(wout), mod_spec,
                    mod_spec, mod_spec, const(gain), const(wrh), const(wrl), const(br)]
        outs_shape = [jax.ShapeDtypeStruct((n, d), F32)] + outs_shape
        outs_spec = [row_spec] + outs_spec
    return pl.pallas_call(
        functools.partial(_route_kernel, with_proj=proj is not None),
        grid=(nt,),
        in_specs=in_specs,
        out_specs=outs_spec,
        out_shape=outs_shape,
        scratch_shapes=[pltpu.VMEM((1, LANES), F32)],
        compiler_params=_cparams(1),
        name="route",
    )(*ins)


def _dispatch_kernel(slot_ref, hf_ref, xs_ref, sem, *, tile):
    def issue(g, c):
        for u in range(ROW_DMA_UNROLL):
            r = g * ROW_DMA_UNROLL + u
            for k in range(TOP_K):
                s = slot_ref[0, 0, TOP_K * r + k]
                pltpu.make_async_copy(hf_ref.at[pl.ds(r, 1), :], xs_ref.at[pl.ds(s, 1), :], sem).start(priority=k)
        return c

    lax.fori_loop(0, tile // ROW_DMA_UNROLL, issue, 0)
    for k in range(TOP_K):
        pltpu.make_async_copy(hf_ref, xs_ref.at[pl.ds(0, tile), :], sem).wait()


def _dispatch(hf, slots, *, tile):
    n, d = hf.shape
    nt = n // tile
    return pl.pallas_call(
        functools.partial(_dispatch_kernel, tile=tile),
        grid=(nt,),
        in_specs=[pl.BlockSpec((1, 1, TOP_K * tile), lambda i: (i, 0, 0), memory_space=pltpu.SMEM),
                  pl.BlockSpec((tile, d), lambda i: (i, 0))],
        out_specs=pl.BlockSpec(memory_space=pl.ANY),
        out_shape=jax.ShapeDtypeStruct((TOP_K * n, d), F32),
        scratch_shapes=[pltpu.SemaphoreType.DMA(())],
        compiler_params=_cparams(1),
        name="moe_dispatch",
    )(slots.reshape(nt, 1, TOP_K * tile), hf)


def _gmm_kernel(wt_ref, we_ref, wlo_ref, whi_ref, xs_ref, wg_ref, wu_ref, wd_ref, y_ref,
                wgb_ref, wub_ref, wdb_ref):
    w = pl.program_id(0)
    prev = jnp.maximum(w - 1, 0)
    new_expert = (w == 0) | (we_ref[w] != we_ref[prev])
    new_tile = (w == 0) | (wt_ref[w] != wt_ref[prev])

    @pl.when(new_expert)
    def _():
        wgb_ref[...] = wg_ref[0, 0].astype(BF16)
        wub_ref[...] = wu_ref[0, 0].astype(BF16)
        wdb_ref[...] = wd_ref[0, 0].astype(BF16)

    @pl.when(new_tile)
    def _():
        y_ref[...] = jnp.zeros(y_ref.shape, F32)

    lo = wlo_ref[w]
    hi = whi_ref[w]

    @pl.when(hi > lo)
    def _():
        x = xs_ref[...].astype(BF16)
        g = _dot(x, wgb_ref[...])
        u = _dot(x, wub_ref[...])
        yv = _dot((_silu(g) * u).astype(BF16), wdb_ref[...])
        row = lax.broadcasted_iota(jnp.int32, (x.shape[0], 1), 0)
        y_ref[...] = y_ref[...] + jnp.where((row >= lo) & (row < hi), yv, 0.0)


def _gmm(xs, work, w_gate, w_up, w_down, *, tile, layer):
    m, d = xs.shape
    ff = w_gate.shape[-1]
    n_work = work[0].shape[0]
    grid_spec = pltpu.PrefetchScalarGridSpec(
        num_scalar_prefetch=4,
        grid=(n_work,),
        in_specs=[pl.BlockSpec((tile, d), lambda w, wt, we, wlo, whi: (wt[w], 0)),
                  pl.BlockSpec((1, 1, d, ff), lambda w, wt, we, wlo, whi: (layer, we[w], 0, 0)),
                  pl.BlockSpec((1, 1, d, ff), lambda w, wt, we, wlo, whi: (layer, we[w], 0, 0)),
                  pl.BlockSpec((1, 1, ff, d), lambda w, wt, we, wlo, whi: (layer, we[w], 0, 0))],
        out_specs=pl.BlockSpec((tile, d), lambda w, wt, we, wlo, whi: (wt[w], 0)),
        scratch_shapes=[pltpu.VMEM((d, ff), BF16), pltpu.VMEM((d, ff), BF16), pltpu.VMEM((ff, d), BF16)],
    )
    return pl.pallas_call(
        _gmm_kernel,
        grid_spec=grid_spec,
        out_shape=jax.ShapeDtypeStruct((m, d), F32),
        compiler_params=_cparams(1),
        name="moe_gmm",
    )(*work, xs, w_gate, w_up, w_down)


def _work_list(counts, n_slots, tile):
    n_tiles = n_slots // tile
    n_work = n_tiles + N_EXPERTS - 1
    ends = jnp.cumsum(counts)
    starts = ends - counts
    first_tile = starts // tile
    last_tile = jnp.maximum(ends - 1, 0) // tile
    n_items = jnp.where(counts > 0, last_tile - first_tile + 1, 0)
    item_end = jnp.cumsum(n_items)
    item_start = item_end - n_items
    w = jnp.arange(n_work, dtype=jnp.int32)
    used = w < item_end[-1]
    e = jnp.minimum(jnp.sum(w[:, None] >= item_end[None, :], axis=1), N_EXPERTS - 1).astype(jnp.int32)
    t = first_tile[e] + (w - item_start[e])
    lo = jnp.maximum(starts[e], t * tile) - t * tile
    hi = jnp.minimum(ends[e], (t + 1) * tile) - t * tile
    last_e = jnp.max(jnp.where(counts > 0, jnp.arange(N_EXPERTS), 0)).astype(jnp.int32)
    wt = jnp.where(used, t, n_tiles - 1).astype(jnp.int32)
    we = jnp.where(used, e, last_e).astype(jnp.int32)
    wlo = jnp.where(used, lo, 0).astype(jnp.int32)
    whi = jnp.where(used, hi, 0).astype(jnp.int32)
    return (wt, we, wlo, whi), starts


def _combine_kernel(*refs, tile, final):
    if final:
        slot_ref, x_ref, gt_ref, info_ref, fg_ref, y_ref, out_ref, ybuf, sem = refs
    else:
        slot_ref, x_ref, gt_ref, info_ref, y_ref, out_ref, ybuf, sem = refs

    def issue(g, c):
        for u in range(ROW_DMA_UNROLL):
            r = g * ROW_DMA_UNROLL + u
            for k in range(TOP_K):
                s = slot_ref[0, 0, TOP_K * r + k]
                pltpu.make_async_copy(y_ref.at[pl.ds(s, 1), :], ybuf.at[k, pl.ds(r, 1), :], sem).start(priority=k)
        return c

    lax.fori_loop(0, tile // ROW_DMA_UNROLL, issue, 0)
    for k in range(TOP_K):
        pltpu.make_async_copy(y_ref.at[pl.ds(0, tile), :], ybuf.at[k], sem).wait()

    info = info_ref[...]
    ffn = info[:, 2:3] * ybuf[0] + info[:, 3:4] * ybuf[1]
    out = x_ref[...] + gt_ref[0] * ffn
    if final:
        out = _rms(out, fg_ref[...])
    out_ref[...] = out


def _combine(x, gt, info, y, slots, *, tile, rows_per_mod, final_gain=None):
    n, d = x.shape
    nt = n // tile
    mrows = gt.shape[1]
    final = final_gain is not None
    in_specs = [pl.BlockSpec((1, 1, TOP_K * tile), lambda i: (i, 0, 0), memory_space=pltpu.SMEM),
                pl.BlockSpec((tile, d), lambda i: (i, 0)),
                pl.BlockSpec((1, mrows, d), lambda i: ((i * tile) // rows_per_mod, 0, 0)),
                pl.BlockSpec((tile, LANES), lambda i: (i, 0))]
    ins = [slots.reshape(nt, 1, TOP_K * tile), x, gt, info]
    if final:
        in_specs.append(pl.BlockSpec(final_gain.shape, lambda i: (0, 0)))
        ins.append(final_gain)
    in_specs.append(pl.BlockSpec(memory_space=pl.ANY))
    ins.append(y)
    return pl.pallas_call(
        functools.partial(_combine_kernel, tile=tile, final=final),
        grid=(nt,),
        in_specs=in_specs,
        out_specs=pl.BlockSpec((tile, d), lambda i: (i, 0)),
        out_shape=jax.ShapeDtypeStruct((n, d), F32),
        scratch_shapes=[pltpu.VMEM((TOP_K, tile, d), F32), pltpu.SemaphoreType.DMA(())],
        compiler_params=_cparams(1),
        name="moe_combine",
    )(*ins)


def _moe(x, mods, rw, experts, *, layer, tile, gmm_tile, rows_per_mod, proj=None, final_gain=None):
    shf, scf, gtf = mods
    gain, wrh, wrl, br = rw
    res = _route(x, shf, scf, gain, wrh, wrl, br, tile=tile, rows_per_mod=rows_per_mod, proj=proj)
    if proj is not None:
        x, hf, info, cnt = res
    else:
        hf, info, cnt = res
    n = x.shape[0]
    counts = cnt[0, :N_EXPERTS].astype(jnp.int32)
    work, starts = _work_list(counts, TOP_K * n, gmm_tile)
    eid = info[:, 0:TOP_K].astype(jnp.int32)
    rank = info[:, 4:4 + TOP_K].astype(jnp.int32)
    is_e = eid[:, :, None] == jnp.arange(N_EXPERTS, dtype=jnp.int32)
    slots = (jnp.sum(jnp.where(is_e, starts.astype(jnp.int32), 0), axis=-1) + rank).reshape(-1)
    xs = _dispatch(hf, slots, tile=tile)
    y = _gmm(xs, work, *experts, tile=gmm_tile, layer=layer)
    return _combine(x, gtf, info, y, slots, tile=tile, rows_per_mod=rows_per_mod, final_gain=final_gain)


def _rope_turn(blk, c, s):
    return blk * c + pltpu.roll(blk, LANES - QK_ROPE, 1) * s


def _mla_proj_kernel(*refs, sample):
    if sample:
        (x_ref, sh_ref, sc_ref, gain_ref, win_ref, qg_ref, wuq_ref, kvg_ref, c_ref, s_ref,
         ckv_ref, kr_ref, q_ref) = refs
    else:
        (x_ref, sh_ref, sc_ref, gain_ref, win_ref, qg_ref, wuq_ref, kvg_ref, c_ref, s_ref, wuk_ref, wuv_ref,
         vone_ref, ckv_ref, kr_ref, q_ref, k_ref, v_ref) = refs
    q_lora = qg_ref.shape[1]
    kv_lora = kvg_ref.shape[1]
    h = _rms(x_ref[...], gain_ref[...]) * (1.0 + sc_ref[0]) + sh_ref[0]
    proj = _dot(h.astype(BF16), win_ref[...])
    qn = _rms(proj[:, :q_lora], qg_ref[...])
    ckv = _rms(proj[:, q_lora:q_lora + kv_lora], kvg_ref[...])
    c = c_ref[...]
    s = s_ref[...]
    kf = _rope_turn(proj[:, q_lora + kv_lora:], c, s)
    ckv_ref[...] = ckv
    kr_ref[...] = kf[:, QK_NOPE:QK_NOPE + QK_ROPE]
    q = _dot(qn.astype(BF16), wuq_ref[...])
    if sample:
        for hh in range(MLA_HEADS):
            cols = slice(hh * HEAD_PAD, (hh + 1) * HEAD_PAD)
            q_ref[:, cols] = _rope_turn(q[:, cols], c, s) * SM_SCALE
    else:
        ckv_b = ckv.astype(BF16)
        kn = _dot(ckv_b, wuk_ref[...])
        v_ref[...] = (_dot(ckv_b, wuv_ref[...]) + vone_ref[...]).astype(BF16)
        for hh in range(MLA_HEADS):
            cols = slice(hh * HEAD_PAD, (hh + 1) * HEAD_PAD)
            q_ref[:, cols] = (_rope_turn(q[:, cols], c, s) * (SM_SCALE * LOG2E)).astype(BF16)
            k_ref[:, cols] = (kn[:, cols] + kf).astype(BF16)


def _mla_proj(x, sh, sc, gain, win, qg, wuq, kvg, ctab, stab, *, tile, rows_per_mod, tab_tiles, kv_w=None):
    n, d = x.shape
    nt = n // tile
    mrows = sh.shape[1]
    sample = kv_w is None
    kv_lora = kvg.shape[1]

    def const(a):
        nd = a.ndim
        return pl.BlockSpec(a.shape, lambda i: (0,) * nd)

    row = lambda width: pl.BlockSpec((tile, width), lambda i: (i, 0))
    mod_spec = pl.BlockSpec((1, mrows, d), lambda i: ((i * tile) // rows_per_mod, 0, 0))
    tab_spec = pl.BlockSpec((tile, LANES), lambda i: (i % tab_tiles, 0))
    ins = [x, sh, sc, gain, win, qg, wuq, kvg, ctab, stab]
    in_specs = [row(d), mod_spec, mod_spec, const(gain), const(win), const(qg), const(wuq), const(kvg),
                tab_spec, tab_spec]
    qw = MLA_HEADS * HEAD_PAD
    out_shape = [jax.ShapeDtypeStruct((n, kv_lora), F32), jax.ShapeDtypeStruct((n, QK_ROPE), F32)]
    out_specs = [row(kv_lora), row(QK_ROPE)]
    if sample:
        out_shape.append(jax.ShapeDtypeStruct((n, qw), F32))
        out_specs.append(row(qw))
    else:
        ins += list(kv_w)
        in_specs += [const(a) for a in kv_w]
        out_shape += [jax.ShapeDtypeStruct((n, qw), BF16)] * 3
        out_specs += [row(qw)] * 3
    return pl.pallas_call(
        functools.partial(_mla_proj_kernel, sample=sample),
        grid=(nt,),
        in_specs=in_specs,
        out_specs=out_specs,
        out_shape=out_shape,
        compiler_params=_cparams(1),
        name="mla_proj",
    )(*ins)


def _attn_kernel(q_ref, k_ref, v_ref, o_ref, sa_ref, sb_ref, m_ref, acc_ref, *, tq, tk):
    i = pl.program_id(2)
    heads = range(2)

    def cols(hh):
        return slice(hh * HEAD_PAD, (hh + 1) * HEAD_PAD)

    def scores(j, dst_ref):
        rows = pl.ds(pl.multiple_of(j * tk, tk), tk)
        for hh in heads:
            dst_ref[hh] = _dot_nt(q_ref[0, :, cols(hh)], k_ref[0, rows, cols(hh)])

    def absorb(j, src_ref, masked):
        rows = pl.ds(pl.multiple_of(j * tk, tk), tk)
        if masked:
            keep = lax.broadcasted_iota(jnp.int32, (tq, tk), 1) <= lax.broadcasted_iota(jnp.int32, (tq, tk), 0)
        for hh in heads:
            s = src_ref[hh]
            if masked:
                s = jnp.where(keep, s, -jnp.inf)
            m = m_ref[hh]
            m_new = jnp.maximum(m, jnp.max(s, axis=1, keepdims=True))
            p = jnp.exp2(s - jnp.concatenate([m_new] * (tk // LANES), axis=1))
            acc_ref[hh] = jnp.exp2(m - m_new) * acc_ref[hh] + _dot(p.astype(BF16), v_ref[0, rows, cols(hh)])
            m_ref[hh] = m_new

    m_ref[...] = jnp.full(m_ref.shape, -jnp.inf, F32)
    acc_ref[...] = jnp.zeros(acc_ref.shape, F32)
    scores(0, sa_ref)

    def pair(jj, c):
        j = 2 * jj
        scores(j + 1, sb_ref)
        absorb(j, sa_ref, False)
        scores(j + 2, sa_ref)
        absorb(j + 1, sb_ref, False)
        return c

    lax.fori_loop(0, i // 2, pair, 0)

    @pl.when(i % 2 == 1)
    def _():
        scores(i, sb_ref)
        absorb(i - 1, sa_ref, False)
        absorb(i, sb_ref, True)

    @pl.when(i % 2 == 0)
    def _():
        absorb(i, sa_ref, True)

    res = [acc_ref[hh] / acc_ref[hh][:, V_DIM:V_DIM + 1] for hh in heads]
    lane = lax.broadcasted_iota(jnp.int32, (tq, HEAD_PAD), 1)
    o_ref[0] = jnp.where(lane < V_DIM, res[0], pltpu.roll(res[1], V_DIM, 1)).astype(o_ref.dtype)


def _attention(q, k, v, *, tq, tk):
    assert tq == tk, "one diagonal key tile per query tile"
    b, s, _ = q.shape
    pairs = MLA_HEADS // 2
    return pl.pallas_call(
        functools.partial(_attn_kernel, tq=tq, tk=tk),
        grid=(b, pairs, s // tq),
        in_specs=[pl.BlockSpec((1, tq, 2 * HEAD_PAD), lambda bi, p, i: (bi, i, p)),
                  pl.BlockSpec((1, s, 2 * HEAD_PAD), lambda bi, p, i: (bi, 0, p)),
                  pl.BlockSpec((1, s, 2 * HEAD_PAD), lambda bi, p, i: (bi, 0, p))],
        out_specs=pl.BlockSpec((1, tq, 2 * V_DIM), lambda bi, p, i: (bi, i, p)),
        out_shape=jax.ShapeDtypeStruct((b, s, MLA_HEADS * V_DIM), BF16),
        scratch_shapes=[pltpu.VMEM((2, tq, tk), F32), pltpu.VMEM((2, tq, tk), F32),
                        pltpu.VMEM((2, tq, LANES), F32), pltpu.VMEM((2, tq, HEAD_PAD), F32)],
        compiler_params=_cparams(3),
        name="prompt_attn",
    )(q, k, v)


def _qabs_kernel(q_ref, m_ref, o_ref):
    o_ref[0] = _dot(q_ref[...].astype(BF16), m_ref[0])


def _qabs(q, mats):
    n = q.shape[0]
    width = mats.shape[-1]
    return pl.pallas_call(
        _qabs_kernel,
        grid=(MLA_HEADS,),
        in_specs=[pl.BlockSpec((n, HEAD_PAD), lambda h: (0, h)),
                  pl.BlockSpec((1, HEAD_PAD, width), lambda h: (h, 0, 0))],
        out_specs=pl.BlockSpec((1, n, width), lambda h: (h, 0, 0)),
        out_shape=jax.ShapeDtypeStruct((MLA_HEADS, n, width), F32),
        compiler_params=_cparams(1),
        name="sample_qabs",
    )(q, mats)


def _paged_attn_kernel(pt_ref, q_ref, cn_ref, rn_ref, ck_hbm, kr_hbm, o_ref,
                       ckbuf, krbuf, m_ref, l_ref, acc_ref, sem, *, group, pages, page, n_chunks, t_new, t_pad):
    bg = pl.program_id(0)
    c = pl.program_id(1)
    step = bg * n_chunks + c
    slot = step % 2
    kv_lora = ckbuf.shape[-1]

    def start_chunk(bgi, ci, sl):
        for g in range(group):
            for p in range(pages):
                phys = pt_ref[bgi * group + g, ci * pages + p]
                pltpu.make_async_copy(ck_hbm.at[phys], ckbuf.at[sl, g, pl.ds(p * page, page), :],
                                      sem.at[0, sl]).start(priority=p % 2)
                pltpu.make_async_copy(kr_hbm.at[phys], krbuf.at[sl, g, p], sem.at[1, sl]).start(priority=(p + 1) % 2)

    @pl.when(step == 0)
    def _():
        start_chunk(bg, c, slot)

    pltpu.make_async_copy(ckbuf.at[slot], ckbuf.at[slot], sem.at[0, slot]).wait()
    pltpu.make_async_copy(krbuf.at[slot], krbuf.at[slot], sem.at[1, slot]).wait()

    @pl.when(c == 0)
    def _():
        m_ref[...] = jnp.full(m_ref.shape, -jnp.inf, F32)
        l_ref[...] = jnp.zeros(l_ref.shape, F32)
        acc_ref[...] = jnp.zeros(acc_ref.shape, F32)

    def absorb(g, s, values):
        m = m_ref[g]
        m_new = jnp.maximum(m, jnp.max(s, axis=1, keepdims=True))
        alpha = jnp.exp(m - m_new)
        p = jnp.exp(s - m_new)
        l_ref[g] = alpha * l_ref[g] + jnp.sum(p, axis=1, keepdims=True)
        acc_ref[g] = alpha * acc_ref[g] + _dot(p.astype(BF16), values)
        m_ref[g] = m_new

    cks, scores = [], []
    for g in range(group):
        qa = q_ref[g]
        q_lat = qa[:, :kv_lora].astype(BF16)
        q_rope = qa[:, kv_lora:kv_lora + QK_ROPE].astype(BF16)
        ck = ckbuf[slot, g].astype(BF16)
        kr_t = jnp.concatenate([krbuf[slot, g, p] for p in range(pages)], axis=1).astype(BF16)
        cks.append(ck)
        scores.append(_dot_nt(q_lat, ck) + _dot(q_rope, kr_t))

    nxt = step + 1

    @pl.when(nxt < pl.num_programs(0) * n_chunks)
    def _():
        start_chunk(nxt // n_chunks, nxt % n_chunks, 1 - slot)

    for g in range(group):
        absorb(g, scores[g], cks[g])

    @pl.when(c == n_chunks - 1)
    def _():
        rows = q_ref.shape[1]
        t_q = lax.broadcasted_iota(jnp.int32, (rows, t_pad), 0) % t_new
        t_k = lax.broadcasted_iota(jnp.int32, (rows, t_pad), 1)
        for g in range(group):
            qa = q_ref[g]
            q_lat = qa[:, :kv_lora].astype(BF16)
            q_rope = qa[:, kv_lora:kv_lora + QK_ROPE].astype(BF16)
            cn = cn_ref[g].astype(BF16)
            s = _dot_nt(q_lat, cn) + _dot_nt(q_rope, rn_ref[g].astype(BF16))
            absorb(g, jnp.where(t_k <= t_q, s, -jnp.inf), cn)
            o_ref[g] = acc_ref[g] / l_ref[g]


def _paged_attn(page_table, qabs, ckv_new, kr_new, cache_ck, cache_kr_t, *, group, pages, t_new):
    b, rows, width = qabs.shape
    n_pages = page_table.shape[1]
    n_chunks = n_pages // pages
    page, kv_lora = cache_ck.shape[1:]
    t_pad = ckv_new.shape[1]
    grid_spec = pltpu.PrefetchScalarGridSpec(
        num_scalar_prefetch=1,
        grid=(b // group, n_chunks),
        in_specs=[pl.BlockSpec((group, rows, width), lambda bi, ci, pt: (bi, 0, 0)),
                  pl.BlockSpec((group, t_pad, kv_lora), lambda bi, ci, pt: (bi, 0, 0)),
                  pl.BlockSpec((group, t_pad, QK_ROPE), lambda bi, ci, pt: (bi, 0, 0)),
                  pl.BlockSpec(memory_space=pl.ANY),
                  pl.BlockSpec(memory_space=pl.ANY)],
        out_specs=pl.BlockSpec((group, rows, kv_lora), lambda bi, ci, pt: (bi, 0, 0)),
        scratch_shapes=[pltpu.VMEM((2, group, pages * page, kv_lora), F32),
                        pltpu.VMEM((2, group, pages, QK_ROPE, page), F32),
                        pltpu.VMEM((group, rows, 1), F32), pltpu.VMEM((group, rows, 1), F32),
                        pltpu.VMEM((group, rows, kv_lora), F32),
                        pltpu.SemaphoreType.DMA((2, 2))],
    )
    return pl.pallas_call(
        functools.partial(_paged_attn_kernel, group=group, pages=pages, page=page, n_chunks=n_chunks, t_new=t_new,
                          t_pad=t_pad),
        grid_spec=grid_spec,
        out_shape=jax.ShapeDtypeStruct((b, rows, kv_lora), F32),
        compiler_params=_cparams(2),
        name="paged_attn",
    )(page_table, qabs, ckv_new, kr_new, cache_ck, cache_kr_t)


def _vup_kernel(o_ref, w_ref, out_ref):
    lat = jnp.concatenate([o_ref[0], o_ref[1]], axis=-1).astype(BF16)
    out_ref[...] = _dot(lat, w_ref[0]).astype(out_ref.dtype)


def _vup(o_lat, w_pairs):
    h, n, c = o_lat.shape
    return pl.pallas_call(
        _vup_kernel,
        grid=(h // 2,),
        in_specs=[pl.BlockSpec((2, n, c), lambda p: (p, 0, 0)),
                  pl.BlockSpec((1, 2 * c, 2 * V_DIM), lambda p: (p, 0, 0))],
        out_specs=pl.BlockSpec((n, 2 * V_DIM), lambda p: (0, p)),
        out_shape=jax.ShapeDtypeStruct((n, h * V_DIM), BF16),
        compiler_params=_cparams(1),
        name="sample_vup",
    )(o_lat, w_pairs)


def _block_diag(w):
    g, a, b = w.shape
    out = jnp.zeros((g * a, g * b), w.dtype)
    for i in range(g):
        out = out.at[i * a:(i + 1) * a, i * b:(i + 1) * b].set(w[i])
    return out


def _swap_halves(w):
    half = w.shape[-1] // 2
    return jnp.concatenate([w[..., half:], w[..., :half]], axis=-1)


def _rope_tables(pos):
    half = QK_ROPE // 2
    inv_freq = ROPE_THETA ** (-jnp.arange(half, dtype=F32) / half)
    ang = pos.astype(F32)[:, None] * inv_freq[None, :]
    cos, sin = jnp.cos(ang), jnp.sin(ang)
    n = pos.shape[0]
    ctab = jnp.concatenate([jnp.ones((n, QK_NOPE), F32), cos, cos, jnp.zeros((n, QK_ROPE), F32)], axis=1)
    stab = jnp.concatenate([jnp.zeros((n, QK_NOPE), F32), -sin, sin, jnp.zeros((n, QK_ROPE), F32)], axis=1)
    return ctab, stab


def _router_weights(w_group, b_group, w_expert, b_expert):
    d = w_group.shape[0]
    we = jnp.transpose(w_expert, (1, 0, 2)).reshape(d, N_EXPERTS)
    w = jnp.concatenate([w_group, we, jnp.zeros((d, LANES - MOE_GROUPS - N_EXPERTS), F32)], axis=1)
    bias = jnp.concatenate([b_group, b_expert.reshape(-1), jnp.zeros((LANES - MOE_GROUPS - N_EXPERTS,), F32)])
    hi = w.astype(BF16)
    lo = (w - hi.astype(F32)).astype(BF16)
    return hi, lo, bias.reshape(1, LANES)


def _pick(n, pref):
    t = min(n, pref)
    while n % t:
        t //= 2
    return t


def kernel(x_prompt, x_sample, state_pool, state_conv, state_lru, cache_kv_latent, cache_k_rope, page_table,
           c_prompt, c_sample, ada_w, ada_b, norm_mix, norm_ffn, ab_w_in, ab_w_out, pool_w, pool_scale,
           conv_w, conv_b, lru_w_r, lru_b_r, lru_w_i, lru_b_i, lru_lambda,
           mla_w_in, mla_q_norm, mla_w_uq, mla_kv_norm, mla_w_uk, mla_w_uv, mla_w_out,
           router_w_group, router_b_group, router_w_expert, router_b_expert,
           moe_w_gate, moe_w_up, moe_w_down, final_norm):
    bp, seq, d = x_prompt.shape
    bs, t_new, _ = x_sample.shape
    depth = ada_w.shape[0]
    pw = pool_scale.shape[-1]
    page = cache_kv_latent.shape[2]
    past_len = page_table.shape[1] * page
    q_lora = mla_q_norm.shape[-1]
    kv_lora = mla_kv_norm.shape[-1]

    n_c = bp + bs
    n_c_pad = -(-n_c // 8) * 8
    c_all = jnp.concatenate([c_prompt, c_sample, jnp.zeros((n_c_pad - n_c, d), F32)], axis=0)
    mod = _ada(c_all, ada_w, ada_b)

    def mods(layer, lo, hi):
        return [mod[layer, lo:hi, k * d:(k + 1) * d] for k in range(6)]

    n_p = bp * seq
    n_s = bs * t_new
    tile_p = _pick(seq, 512)
    tile_s = _pick(n_s, 512)
    gmm_tile_p = _pick(TOP_K * n_p, 256)
    gmm_tile_s = _pick(TOP_K * n_s, 256)

    xp = x_prompt.reshape(n_p, d)
    xs = x_sample.reshape(n_s, d)
    outs = {}

    for layer in range(depth):
        sh_m, sc_m, gt_m, sh_f, sc_f, gt_f = mods(layer, 0, bp)
        sh_ms, sc_ms, gt_ms, sh_fs, sc_fs, gt_fs = mods(layer, bp, bp + bs)
        rep = lambda v: jnp.repeat(v, t_new, axis=0)[None]
        per_seq = lambda v: v[:, None, :]
        rw = (norm_ffn[layer].reshape(1, d),) + _router_weights(
            router_w_group[layer], router_b_group[layer], router_w_expert[layer], router_b_expert[layer])
        experts = (moe_w_gate, moe_w_up, moe_w_down)
        final_gain = final_norm.reshape(1, d) if layer == depth - 1 else None
        ffn_mods_p = (per_seq(sh_f), per_seq(sc_f), per_seq(gt_f))
        ffn_mods_s = (rep(sh_fs), rep(sc_fs), rep(gt_fs))

        if layer % 2 == 0:
            e = layer // 2
            wts = (norm_mix[layer].reshape(1, d), ab_w_in[e].astype(BF16), _block_diag(pool_w[e]).astype(BF16),
                   pool_scale[e].reshape(1, pw), conv_w[e], conv_b[e].reshape(1, pw),
                   jnp.concatenate([_block_diag(lru_w_r[e]), _block_diag(lru_w_i[e])], axis=1).astype(BF16),
                   jnp.concatenate([lru_b_r[e], lru_b_i[e]]).reshape(1, 2 * pw),
                   lru_lambda[e].reshape(1, pw), ab_w_out[e].astype(BF16))
            x1p, pool_p, conv_p, lru_p = _mix0(
                xp.reshape(bp, seq, d), per_seq(sh_m), per_seq(sc_m), per_seq(gt_m),
                jnp.zeros((bp, POOL_HALO - 1, pw), F32), jnp.zeros((bp, CONV_WIDTH - 1, pw), F32),
                jnp.zeros((bp, 1, pw), F32), wts, tt=tile_p, bb=1, start=0)
            tm = lambda a: jnp.swapaxes(a, 0, 1).reshape(1, -1, a.shape[-1])
            x1s, pool_s, conv_s, lru_s = _mix0(
                tm(xs.reshape(bs, t_new, d)), sh_ms[None], sc_ms[None], gt_ms[None],
                tm(state_pool[e]), tm(state_conv[e]), state_lru[e][None], wts, tt=t_new, bb=bs, start=past_len)
            bm = lambda a, n: jnp.swapaxes(a.reshape(n, bs, a.shape[-1]), 0, 1)
            outs.setdefault("pool_p", []).append(pool_p)
            outs.setdefault("pool_s", []).append(bm(pool_s, POOL_HALO - 1))
            outs.setdefault("conv_p", []).append(conv_p)
            outs.setdefault("conv_s", []).append(bm(conv_s, CONV_WIDTH - 1))
            outs.setdefault("lru_p", []).append(lru_p.reshape(bp, pw))
            outs.setdefault("lru_s", []).append(lru_s.reshape(bs, pw))
            xp = x1p.reshape(n_p, d)
            xs = bm(x1s, t_new).reshape(n_s, d)
            proj_p = proj_s = None
        else:
            o = layer // 2
            gain = norm_mix[layer].reshape(1, d)
            w_in = mla_w_in[o]
            w_kr = w_in[:, q_lora + kv_lora:]
            win = jnp.concatenate([w_in[:, :q_lora + kv_lora], jnp.zeros((d, QK_NOPE), F32), w_kr,
                                   _swap_halves(w_kr)], axis=1).astype(BF16)
            wq = mla_w_uq[o].reshape(q_lora, MLA_HEADS, QK_NOPE + QK_ROPE)
            wuq = jnp.concatenate([wq, _swap_halves(wq[..., QK_NOPE:])], axis=-1).reshape(
                q_lora, MLA_HEADS * HEAD_PAD).astype(BF16)
            wuk = jnp.concatenate([mla_w_uk[o], jnp.zeros((kv_lora, MLA_HEADS, HEAD_PAD - QK_NOPE), F32)],
                                  axis=-1).reshape(kv_lora, MLA_HEADS * HEAD_PAD).astype(BF16)
            wuv = jnp.concatenate([mla_w_uv[o], jnp.zeros((kv_lora, MLA_HEADS, HEAD_PAD - V_DIM), F32)],
                                  axis=-1).reshape(kv_lora, MLA_HEADS * HEAD_PAD).astype(BF16)
            vone = jnp.tile((jnp.arange(HEAD_PAD) == V_DIM).astype(F32), MLA_HEADS).reshape(1, -1)
            qg = mla_q_norm[o].reshape(1, q_lora)
            kvg = mla_kv_norm[o].reshape(1, kv_lora)
            wout = mla_w_out[o].astype(BF16)

            ctab, stab = _rope_tables(jnp.arange(seq, dtype=jnp.int32))
            ckv_p, kr_p, q_p, k_p, v_p = _mla_proj(
                xp, per_seq(sh_m), per_seq(sc_m), gain, win, qg, wuq, kvg, ctab, stab,
                tile=tile_p, rows_per_mod=seq, tab_tiles=seq // tile_p, kv_w=(wuk, wuv, vone))
            qw = MLA_HEADS * HEAD_PAD
            ta = _pick(seq, 512)
            o_p = _attention(q_p.reshape(bp, seq, qw), k_p.reshape(bp, seq, qw), v_p.reshape(bp, seq, qw),
                             tq=ta, tk=ta)
            proj_p = (o_p.reshape(n_p, MLA_HEADS * V_DIM), wout, per_seq(gt_m))
            outs.setdefault("lat_p", []).append(ckv_p.reshape(bp, seq, kv_lora))
            outs.setdefault("rope_p", []).append(kr_p.reshape(bp, seq, QK_ROPE))

            pos_s = jnp.tile(past_len + jnp.arange(t_new, dtype=jnp.int32), bs)
            ctab_s, stab_s = _rope_tables(pos_s)
            ckv_s, kr_s, q_s = _mla_proj(
                xs, rep(sh_ms), rep(sc_ms), gain, win, qg, wuq, kvg, ctab_s, stab_s,
                tile=tile_s, rows_per_mod=n_s, tab_tiles=n_s // tile_s)
            aw = kv_lora + HEAD_PAD
            wk_t = jnp.transpose(mla_w_uk[o], (1, 2, 0))
            mats = jnp.zeros((MLA_HEADS, HEAD_PAD, aw), F32)
            mats = mats.at[:, :QK_NOPE, :kv_lora].set(wk_t)
            mats = mats.at[:, QK_NOPE:QK_NOPE + QK_ROPE, kv_lora:kv_lora + QK_ROPE].set(
                jnp.broadcast_to(jnp.eye(QK_ROPE, dtype=F32), (MLA_HEADS, QK_ROPE, QK_ROPE)))
            qabs = _qabs(q_s, mats.astype(BF16))
            qabs = jnp.transpose(qabs.reshape(MLA_HEADS, bs, t_new, aw), (1, 0, 2, 3)).reshape(
                bs, MLA_HEADS * t_new, aw)
            t_pad = 8
            pad_t = lambda a: jnp.pad(a.reshape(bs, t_new, -1), ((0, 0), (0, t_pad - t_new), (0, 0)))
            o_lat = _paged_attn(page_table, qabs, pad_t(ckv_s), pad_t(kr_s), cache_kv_latent[o],
                                jnp.swapaxes(cache_k_rope[o], 1, 2),
                                group=_pick(bs, 4), pages=_pick(page_table.shape[1], 8), t_new=t_new)
            o_lat = jnp.transpose(o_lat.reshape(bs, MLA_HEADS, t_new, kv_lora), (1, 0, 2, 3)).reshape(
                MLA_HEADS, n_s, kv_lora)
            wv = jnp.transpose(mla_w_uv[o], (1, 0, 2))
            w_pairs = jnp.zeros((MLA_HEADS // 2, 2 * kv_lora, 2 * V_DIM), F32)
            w_pairs = w_pairs.at[:, :kv_lora, :V_DIM].set(wv[0::2]).at[:, kv_lora:, V_DIM:].set(wv[1::2])
            o_s = _vup(o_lat, w_pairs.astype(BF16))
            proj_s = (o_s, wout, rep(gt_ms))
            outs.setdefault("lat_s", []).append(ckv_s.reshape(bs, t_new, kv_lora))
            outs.setdefault("rope_s", []).append(kr_s.reshape(bs, t_new, QK_ROPE))

        xp = _moe(xp, ffn_mods_p, rw, experts, layer=layer, tile=tile_p, gmm_tile=gmm_tile_p, rows_per_mod=seq,
                  proj=proj_p, final_gain=final_gain)
        xs = _moe(xs, ffn_mods_s, rw, experts, layer=layer, tile=tile_s, gmm_tile=gmm_tile_s, rows_per_mod=n_s,
                  proj=proj_s, final_gain=final_gain)

    st = lambda k: jnp.stack(outs[k])
    return (xp.reshape(bp, seq, d), xs.reshape(bs, t_new, d),
            st("pool_p"), st("pool_s"), st("conv_p"), st("conv_s"), st("lru_p"), st("lru_s"),
            st("lat_p"), st("lat_s"), st("rope_p"), st("rope_s"))
```

```python
import functools

import jax
import jax.numpy as jnp
from jax import lax
from jax.experimental import pallas as pl
from jax.experimental.pallas import tpu as pltpu

F32 = jnp.float32
BF16 = jnp.bfloat16

EPS = 1e-6
POOL_WINDOWS = (2, 4, 8, 16)
POOL_HALO = 16
CONV_WIDTH = 4
CONV_HALO = 8
LRU_C = 8.0
MLA_HEADS = 16
QK_NOPE = 64
QK_ROPE = 32
V_DIM = 64
HEAD_PAD = 128
ROPE_THETA = 10000.0
SM_SCALE = (QK_NOPE + QK_ROPE) ** -0.5
MOE_GROUPS = 4
EXPERTS_PER_GROUP = 8
N_EXPERTS = MOE_GROUPS * EXPERTS_PER_GROUP
TOP_K = 2
LANES = 128
ROW_DMA_UNROLL = 8
PAGE_RING = 3
LOG2E = 1.4426950408889634
VMEM_LIMIT = 56 * 1024 * 1024


def _cparams(n_axes):
    return pltpu.CompilerParams(dimension_semantics=("arbitrary",) * n_axes,
                                vmem_limit_bytes=VMEM_LIMIT)


def _dot(a, b):
    return jnp.dot(a, b, preferred_element_type=F32)


def _dot_nt(a, b):
    return lax.dot_general(a, b, (((1,), (1,)), ((), ())), preferred_element_type=F32)


def _rms(x, gain):
    return x * lax.rsqrt(jnp.mean(x * x, axis=-1, keepdims=True) + EPS) * gain


def _silu(x):
    return x * jax.nn.sigmoid(x)


def _ada_kernel(c_ref, w_ref, b_ref, o_ref):
    a = _silu(c_ref[...]).astype(BF16)
    o_ref[0] = _dot(a, w_ref[0].astype(BF16)) + b_ref[0]


def _ada(c_all, ada_w, ada_b):
    depth, d, n6 = ada_w.shape
    rows = c_all.shape[0]
    tn = 1024
    return pl.pallas_call(
        _ada_kernel,
        grid=(depth, n6 // tn),
        in_specs=[pl.BlockSpec((rows, d), lambda l, j: (0, 0)),
                  pl.BlockSpec((1, d, tn), lambda l, j: (l, 0, j)),
                  pl.BlockSpec((1, 1, tn), lambda l, j: (l, 0, j))],
        out_specs=pl.BlockSpec((1, rows, tn), lambda l, j: (l, 0, j)),
        out_shape=jax.ShapeDtypeStruct((depth, rows, n6), F32),
        compiler_params=_cparams(2),
        name="ada_mod",
    )(c_all, ada_w, ada_b.reshape(depth, 1, n6))


def _mix0_kernel(x_ref, sh_ref, sc_ref, gt_ref, pool0_ref, conv0_ref, lru0_ref,
                 gain_ref, win_ref, poolw_ref, pscale_ref, convw_ref, convb_ref,
                 wri_ref, bri_ref, lam_ref, wout_ref,
                 x1_ref, pooln_ref, convn_ref, lrun_ref,
                 zp_ref, zc_ref, h_ref, *, tt, bb, start, n_t):
    t = pl.program_id(1)
    rows = tt * bb
    pw = zp_ref.shape[1]
    p0 = POOL_HALO * bb
    c0 = CONV_HALO * bb

    @pl.when(t == 0)
    def _():
        zp_ref[0:bb, :] = jnp.zeros((bb, pw), F32)
        zp_ref[bb:p0, :] = pool0_ref[0]
        zc_ref[0:c0 - (CONV_WIDTH - 1) * bb, :] = jnp.zeros((c0 - (CONV_WIDTH - 1) * bb, pw), F32)
        zc_ref[c0 - (CONV_WIDTH - 1) * bb:c0, :] = conv0_ref[0]
        h_ref[...] = lru0_ref[0]

    def per_row(v):
        return v if bb == 1 else jnp.concatenate([v] * tt, axis=0)

    x = x_ref[0]
    h = _rms(x, gain_ref[...]) * (1.0 + per_row(sc_ref[0])) + per_row(sh_ref[0])
    proj = _dot(h.astype(BF16), win_ref[...])
    u_pool = proj[:, :pw]
    u_x = proj[:, pw:2 * pw]
    u_g = proj[:, 2 * pw:]
    zp_ref[p0:p0 + rows, :] = u_pool
    zc_ref[c0:c0 + rows, :] = u_x

    if bb == 1:
        tix = lax.broadcasted_iota(jnp.int32, (rows, 1), 0)
    else:
        tix = jnp.concatenate([jnp.full((bb, 1), i, jnp.int32) for i in range(tt)], axis=0)
    pos = start + t * tt + tix

    gd = pw // len(POOL_WINDOWS)
    means = []
    for g, w in enumerate(POOL_WINDOWS):
        cols = slice(g * gd, (g + 1) * gd)
        acc = zp_ref[p0:p0 + rows, cols]
        for i in range(1, w):
            acc = acc + zp_ref[p0 - i * bb:p0 - i * bb + rows, cols]
        cnt = jnp.minimum(pos + 1, w).astype(F32)
        means.append(acc / cnt)
    pooled = jnp.concatenate(means, axis=-1) - u_pool
    y_a = _dot(pooled.astype(BF16), poolw_ref[...]) * pscale_ref[...]

    xc = convb_ref[...]
    for k in range(CONV_WIDTH):
        off = c0 - (CONV_WIDTH - 1 - k) * bb
        xc = xc + zc_ref[off:off + rows, :] * convw_ref[k:k + 1, :]
    pre = _dot(xc.astype(BF16), wri_ref[...]) + bri_ref[...]
    r = jax.nn.sigmoid(pre[:, :pw])
    gi = jax.nn.sigmoid(pre[:, pw:])
    lam = lam_ref[...]
    softplus_neg = jnp.maximum(-lam, 0.0) + jnp.log1p(jnp.exp(-jnp.abs(lam)))
    log_a = -LRU_C * r * softplus_neg
    a = jnp.exp(log_a)
    b = jnp.sqrt(1.0 - a * a) * gi * xc

    if bb == 1:
        rowi = lax.broadcasted_iota(jnp.int32, (rows, 1), 0)
        s = 1
        while s < rows:
            a_sh = pltpu.roll(a, s, 0)
            b_sh = pltpu.roll(b, s, 0)
            keep = rowi >= s
            b = jnp.where(keep, a * b_sh + b, b)
            a = jnp.where(keep, a * a_sh, a)
            s *= 2
        hs = b + a * h_ref[...]
        h_ref[...] = hs[rows - 1:rows, :]
    else:
        hprev = h_ref[...]
        parts = []
        for i in range(tt):
            hprev = a[i * bb:(i + 1) * bb] * hprev + b[i * bb:(i + 1) * bb]
            parts.append(hprev)
        hs = jnp.concatenate(parts, axis=0)
        h_ref[...] = hprev
    y_b = hs * jax.nn.gelu(u_g)

    mix = _dot(y_a.astype(BF16), wout_ref[0:pw, :]) + _dot(y_b.astype(BF16), wout_ref[pw:2 * pw, :])
    x1_ref[0] = x + per_row(gt_ref[0]) * mix

    @pl.when(t == n_t - 1)
    def _():
        pooln_ref[0] = zp_ref[p0 + rows - (POOL_HALO - 1) * bb:p0 + rows, :]
        convn_ref[0] = zc_ref[c0 + rows - (CONV_WIDTH - 1) * bb:c0 + rows, :]
        lrun_ref[0] = h_ref[...]

    if n_t > 1:
        @pl.when(t < n_t - 1)
        def _():
            zp_ref[bb:p0, :] = zp_ref[rows + bb:rows + p0, :]
            zc_ref[c0 - (CONV_WIDTH - 1) * bb:c0, :] = zc_ref[c0 + rows - (CONV_WIDTH - 1) * bb:c0 + rows, :]


def _mix0(x, sh, sc, gt, pool0, conv0, lru0, wts, *, tt, bb, start):
    nbb, tot, d = x.shape
    n_t = tot // (tt * bb)
    rows = tt * bb
    pw = pool0.shape[-1]
    hp = (POOL_HALO - 1) * bb
    hc = (CONV_WIDTH - 1) * bb

    def const(a):
        nd = a.ndim
        return pl.BlockSpec(a.shape, lambda i, j: (0,) * nd)

    def per_b(n_rows, width):
        return pl.BlockSpec((1, n_rows, width), lambda i, j: (i, 0, 0))

    kern = functools.partial(_mix0_kernel, tt=tt, bb=bb, start=start, n_t=n_t)
    return pl.pallas_call(
        kern,
        grid=(nbb, n_t),
        in_specs=[pl.BlockSpec((1, rows, d), lambda i, j: (i, j, 0)),
                  per_b(bb, d), per_b(bb, d), per_b(bb, d),
                  per_b(hp, pw), per_b(hc, pw), per_b(bb, pw)] + [const(w) for w in wts],
        out_specs=[pl.BlockSpec((1, rows, d), lambda i, j: (i, j, 0)),
                   per_b(hp, pw), per_b(hc, pw), per_b(bb, pw)],
        out_shape=[jax.ShapeDtypeStruct((nbb, tot, d), F32),
                   jax.ShapeDtypeStruct((nbb, hp, pw), F32),
                   jax.ShapeDtypeStruct((nbb, hc, pw), F32),
                   jax.ShapeDtypeStruct((nbb, bb, pw), F32)],
        scratch_shapes=[pltpu.VMEM(((POOL_HALO + tt) * bb, pw), F32),
                        pltpu.VMEM(((CONV_HALO + tt) * bb, pw), F32),
                        pltpu.VMEM((bb, pw), F32)],
        compiler_params=_cparams(2),
        name="mix0",
    )(x, sh, sc, gt, pool0, conv0, lru0, *wts)


def _route_kernel(*refs, with_proj):
    if with_proj:
        (x_ref, o_ref, wout_ref, gtm_ref, shf_ref, scf_ref, gain_ref, wrh_ref, wrl_ref, br_ref,
         x1_ref, hf_ref, info_ref, cnt_ref, run_ref) = refs
    else:
        (x_ref, shf_ref, scf_ref, gain_ref, wrh_ref, wrl_ref, br_ref,
         hf_ref, info_ref, cnt_ref, run_ref) = refs

    @pl.when(pl.program_id(0) == 0)
    def _():
        run_ref[...] = jnp.zeros(run_ref.shape, F32)

    x = x_ref[...]
    if with_proj:
        x = x + gtm_ref[0] * _dot(o_ref[...], wout_ref[...])
        x1_ref[...] = x
    hf = _rms(x, gain_ref[...]) * (1.0 + scf_ref[0]) + shf_ref[0]
    hf_ref[...] = hf

    hi = hf.astype(BF16)
    lo = (hf - hi.astype(F32)).astype(BF16)
    logits = _dot(hi, wrh_ref[...]) + _dot(lo, wrh_ref[...]) + _dot(hi, wrl_ref[...]) + br_ref[...]

    tq = logits.shape[0]
    lane = lax.broadcasted_iota(jnp.int32, (tq, LANES), 1).astype(F32)
    neg = -jnp.inf
    big = float(LANES)
    is_g = lane < MOE_GROUPS
    gl = jnp.where(is_g, logits, neg)
    mg = jnp.max(gl, axis=1, keepdims=True)
    gidx = jnp.min(jnp.where(gl == mg, lane, big), axis=1, keepdims=True)
    p_group = 1.0 / jnp.sum(jnp.where(is_g, jnp.exp(gl - mg), 0.0), axis=1, keepdims=True)
    first = MOE_GROUPS + gidx * EXPERTS_PER_GROUP
    el = jnp.where((lane >= first) & (lane < first + EXPERTS_PER_GROUP), logits, neg)
    v1 = jnp.max(el, axis=1, keepdims=True)
    i1 = jnp.min(jnp.where(el == v1, lane, big), axis=1, keepdims=True)
    el2 = jnp.where(lane == i1, neg, el)
    v2 = jnp.max(el2, axis=1, keepdims=True)
    i2 = jnp.min(jnp.where(el2 == v2, lane, big), axis=1, keepdims=True)
    ex = jnp.exp(v2 - v1)
    g1 = p_group / (1.0 + ex)
    g2 = p_group * ex / (1.0 + ex)
    e1 = i1 - MOE_GROUPS
    e2 = i2 - MOE_GROUPS

    oh1 = jnp.where(lane == e1, 1.0, 0.0)
    oh2 = jnp.where(lane == e2, 1.0, 0.0)
    oh = oh1 + oh2
    ri = lax.broadcasted_iota(jnp.int32, (tq, tq), 0)
    ci = lax.broadcasted_iota(jnp.int32, (tq, tq), 1)
    lower = jnp.where(ri > ci, 1.0, 0.0).astype(BF16)
    before = _dot(lower, oh.astype(BF16)) + run_ref[...]
    r1 = jnp.sum(before * oh1, axis=1, keepdims=True)
    r2 = jnp.sum(before * oh2, axis=1, keepdims=True)
    run_ref[...] = run_ref[...] + jnp.sum(oh, axis=0, keepdims=True)
    cnt_ref[...] = run_ref[...]

    info = jnp.where(lane == 0, e1, jnp.where(lane == 1, e2, jnp.where(lane == 2, g1, jnp.where(
        lane == 3, g2, jnp.where(lane == 4, r1, jnp.where(lane == 5, r2, 0.0))))))
    info_ref[...] = info


def _route(x, shf, scf, gain, wrh, wrl, br, *, tile, rows_per_mod, proj=None):
    n, d = x.shape
    nt = n // tile
    mrows = shf.shape[1]
    mod_spec = pl.BlockSpec((1, mrows, d), lambda i: ((i * tile) // rows_per_mod, 0, 0))
    row_spec = pl.BlockSpec((tile, d), lambda i: (i, 0))

    def const(a):
        nd = a.ndim
        return pl.BlockSpec(a.shape, lambda i: (0,) * nd)

    info_spec = pl.BlockSpec((tile, LANES), lambda i: (i, 0))
    cnt_spec = pl.BlockSpec((1, LANES), lambda i: (0, 0))
    outs_shape = [jax.ShapeDtypeStruct((n, d), F32), jax.ShapeDtypeStruct((n, LANES), F32),
                  jax.ShapeDtypeStruct((1, LANES), F32)]
    outs_spec = [row_spec, info_spec, cnt_spec]
    if proj is None:
        ins = [x, shf, scf, gain, wrh, wrl, br]
        in_specs = [row_spec, mod_spec, mod_spec, const(gain), const(wrh), const(wrl), const(br)]
    else:
        o, wout, gtm = proj
        ins = [x, o, wout, gtm, shf, scf, gain, wrh, wrl, br]
        in_specs = [row_spec, pl.BlockSpec((tile, o.shape[1]), lambda i: (i, 0)), const(wout), mod_spec,
                    mod_spec, mod_spec, const(gain), const(wrh), const(wrl), const(br)]
        outs_shape = [jax.ShapeDtypeStruct((n, d), F32)] + outs_shape
        outs_spec = [row_spec] + outs_spec
    return pl.pallas_call(
        functools.partial(_route_kernel, with_proj=proj is not None),
        grid=(nt,),
        in_specs=in_specs,
        out_specs=outs_spec,
        out_shape=outs_shape,
        scratch_shapes=[pltpu.VMEM((1, LANES), F32)],
        compiler_params=_cparams(1),
        name="route",
    )(*ins)


def _dispatch_kernel(slot_ref, hf_ref, xs_ref, sem, *, tile):
    def issue(g, c):
        for u in range(ROW_DMA_UNROLL):
            r = g * ROW_DMA_UNROLL + u
            for k in range(TOP_K):
                s = slot_ref[0, 0, TOP_K * r + k]
                pltpu.make_async_copy(hf_ref.at[pl.ds(r, 1), :], xs_ref.at[pl.ds(s, 1), :], sem).start()
        return c

    lax.fori_loop(0, tile // ROW_DMA_UNROLL, issue, 0)
    for k in range(TOP_K):
        pltpu.make_async_copy(hf_ref, xs_ref.at[pl.ds(0, tile), :], sem).wait()


def _dispatch(hf, slots, *, tile):
    n, d = hf.shape
    nt = n // tile
    return pl.pallas_call(
        functools.partial(_dispatch_kernel, tile=tile),
        grid=(nt,),
        in_specs=[pl.BlockSpec((1, 1, TOP_K * tile), lambda i: (i, 0, 0), memory_space=pltpu.SMEM),
                  pl.BlockSpec((tile, d), lambda i: (i, 0))],
        out_specs=pl.BlockSpec(memory_space=pl.ANY),
        out_shape=jax.ShapeDtypeStruct((TOP_K * n, d), F32),
        scratch_shapes=[pltpu.SemaphoreType.DMA(())],
        compiler_params=_cparams(1),
        name="moe_dispatch",
    )(slots.reshape(nt, 1, TOP_K * tile), hf)


def _gmm_kernel(wt_ref, we_ref, wlo_ref, whi_ref, xs_ref, wg_ref, wu_ref, wd_ref, y_ref,
                wgb_ref, wub_ref, wdb_ref):
    w = pl.program_id(0)
    prev = jnp.maximum(w - 1, 0)
    new_expert = (w == 0) | (we_ref[w] != we_ref[prev])
    new_tile = (w == 0) | (wt_ref[w] != wt_ref[prev])

    @pl.when(new_expert)
    def _():
        wgb_ref[...] = wg_ref[0, 0].astype(BF16)
        wub_ref[...] = wu_ref[0, 0].astype(BF16)
        wdb_ref[...] = wd_ref[0, 0].astype(BF16)

    @pl.when(new_tile)
    def _():
        y_ref[...] = jnp.zeros(y_ref.shape, F32)

    lo = wlo_ref[w]
    hi = whi_ref[w]

    @pl.when(hi > lo)
    def _():
        x = xs_ref[...].astype(BF16)
        g = _dot(x, wgb_ref[...])
        u = _dot(x, wub_ref[...])
        yv = _dot((_silu(g) * u).astype(BF16), wdb_ref[...])
        row = lax.broadcasted_iota(jnp.int32, (x.shape[0], 1), 0)
        y_ref[...] = y_ref[...] + jnp.where((row >= lo) & (row < hi), yv, 0.0)


def _gmm(xs, work, w_gate, w_up, w_down, *, tile, layer):
    m, d = xs.shape
    ff = w_gate.shape[-1]
    n_work = work[0].shape[0]
    grid_spec = pltpu.PrefetchScalarGridSpec(
        num_scalar_prefetch=4,
        grid=(n_work,),
        in_specs=[pl.BlockSpec((tile, d), lambda w, wt, we, wlo, whi: (wt[w], 0)),
                  pl.BlockSpec((1, 1, d, ff), lambda w, wt, we, wlo, whi: (layer, we[w], 0, 0)),
                  pl.BlockSpec((1, 1, d, ff), lambda w, wt, we, wlo, whi: (layer, we[w], 0, 0)),
                  pl.BlockSpec((1, 1, ff, d), lambda w, wt, we, wlo, whi: (layer, we[w], 0, 0))],
        out_specs=pl.BlockSpec((tile, d), lambda w, wt, we, wlo, whi: (wt[w], 0)),
        scratch_shapes=[pltpu.VMEM((d, ff), BF16), pltpu.VMEM((d, ff), BF16), pltpu.VMEM((ff, d), BF16)],
    )
    return pl.pallas_call(
        _gmm_kernel,
        grid_spec=grid_spec,
        out_shape=jax.ShapeDtypeStruct((m, d), F32),
        compiler_params=_cparams(1),
        name="moe_gmm",
    )(*work, xs, w_gate, w_up, w_down)


def _work_list(counts, n_slots, tile):
    n_tiles = n_slots // tile
    n_work = n_tiles + N_EXPERTS - 1
    ends = jnp.cumsum(counts)
    starts = ends - counts
    first_tile = starts // tile
    last_tile = jnp.maximum(ends - 1, 0) // tile
    n_items = jnp.where(counts > 0, last_tile - first_tile + 1, 0)
    item_end = jnp.cumsum(n_items)
    item_start = item_end - n_items
    w = jnp.arange(n_work, dtype=jnp.int32)
    used = w < item_end[-1]
    e = jnp.minimum(jnp.sum(w[:, None] >= item_end[None, :], axis=1), N_EXPERTS - 1).astype(jnp.int32)
    t = first_tile[e] + (w - item_start[e])
    lo = jnp.maximum(starts[e], t * tile) - t * tile
    hi = jnp.minimum(ends[e], (t + 1) * tile) - t * tile
    last_e = jnp.max(jnp.where(counts > 0, jnp.arange(N_EXPERTS), 0)).astype(jnp.int32)
    wt = jnp.where(used, t, n_tiles - 1).astype(jnp.int32)
    we = jnp.where(used, e, last_e).astype(jnp.int32)
    wlo = jnp.where(used, lo, 0).astype(jnp.int32)
    whi = jnp.where(used, hi, 0).astype(jnp.int32)
    return (wt, we, wlo, whi), starts


def _combine_kernel(*refs, tile, final):
    if final:
        slot_ref, x_ref, gt_ref, info_ref, fg_ref, y_ref, out_ref, ybuf, sem = refs
    else:
        slot_ref, x_ref, gt_ref, info_ref, y_ref, out_ref, ybuf, sem = refs

    def issue(g, c):
        for u in range(ROW_DMA_UNROLL):
            r = g * ROW_DMA_UNROLL + u
            for k in range(TOP_K):
                s = slot_ref[0, 0, TOP_K * r + k]
                pltpu.make_async_copy(y_ref.at[pl.ds(s, 1), :], ybuf.at[k, pl.ds(r, 1), :], sem).start()
        return c

    lax.fori_loop(0, tile // ROW_DMA_UNROLL, issue, 0)
    for k in range(TOP_K):
        pltpu.make_async_copy(y_ref.at[pl.ds(0, tile), :], ybuf.at[k], sem).wait()

    info = info_ref[...]
    ffn = info[:, 2:3] * ybuf[0] + info[:, 3:4] * ybuf[1]
    out = x_ref[...] + gt_ref[0] * ffn
    if final:
        out = _rms(out, fg_ref[...])
    out_ref[...] = out


def _combine(x, gt, info, y, slots, *, tile, rows_per_mod, final_gain=None):
    n, d = x.shape
    nt = n // tile
    mrows = gt.shape[1]
    final = final_gain is not None
    in_specs = [pl.BlockSpec((1, 1, TOP_K * tile), lambda i: (i, 0, 0), memory_space=pltpu.SMEM),
                pl.BlockSpec((tile, d), lambda i: (i, 0)),
                pl.BlockSpec((1, mrows, d), lambda i: ((i * tile) // rows_per_mod, 0, 0)),
                pl.BlockSpec((tile, LANES), lambda i: (i, 0))]
    ins = [slots.reshape(nt, 1, TOP_K * tile), x, gt, info]
    if final:
        in_specs.append(pl.BlockSpec(final_gain.shape, lambda i: (0, 0)))
        ins.append(final_gain)
    in_specs.append(pl.BlockSpec(memory_space=pl.ANY))
    ins.append(y)
    return pl.pallas_call(
        functools.partial(_combine_kernel, tile=tile, final=final),
        grid=(nt,),
        in_specs=in_specs,
        out_specs=pl.BlockSpec((tile, d), lambda i: (i, 0)),
        out_shape=jax.ShapeDtypeStruct((n, d), F32),
        scratch_shapes=[pltpu.VMEM((TOP_K, tile, d), F32), pltpu.SemaphoreType.DMA(())],
        compiler_params=_cparams(1),
        name="moe_combine",
    )(*ins)


def _moe(x, mods, rw, experts, *, layer, tile, gmm_tile, rows_per_mod, proj=None, final_gain=None):
    shf, scf, gtf = mods
    gain, wrh, wrl, br = rw
    res = _route(x, shf, scf, gain, wrh, wrl, br, tile=tile, rows_per_mod=rows_per_mod, proj=proj)
    if proj is not None:
        x, hf, info, cnt = res
    else:
        hf, info, cnt = res
    n = x.shape[0]
    counts = cnt[0, :N_EXPERTS].astype(jnp.int32)
    work, starts = _work_list(counts, TOP_K * n, gmm_tile)
    eid = info[:, 0:TOP_K].astype(jnp.int32)
    rank = info[:, 4:4 + TOP_K].astype(jnp.int32)
    is_e = eid[:, :, None] == jnp.arange(N_EXPERTS, dtype=jnp.int32)
    slots = (jnp.sum(jnp.where(is_e, starts.astype(jnp.int32), 0), axis=-1) + rank).reshape(-1)
    xs = _dispatch(hf, slots, tile=tile)
    y = _gmm(xs, work, *experts, tile=gmm_tile, layer=layer)
    return _combine(x, gtf, info, y, slots, tile=tile, rows_per_mod=rows_per_mod, final_gain=final_gain)


def _rope_turn(blk, c, s):
    return blk * c + pltpu.roll(blk, LANES - QK_ROPE, 1) * s


def _mla_proj_kernel(*refs, sample):
    if sample:
        (x_ref, sh_ref, sc_ref, gain_ref, win_ref, qg_ref, wuq_ref, kvg_ref, c_ref, s_ref,
         ckv_ref, kr_ref, q_ref) = refs
    else:
        (x_ref, sh_ref, sc_ref, gain_ref, win_ref, qg_ref, wuq_ref, kvg_ref, c_ref, s_ref, wuk_ref, wuvt_ref,
         vonet_ref, ct_ref, st_ref, ckv_ref, kr_ref, q_ref, k_ref, v_ref) = refs
    q_lora = qg_ref.shape[1]
    kv_lora = kvg_ref.shape[1]
    h = _rms(x_ref[...], gain_ref[...]) * (1.0 + sc_ref[0]) + sh_ref[0]
    proj = _dot(h.astype(BF16), win_ref[...])
    qn = _rms(proj[:, :q_lora], qg_ref[...])
    ckv = _rms(proj[:, q_lora:q_lora + kv_lora], kvg_ref[...])
    c = c_ref[...]
    s = s_ref[...]
    kf = _rope_turn(proj[:, q_lora + kv_lora:], c, s)
    ckv_ref[...] = ckv
    kr_ref[...] = kf[:, QK_NOPE:QK_NOPE + QK_ROPE]
    qn_b = qn.astype(BF16)
    if sample:
        q = _dot(qn_b, wuq_ref[...])
        for hh in range(MLA_HEADS):
            cols = slice(hh * HEAD_PAD, (hh + 1) * HEAD_PAD)
            q_ref[:, cols] = _rope_turn(q[:, cols], c, s) * SM_SCALE
    else:
        ckv_b = ckv.astype(BF16)
        kn = _dot(ckv_b, wuk_ref[...])
        qt = _dot_nt(wuq_ref[...], qn_b)
        v_ref[0] = (_dot_nt(wuvt_ref[...], ckv_b) + vonet_ref[...]).astype(BF16)
        ct = ct_ref[...]
        st = st_ref[...]
        for hh in range(MLA_HEADS):
            cols = slice(hh * HEAD_PAD, (hh + 1) * HEAD_PAD)
            k_ref[:, cols] = (kn[:, cols] + kf).astype(BF16)
            blk = qt[cols, :]
            turned = jnp.concatenate([blk[QK_ROPE:], blk[:QK_ROPE]], axis=0)
            q_ref[0, cols, :] = ((blk * ct + turned * st) * (SM_SCALE * LOG2E)).astype(BF16)


def _mla_proj(x, sh, sc, gain, win, qg, wuq, kvg, ctab, stab, *, tile, rows_per_mod, tab_tiles, kv_w=None):
    n, d = x.shape
    nt = n // tile
    mrows = sh.shape[1]
    sample = kv_w is None
    kv_lora = kvg.shape[1]

    def const(a):
        nd = a.ndim
        return pl.BlockSpec(a.shape, lambda i: (0,) * nd)

    row = lambda width: pl.BlockSpec((tile, width), lambda i: (i, 0))
    mod_spec = pl.BlockSpec((1, mrows, d), lambda i: ((i * tile) // rows_per_mod, 0, 0))
    tab_spec = pl.BlockSpec((tile, LANES), lambda i: (i % tab_tiles, 0))
    ins = [x, sh, sc, gain, win, qg, wuq, kvg, ctab, stab]
    in_specs = [row(d), mod_spec, mod_spec, const(gain), const(win), const(qg), const(wuq), const(kvg),
                tab_spec, tab_spec]
    qw = MLA_HEADS * HEAD_PAD
    out_shape = [jax.ShapeDtypeStruct((n, kv_lora), F32), jax.ShapeDtypeStruct((n, QK_ROPE), F32)]
    out_specs = [row(kv_lora), row(QK_ROPE)]
    if sample:
        out_shape.append(jax.ShapeDtypeStruct((n, qw), F32))
        out_specs.append(row(qw))
    else:
        wuk, wuvt, vonet, ctab_t, stab_t = kv_w
        seq_len = tab_tiles * tile
        tab_t_spec = pl.BlockSpec((LANES, tile), lambda i: (0, i % tab_tiles))
        by_seq = pl.BlockSpec((1, qw, tile), lambda i: (i // tab_tiles, 0, i % tab_tiles))
        ins += [wuk, wuvt, vonet, ctab_t, stab_t]
        in_specs += [const(wuk), const(wuvt), const(vonet), tab_t_spec, tab_t_spec]
        out_shape += [jax.ShapeDtypeStruct((n // seq_len, qw, seq_len), BF16), jax.ShapeDtypeStruct((n, qw), BF16),
                      jax.ShapeDtypeStruct((n // seq_len, qw, seq_len), BF16)]
        out_specs += [by_seq, row(qw), by_seq]
    return pl.pallas_call(
        functools.partial(_mla_proj_kernel, sample=sample),
        grid=(nt,),
        in_specs=in_specs,
        out_specs=out_specs,
        out_shape=out_shape,
        compiler_params=_cparams(1),
        name="mla_proj",
    )(*ins)


def _attn_kernel(q_ref, k_ref, v_ref, o_ref, sa_ref, sb_ref, m_ref, acc_ref, *, tq, tk):
    i = pl.program_id(2)
    heads = range(2)

    def cols(hh):
        return slice(hh * HEAD_PAD, (hh + 1) * HEAD_PAD)

    def scores(j, dst_ref):
        keys = pl.ds(pl.multiple_of(j * tk, tk), tk)
        for hh in heads:
            dst_ref[hh] = _dot(k_ref[0, keys, cols(hh)], q_ref[0, cols(hh), :])

    def absorb(j, src_ref, masked):
        keys = pl.ds(pl.multiple_of(j * tk, tk), tk)
        if masked:
            keep = lax.broadcasted_iota(jnp.int32, (tk, tq), 0) <= lax.broadcasted_iota(jnp.int32, (tk, tq), 1)
        for hh in heads:
            s = src_ref[hh]
            if masked:
                s = jnp.where(keep, s, -jnp.inf)
            m = m_ref[hh]
            m_new = jnp.maximum(m, jnp.max(s, axis=0, keepdims=True))
            p = jnp.exp2(s - m_new)
            acc_ref[hh] = jnp.exp2(m - m_new) * acc_ref[hh] + _dot(v_ref[0, cols(hh), keys], p.astype(BF16))
            m_ref[hh] = m_new

    m_ref[...] = jnp.full(m_ref.shape, -jnp.inf, F32)
    acc_ref[...] = jnp.zeros(acc_ref.shape, F32)
    scores(0, sa_ref)

    def pair(jj, c):
        j = 2 * jj
        scores(j + 1, sb_ref)
        absorb(j, sa_ref, False)
        scores(j + 2, sa_ref)
        absorb(j + 1, sb_ref, False)
        return c

    lax.fori_loop(0, i // 2, pair, 0)

    @pl.when(i % 2 == 1)
    def _():
        scores(i, sb_ref)
        absorb(i - 1, sa_ref, False)
        absorb(i, sb_ref, True)

    @pl.when(i % 2 == 0)
    def _():
        absorb(i, sa_ref, True)

    out_t = jnp.concatenate([acc_ref[hh][:V_DIM] / acc_ref[hh][V_DIM:V_DIM + 1] for hh in heads], axis=0)
    o_ref[0] = out_t.T.astype(o_ref.dtype)


def _attention(q_t, k, v_t, *, tq, tk):
    assert tq == tk, "one diagonal key tile per query tile"
    b, s, _ = k.shape
    pairs = MLA_HEADS // 2
    return pl.pallas_call(
        functools.partial(_attn_kernel, tq=tq, tk=tk),
        grid=(b, pairs, s // tq),
        in_specs=[pl.BlockSpec((1, 2 * HEAD_PAD, tq), lambda bi, p, i: (bi, p, i)),
                  pl.BlockSpec((1, s, 2 * HEAD_PAD), lambda bi, p, i: (bi, 0, p)),
                  pl.BlockSpec((1, 2 * HEAD_PAD, s), lambda bi, p, i: (bi, p, 0))],
        out_specs=pl.BlockSpec((1, tq, 2 * V_DIM), lambda bi, p, i: (bi, i, p)),
        out_shape=jax.ShapeDtypeStruct((b, s, MLA_HEADS * V_DIM), BF16),
        scratch_shapes=[pltpu.VMEM((2, tk, tq), F32), pltpu.VMEM((2, tk, tq), F32),
                        pltpu.VMEM((2, 1, tq), F32), pltpu.VMEM((2, HEAD_PAD, tq), F32)],
        compiler_params=_cparams(3),
        name="prompt_attn",
    )(q_t, k, v_t)


def _qabs_kernel(q_ref, m_ref, o_ref):
    o_ref[0] = _dot(q_ref[...].astype(BF16), m_ref[0])


def _qabs(q, mats):
    n = q.shape[0]
    width = mats.shape[-1]
    return pl.pallas_call(
        _qabs_kernel,
        grid=(MLA_HEADS,),
        in_specs=[pl.BlockSpec((n, HEAD_PAD), lambda h: (0, h)),
                  pl.BlockSpec((1, HEAD_PAD, width), lambda h: (h, 0, 0))],
        out_specs=pl.BlockSpec((1, n, width), lambda h: (h, 0, 0)),
        out_shape=jax.ShapeDtypeStruct((MLA_HEADS, n, width), F32),
        compiler_params=_cparams(1),
        name="sample_qabs",
    )(q, mats)


def _paged_attn_kernel(pt_ref, q_ref, cn_ref, rn_ref, ck_hbm, kr_hbm, o_ref,
                       ckbuf, krbuf, m_ref, l_ref, acc_ref, sem, *, group, pages, page, n_chunks, t_new, t_pad):
    bg = pl.program_id(0)
    c = pl.program_id(1)
    step = bg * n_chunks + c
    n_steps = pl.num_programs(0) * n_chunks
    slot = step % PAGE_RING
    kv_lora = ckbuf.shape[-1]

    def start_chunk(st):
        bgi, ci, sl = st // n_chunks, st % n_chunks, st % PAGE_RING
        for g in range(group):
            for p in range(pages):
                phys = pt_ref[bgi * group + g, ci * pages + p]
                pltpu.make_async_copy(ck_hbm.at[phys], ckbuf.at[sl, g, pl.ds(p * page, page), :],
                                      sem.at[0, sl]).start()
                pltpu.make_async_copy(kr_hbm.at[phys], krbuf.at[sl, g, p], sem.at[1, sl]).start()

    @pl.when(step == 0)
    def _():
        for ahead in range(PAGE_RING - 1):
            @pl.when(ahead < n_steps)
            def _():
                start_chunk(step + ahead)

    pltpu.make_async_copy(ckbuf.at[slot], ckbuf.at[slot], sem.at[0, slot]).wait()
    pltpu.make_async_copy(krbuf.at[slot], krbuf.at[slot], sem.at[1, slot]).wait()

    @pl.when(c == 0)
    def _():
        m_ref[...] = jnp.full(m_ref.shape, -jnp.inf, F32)
        l_ref[...] = jnp.zeros(l_ref.shape, F32)
        acc_ref[...] = jnp.zeros(acc_ref.shape, F32)

    def absorb(g, s, values):
        m = m_ref[g]
        m_new = jnp.maximum(m, jnp.max(s, axis=1, keepdims=True))
        alpha = jnp.exp(m - m_new)
        p = jnp.exp(s - m_new)
        l_ref[g] = alpha * l_ref[g] + jnp.sum(p, axis=1, keepdims=True)
        acc_ref[g] = alpha * acc_ref[g] + _dot(p.astype(BF16), values)
        m_ref[g] = m_new

    cks, scores = [], []
    for g in range(group):
        qa = q_ref[g]
        q_lat = qa[:, :kv_lora].astype(BF16)
        q_rope = qa[:, kv_lora:kv_lora + QK_ROPE].astype(BF16)
        ck = ckbuf[slot, g].astype(BF16)
        kr_t = jnp.concatenate([krbuf[slot, g, p] for p in range(pages)], axis=1).astype(BF16)
        cks.append(ck)
        scores.append(_dot_nt(q_lat, ck) + _dot(q_rope, kr_t))

    nxt = step + PAGE_RING - 1

    @pl.when(nxt < n_steps)
    def _():
        start_chunk(nxt)

    for g in range(group):
        absorb(g, scores[g], cks[g])

    @pl.when(c == n_chunks - 1)
    def _():
        rows = q_ref.shape[1]
        t_q = lax.broadcasted_iota(jnp.int32, (rows, t_pad), 0) % t_new
        t_k = lax.broadcasted_iota(jnp.int32, (rows, t_pad), 1)
        for g in range(group):
            qa = q_ref[g]
            q_lat = qa[:, :kv_lora].astype(BF16)
            q_rope = qa[:, kv_lora:kv_lora + QK_ROPE].astype(BF16)
            cn = cn_ref[g].astype(BF16)
            s = _dot_nt(q_lat, cn) + _dot_nt(q_rope, rn_ref[g].astype(BF16))
            absorb(g, jnp.where(t_k <= t_q, s, -jnp.inf), cn)
            o_ref[g] = acc_ref[g] / l_ref[g]


def _paged_attn(page_table, qabs, ckv_new, kr_new, cache_ck, cache_kr_t, *, group, pages, t_new):
    b, rows, width = qabs.shape
    n_pages = page_table.shape[1]
    n_chunks = n_pages // pages
    page, kv_lora = cache_ck.shape[1:]
    t_pad = ckv_new.shape[1]
    grid_spec = pltpu.PrefetchScalarGridSpec(
        num_scalar_prefetch=1,
        grid=(b // group, n_chunks),
        in_specs=[pl.BlockSpec((group, rows, width), lambda bi, ci, pt: (bi, 0, 0)),
                  pl.BlockSpec((group, t_pad, kv_lora), lambda bi, ci, pt: (bi, 0, 0)),
                  pl.BlockSpec((group, t_pad, QK_ROPE), lambda bi, ci, pt: (bi, 0, 0)),
                  pl.BlockSpec(memory_space=pl.ANY),
                  pl.BlockSpec(memory_space=pl.ANY)],
        out_specs=pl.BlockSpec((group, rows, kv_lora), lambda bi, ci, pt: (bi, 0, 0)),
        scratch_shapes=[pltpu.VMEM((PAGE_RING, group, pages * page, kv_lora), F32),
                        pltpu.VMEM((PAGE_RING, group, pages, QK_ROPE, page), F32),
                        pltpu.VMEM((group, rows, 1), F32), pltpu.VMEM((group, rows, 1), F32),
                        pltpu.VMEM((group, rows, kv_lora), F32),
                        pltpu.SemaphoreType.DMA((2, PAGE_RING))],
    )
    return pl.pallas_call(
        functools.partial(_paged_attn_kernel, group=group, pages=pages, page=page, n_chunks=n_chunks, t_new=t_new,
                          t_pad=t_pad),
        grid_spec=grid_spec,
        out_shape=jax.ShapeDtypeStruct((b, rows, kv_lora), F32),
        compiler_params=_cparams(2),
        name="paged_attn",
    )(page_table, qabs, ckv_new, kr_new, cache_ck, cache_kr_t)


def _vup_kernel(o_ref, w_ref, out_ref):
    lat = jnp.concatenate([o_ref[0], o_ref[1]], axis=-1).astype(BF16)
    out_ref[...] = _dot(lat, w_ref[0]).astype(out_ref.dtype)


def _vup(o_lat, w_pairs):
    h, n, c = o_lat.shape
    return pl.pallas_call(
        _vup_kernel,
        grid=(h // 2,),
        in_specs=[pl.BlockSpec((2, n, c), lambda p: (p, 0, 0)),
                  pl.BlockSpec((1, 2 * c, 2 * V_DIM), lambda p: (p, 0, 0))],
        out_specs=pl.BlockSpec((n, 2 * V_DIM), lambda p: (0, p)),
        out_shape=jax.ShapeDtypeStruct((n, h * V_DIM), BF16),
        compiler_params=_cparams(1),
        name="sample_vup",
    )(o_lat, w_pairs)


def _block_diag(w):
    g, a, b = w.shape
    out = jnp.zeros((g * a, g * b), w.dtype)
    for i in range(g):
        out = out.at[i * a:(i + 1) * a, i * b:(i + 1) * b].set(w[i])
    return out


def _swap_halves(w):
    half = w.shape[-1] // 2
    return jnp.concatenate([w[..., half:], w[..., :half]], axis=-1)


def _rope_tables(pos):
    half = QK_ROPE // 2
    inv_freq = ROPE_THETA ** (-jnp.arange(half, dtype=F32) / half)
    ang = pos.astype(F32)[:, None] * inv_freq[None, :]
    cos, sin = jnp.cos(ang), jnp.sin(ang)
    n = pos.shape[0]
    ctab = jnp.concatenate([jnp.ones((n, QK_NOPE), F32), cos, cos, jnp.zeros((n, QK_ROPE), F32)], axis=1)
    stab = jnp.concatenate([jnp.zeros((n, QK_NOPE), F32), -sin, sin, jnp.zeros((n, QK_ROPE), F32)], axis=1)
    return ctab, stab


def _router_weights(w_group, b_group, w_expert, b_expert):
    d = w_group.shape[0]
    we = jnp.transpose(w_expert, (1, 0, 2)).reshape(d, N_EXPERTS)
    w = jnp.concatenate([w_group, we, jnp.zeros((d, LANES - MOE_GROUPS - N_EXPERTS), F32)], axis=1)
    bias = jnp.concatenate([b_group, b_expert.reshape(-1), jnp.zeros((LANES - MOE_GROUPS - N_EXPERTS,), F32)])
    hi = w.astype(BF16)
    lo = (w - hi.astype(F32)).astype(BF16)
    return hi, lo, bias.reshape(1, LANES)


def _pick(n, pref):
    t = min(n, pref)
    while n % t:
        t //= 2
    return t


def kernel(x_prompt, x_sample, state_pool, state_conv, state_lru, cache_kv_latent, cache_k_rope, page_table,
           c_prompt, c_sample, ada_w, ada_b, norm_mix, norm_ffn, ab_w_in, ab_w_out, pool_w, pool_scale,
           conv_w, conv_b, lru_w_r, lru_b_r, lru_w_i, lru_b_i, lru_lambda,
           mla_w_in, mla_q_norm, mla_w_uq, mla_kv_norm, mla_w_uk, mla_w_uv, mla_w_out,
           router_w_group, router_b_group, router_w_expert, router_b_expert,
           moe_w_gate, moe_w_up, moe_w_down, final_norm):
    bp, seq, d = x_prompt.shape
    bs, t_new, _ = x_sample.shape
    depth = ada_w.shape[0]
    pw = pool_scale.shape[-1]
    page = cache_kv_latent.shape[2]
    past_len = page_table.shape[1] * page
    q_lora = mla_q_norm.shape[-1]
    kv_lora = mla_kv_norm.shape[-1]

    n_c = bp + bs
    n_c_pad = -(-n_c // 8) * 8
    c_all = jnp.concatenate([c_prompt, c_sample, jnp.zeros((n_c_pad - n_c, d), F32)], axis=0)
    mod = _ada(c_all, ada_w, ada_b)

    def mods(layer, lo, hi):
        return [mod[layer, lo:hi, k * d:(k + 1) * d] for k in range(6)]

    n_p = bp * seq
    n_s = bs * t_new
    tile_p = _pick(seq, 512)
    tile_s = _pick(n_s, 512)
    gmm_tile_p = _pick(TOP_K * n_p, 256)
    gmm_tile_s = _pick(TOP_K * n_s, 256)

    xp = x_prompt.reshape(n_p, d)
    xs = x_sample.reshape(n_s, d)
    outs = {}

    for layer in range(depth):
        sh_m, sc_m, gt_m, sh_f, sc_f, gt_f = mods(layer, 0, bp)
        sh_ms, sc_ms, gt_ms, sh_fs, sc_fs, gt_fs = mods(layer, bp, bp + bs)
        rep = lambda v: jnp.repeat(v, t_new, axis=0)[None]
        per_seq = lambda v: v[:, None, :]
        rw = (norm_ffn[layer].reshape(1, d),) + _router_weights(
            router_w_group[layer], router_b_group[layer], router_w_expert[layer], router_b_expert[layer])
        experts = (moe_w_gate, moe_w_up, moe_w_down)
        final_gain = final_norm.reshape(1, d) if layer == depth - 1 else None
        ffn_mods_p = (per_seq(sh_f), per_seq(sc_f), per_seq(gt_f))
        ffn_mods_s = (rep(sh_fs), rep(sc_fs), rep(gt_fs))

        if layer % 2 == 0:
            e = layer // 2
            wts = (norm_mix[layer].reshape(1, d), ab_w_in[e].astype(BF16), _block_diag(pool_w[e]).astype(BF16),
                   pool_scale[e].reshape(1, pw), conv_w[e], conv_b[e].reshape(1, pw),
                   jnp.concatenate([_block_diag(lru_w_r[e]), _block_diag(lru_w_i[e])], axis=1).astype(BF16),
                   jnp.concatenate([lru_b_r[e], lru_b_i[e]]).reshape(1, 2 * pw),
                   lru_lambda[e].reshape(1, pw), ab_w_out[e].astype(BF16))
            x1p, pool_p, conv_p, lru_p = _mix0(
                xp.reshape(bp, seq, d), per_seq(sh_m), per_seq(sc_m), per_seq(gt_m),
                jnp.zeros((bp, POOL_HALO - 1, pw), F32), jnp.zeros((bp, CONV_WIDTH - 1, pw), F32),
                jnp.zeros((bp, 1, pw), F32), wts, tt=tile_p, bb=1, start=0)
            tm = lambda a: jnp.swapaxes(a, 0, 1).reshape(1, -1, a.shape[-1])
            x1s, pool_s, conv_s, lru_s = _mix0(
                tm(xs.reshape(bs, t_new, d)), sh_ms[None], sc_ms[None], gt_ms[None],
                tm(state_pool[e]), tm(state_conv[e]), state_lru[e][None], wts, tt=t_new, bb=bs, start=past_len)
            bm = lambda a, n: jnp.swapaxes(a.reshape(n, bs, a.shape[-1]), 0, 1)
            outs.setdefault("pool_p", []).append(pool_p)
            outs.setdefault("pool_s", []).append(bm(pool_s, POOL_HALO - 1))
            outs.setdefault("conv_p", []).append(conv_p)
            outs.setdefault("conv_s", []).append(bm(conv_s, CONV_WIDTH - 1))
            outs.setdefault("lru_p", []).append(lru_p.reshape(bp, pw))
            outs.setdefault("lru_s", []).append(lru_s.reshape(bs, pw))
            xp = x1p.reshape(n_p, d)
            xs = bm(x1s, t_new).reshape(n_s, d)
            proj_p = proj_s = None
        else:
            o = layer // 2
            gain = norm_mix[layer].reshape(1, d)
            w_in = mla_w_in[o]
            w_kr = w_in[:, q_lora + kv_lora:]
            win = jnp.concatenate([w_in[:, :q_lora + kv_lora], jnp.zeros((d, QK_NOPE), F32), w_kr,
                                   _swap_halves(w_kr)], axis=1).astype(BF16)
            wq = mla_w_uq[o].reshape(q_lora, MLA_HEADS, QK_NOPE + QK_ROPE)
            wuq = jnp.concatenate([wq, _swap_halves(wq[..., QK_NOPE:])], axis=-1).reshape(
                q_lora, MLA_HEADS * HEAD_PAD).astype(BF16)
            wuk = jnp.concatenate([mla_w_uk[o], jnp.zeros((kv_lora, MLA_HEADS, HEAD_PAD - QK_NOPE), F32)],
                                  axis=-1).reshape(kv_lora, MLA_HEADS * HEAD_PAD).astype(BF16)
            wuv_t = jnp.concatenate([mla_w_uv[o], jnp.zeros((kv_lora, MLA_HEADS, HEAD_PAD - V_DIM), F32)],
                                    axis=-1).reshape(kv_lora, MLA_HEADS * HEAD_PAD).T.astype(BF16)
            vone_t = jnp.tile((jnp.arange(HEAD_PAD) == V_DIM).astype(F32), MLA_HEADS).reshape(-1, 1)
            qg = mla_q_norm[o].reshape(1, q_lora)
            kvg = mla_kv_norm[o].reshape(1, kv_lora)
            wout = mla_w_out[o].astype(BF16)

            ctab, stab = _rope_tables(jnp.arange(seq, dtype=jnp.int32))
            ckv_p, kr_p, qt_p, k_p, vt_p = _mla_proj(
                xp, per_seq(sh_m), per_seq(sc_m), gain, win, qg, wuq.T, kvg, ctab, stab,
                tile=tile_p, rows_per_mod=seq, tab_tiles=seq // tile_p, kv_w=(wuk, wuv_t, vone_t, ctab.T, stab.T))
            qw = MLA_HEADS * HEAD_PAD
            o_p = _attention(qt_p, k_p.reshape(bp, seq, qw), vt_p, tq=tile_p, tk=tile_p)
            proj_p = (o_p.reshape(n_p, MLA_HEADS * V_DIM), wout, per_seq(gt_m))
            outs.setdefault("lat_p", []).append(ckv_p.reshape(bp, seq, kv_lora))
            outs.setdefault("rope_p", []).append(kr_p.reshape(bp, seq, QK_ROPE))

            pos_s = jnp.tile(past_len + jnp.arange(t_new, dtype=jnp.int32), bs)
            ctab_s, stab_s = _rope_tables(pos_s)
            ckv_s, kr_s, q_s = _mla_proj(
                xs, rep(sh_ms), rep(sc_ms), gain, win, qg, wuq, kvg, ctab_s, stab_s,
                tile=tile_s, rows_per_mod=n_s, tab_tiles=n_s // tile_s)
            aw = kv_lora + HEAD_PAD
            wk_t = jnp.transpose(mla_w_uk[o], (1, 2, 0))
            mats = jnp.zeros((MLA_HEADS, HEAD_PAD, aw), F32)
            mats = mats.at[:, :QK_NOPE, :kv_lora].set(wk_t)
            mats = mats.at[:, QK_NOPE:QK_NOPE + QK_ROPE, kv_lora:kv_lora + QK_ROPE].set(
                jnp.broadcast_to(jnp.eye(QK_ROPE, dtype=F32), (MLA_HEADS, QK_ROPE, QK_ROPE)))
            qabs = _qabs(q_s, mats.astype(BF16))
            qabs = jnp.transpose(qabs.reshape(MLA_HEADS, bs, t_new, aw), (1, 0, 2, 3)).reshape(
                bs, MLA_HEADS * t_new, aw)
            t_pad = 8
            pad_t = lambda a: jnp.pad(a.reshape(bs, t_new, -1), ((0, 0), (0, t_pad - t_new), (0, 0)))
            o_lat = _paged_attn(page_table, qabs, pad_t(ckv_s), pad_t(kr_s), cache_kv_latent[o],
                                jnp.swapaxes(cache_k_rope[o], 1, 2),
                                group=_pick(bs, 4), pages=_pick(page_table.shape[1], 8), t_new=t_new)
            o_lat = jnp.transpose(o_lat.reshape(bs, MLA_HEADS, t_new, kv_lora), (1, 0, 2, 3)).reshape(
                MLA_HEADS, n_s, kv_lora)
            wv = jnp.transpose(mla_w_uv[o], (1, 0, 2))
            w_pairs = jnp.zeros((MLA_HEADS // 2, 2 * kv_lora, 2 * V_DIM), F32)
            w_pairs = w_pairs.at[:, :kv_lora, :V_DIM].set(wv[0::2]).at[:, kv_lora:, V_DIM:].set(wv[1::2])
            o_s = _vup(o_lat, w_pairs.astype(BF16))
            proj_s = (o_s, wout, rep(gt_ms))
            outs.setdefault("lat_s", []).append(ckv_s.reshape(bs, t_new, kv_lora))
            outs.setdefault("rope_s", []).append(kr_s.reshape(bs, t_new, QK_ROPE))

        xp = _moe(xp, ffn_mods_p, rw, experts, layer=layer, tile=tile_p, gmm_tile=gmm_tile_p, rows_per_mod=seq,
                  proj=proj_p, final_gain=final_gain)
        xs = _moe(xs, ffn_mods_s, rw, experts, layer=layer, tile=tile_s, gmm_tile=gmm_tile_s, rows_per_mod=n_s,
                  proj=proj_s, final_gain=final_gain)

    st = lambda k: jnp.stack(outs[k])
    return (xp.reshape(bp, seq, d), xs.reshape(bs, t_new, d),
            st("pool_p"), st("pool_s"), st("conv_p"), st("conv_s"), st("lru_p"), st("lru_s"),
            st("lat_p"), st("lat_s"), st("rope_p"), st("rope_s"))
```

```python
import functools

import jax
import jax.numpy as jnp
from jax import lax
from jax.experimental import pallas as pl
from jax.experimental.pallas import tpu as pltpu

F32 = jnp.float32
BF16 = jnp.bfloat16

EPS = 1e-6
POOL_WINDOWS = (2, 4, 8, 16)
POOL_HALO = 16
CONV_WIDTH = 4
CONV_HALO = 8
LRU_C = 8.0
MLA_HEADS = 16
QK_NOPE = 64
QK_ROPE = 32
V_DIM = 64
HEAD_PAD = 128
ROPE_THETA = 10000.0
SM_SCALE = (QK_NOPE + QK_ROPE) ** -0.5
MOE_GROUPS = 4
EXPERTS_PER_GROUP = 8
N_EXPERTS = MOE_GROUPS * EXPERTS_PER_GROUP
TOP_K = 2
LANES = 128
ROW_DMA_UNROLL = 8
PAGE_RING = 3
LOG2E = 1.4426950408889634
VMEM_LIMIT = 56 * 1024 * 1024


def _cparams(n_axes):
    return pltpu.CompilerParams(dimension_semantics=("arbitrary",) * n_axes,
                                vmem_limit_bytes=VMEM_LIMIT)


def _dot(a, b):
    return jnp.dot(a, b, preferred_element_type=F32)


def _dot_nt(a, b):
    return lax.dot_general(a, b, (((1,), (1,)), ((), ())), preferred_element_type=F32)


def _rms(x, gain):
    return x * lax.rsqrt(jnp.mean(x * x, axis=-1, keepdims=True) + EPS) * gain


def _silu(x):
    return x * jax.nn.sigmoid(x)


def _ada_kernel(c_ref, w_ref, b_ref, o_ref):
    a = _silu(c_ref[...]).astype(BF16)
    o_ref[0] = _dot(a, w_ref[0].astype(BF16)) + b_ref[0]


def _ada(c_all, ada_w, ada_b):
    depth, d, n6 = ada_w.shape
    rows = c_all.shape[0]
    tn = 1024
    return pl.pallas_call(
        _ada_kernel,
        grid=(depth, n6 // tn),
        in_specs=[pl.BlockSpec((rows, d), lambda l, j: (0, 0)),
                  pl.BlockSpec((1, d, tn), lambda l, j: (l, 0, j)),
                  pl.BlockSpec((1, 1, tn), lambda l, j: (l, 0, j))],
        out_specs=pl.BlockSpec((1, rows, tn), lambda l, j: (l, 0, j)),
        out_shape=jax.ShapeDtypeStruct((depth, rows, n6), F32),
        compiler_params=_cparams(2),
        name="ada_mod",
    )(c_all, ada_w, ada_b.reshape(depth, 1, n6))


def _mix0_kernel(x_ref, sh_ref, sc_ref, gt_ref, pool0_ref, conv0_ref, lru0_ref,
                 gain_ref, win_ref, poolw_ref, pscale_ref, convw_ref, convb_ref,
                 wri_ref, bri_ref, lam_ref, wout_ref,
                 x1_ref, pooln_ref, convn_ref, lrun_ref,
                 zp_ref, zc_ref, h_ref, *, tt, bb, start, n_t):
    t = pl.program_id(1)
    rows = tt * bb
    pw = zp_ref.shape[1]
    p0 = POOL_HALO * bb
    c0 = CONV_HALO * bb

    @pl.when(t == 0)
    def _():
        zp_ref[0:bb, :] = jnp.zeros((bb, pw), F32)
        zp_ref[bb:p0, :] = pool0_ref[0]
        zc_ref[0:c0 - (CONV_WIDTH - 1) * bb, :] = jnp.zeros((c0 - (CONV_WIDTH - 1) * bb, pw), F32)
        zc_ref[c0 - (CONV_WIDTH - 1) * bb:c0, :] = conv0_ref[0]
        h_ref[...] = lru0_ref[0]

    def per_row(v):
        return v if bb == 1 else jnp.concatenate([v] * tt, axis=0)

    x = x_ref[0]
    h = _rms(x, gain_ref[...]) * (1.0 + per_row(sc_ref[0])) + per_row(sh_ref[0])
    proj = _dot(h.astype(BF16), win_ref[...])
    u_pool = proj[:, :pw]
    u_x = proj[:, pw:2 * pw]
    u_g = proj[:, 2 * pw:]
    zp_ref[p0:p0 + rows, :] = u_pool
    zc_ref[c0:c0 + rows, :] = u_x

    if bb == 1:
        tix = lax.broadcasted_iota(jnp.int32, (rows, 1), 0)
    else:
        tix = jnp.concatenate([jnp.full((bb, 1), i, jnp.int32) for i in range(tt)], axis=0)
    pos = start + t * tt + tix

    gd = pw // len(POOL_WINDOWS)
    means = []
    if bb == 1 and POOL_WINDOWS == (2, 4, 8, 16):
        sw = zp_ref[0:p0 + rows, :]
        for g, w in enumerate(POOL_WINDOWS):
            sw = sw + pltpu.roll(sw, w // 2, 0)
            cnt = jnp.minimum(pos + 1, w).astype(F32)
            means.append(sw[p0:, :gd] / cnt)
            if g + 1 < len(POOL_WINDOWS):
                sw = sw[:, gd:]
    else:
        for g, w in enumerate(POOL_WINDOWS):
            cols = slice(g * gd, (g + 1) * gd)
            acc = zp_ref[p0:p0 + rows, cols]
            for i in range(1, w):
                acc = acc + zp_ref[p0 - i * bb:p0 - i * bb + rows, cols]
            cnt = jnp.minimum(pos + 1, w).astype(F32)
            means.append(acc / cnt)
    pooled = jnp.concatenate(means, axis=-1) - u_pool
    y_a = _dot(pooled.astype(BF16), poolw_ref[...]) * pscale_ref[...]

    xc = convb_ref[...]
    if bb == 1:
        z_al = zc_ref[0:c0 + rows, :]
        for k in range(CONV_WIDTH):
            back = CONV_WIDTH - 1 - k
            z_k = z_al if back == 0 else pltpu.roll(z_al, back, 0)
            xc = xc + z_k[c0:, :] * convw_ref[k:k + 1, :]
    else:
        for k in range(CONV_WIDTH):
            off = c0 - (CONV_WIDTH - 1 - k) * bb
            xc = xc + zc_ref[off:off + rows, :] * convw_ref[k:k + 1, :]
    pre = _dot(xc.astype(BF16), wri_ref[...]) + bri_ref[...]
    r = jax.nn.sigmoid(pre[:, :pw])
    gi = jax.nn.sigmoid(pre[:, pw:])
    lam = lam_ref[...]
    softplus_neg = jnp.maximum(-lam, 0.0) + jnp.log1p(jnp.exp(-jnp.abs(lam)))
    log_a = -LRU_C * r * softplus_neg
    a = jnp.exp(log_a)
    one_m = 1.0 - a * a
    b = jnp.where(one_m > 0.0, one_m * lax.rsqrt(one_m), 0.0) * gi * xc

    if bb == 1:
        rowi = lax.broadcasted_iota(jnp.int32, (rows, 1), 0)
        s = 1
        while s < rows:
            if s < 8:
                keep = rowi >= s
                b = jnp.where(keep, a * pltpu.roll(b, s, 0) + b, b)
                a = jnp.where(keep, a * pltpu.roll(a, s, 0), a)
            else:
                b = jnp.concatenate([b[:s], a[s:] * b[:rows - s] + b[s:]], axis=0)
                a = jnp.concatenate([a[:s], a[s:] * a[:rows - s]], axis=0)
            s *= 2
        hs = b + a * h_ref[...]
        h_ref[...] = hs[rows - 1:rows, :]
    else:
        hprev = h_ref[...]
        parts = []
        for i in range(tt):
            hprev = a[i * bb:(i + 1) * bb] * hprev + b[i * bb:(i + 1) * bb]
            parts.append(hprev)
        hs = jnp.concatenate(parts, axis=0)
        h_ref[...] = hprev
    y_b = hs * jax.nn.gelu(u_g)

    mix = _dot(y_a.astype(BF16), wout_ref[0:pw, :]) + _dot(y_b.astype(BF16), wout_ref[pw:2 * pw, :])
    x1_ref[0] = x + per_row(gt_ref[0]) * mix

    @pl.when(t == n_t - 1)
    def _():
        pooln_ref[0] = zp_ref[p0 + rows - (POOL_HALO - 1) * bb:p0 + rows, :]
        convn_ref[0] = zc_ref[c0 + rows - (CONV_WIDTH - 1) * bb:c0 + rows, :]
        lrun_ref[0] = h_ref[...]

    if n_t > 1:
        @pl.when(t < n_t - 1)
        def _():
            zp_ref[bb:p0, :] = zp_ref[rows + bb:rows + p0, :]
            zc_ref[c0 - (CONV_WIDTH - 1) * bb:c0, :] = zc_ref[c0 + rows - (CONV_WIDTH - 1) * bb:c0 + rows, :]


def _mix0(x, sh, sc, gt, pool0, conv0, lru0, wts, *, tt, bb, start):
    nbb, tot, d = x.shape
    n_t = tot // (tt * bb)
    rows = tt * bb
    pw = pool0.shape[-1]
    hp = (POOL_HALO - 1) * bb
    hc = (CONV_WIDTH - 1) * bb

    def const(a):
        nd = a.ndim
        return pl.BlockSpec(a.shape, lambda i, j: (0,) * nd)

    def per_b(n_rows, width):
        return pl.BlockSpec((1, n_rows, width), lambda i, j: (i, 0, 0))

    kern = functools.partial(_mix0_kernel, tt=tt, bb=bb, start=start, n_t=n_t)
    return pl.pallas_call(
        kern,
        grid=(nbb, n_t),
        in_specs=[pl.BlockSpec((1, rows, d), lambda i, j: (i, j, 0)),
                  per_b(bb, d), per_b(bb, d), per_b(bb, d),
                  per_b(hp, pw), per_b(hc, pw), per_b(bb, pw)] + [const(w) for w in wts],
        out_specs=[pl.BlockSpec((1, rows, d), lambda i, j: (i, j, 0)),
                   per_b(hp, pw), per_b(hc, pw), per_b(bb, pw)],
        out_shape=[jax.ShapeDtypeStruct((nbb, tot, d), F32),
                   jax.ShapeDtypeStruct((nbb, hp, pw), F32),
                   jax.ShapeDtypeStruct((nbb, hc, pw), F32),
                   jax.ShapeDtypeStruct((nbb, bb, pw), F32)],
        scratch_shapes=[pltpu.VMEM(((POOL_HALO + tt) * bb, pw), F32),
                        pltpu.VMEM(((CONV_HALO + tt) * bb, pw), F32),
                        pltpu.VMEM((bb, pw), F32)],
        compiler_params=_cparams(2),
        name="mix0",
    )(x, sh, sc, gt, pool0, conv0, lru0, *wts)


def _route_kernel(*refs, with_proj):
    if with_proj:
        (x_ref, o_ref, wout_ref, gtm_ref, shf_ref, scf_ref, gain_ref, wrh_ref, wrl_ref, br_ref,
         x1_ref, hf_ref, info_ref, cnt_ref, run_ref) = refs
    else:
        (x_ref, shf_ref, scf_ref, gain_ref, wrh_ref, wrl_ref, br_ref,
         hf_ref, info_ref, cnt_ref, run_ref) = refs

    @pl.when(pl.program_id(0) == 0)
    def _():
        run_ref[...] = jnp.zeros(run_ref.shape, F32)

    x = x_ref[...]
    if with_proj:
        x = x + gtm_ref[0] * _dot(o_ref[...], wout_ref[...])
        x1_ref[...] = x
    hf = _rms(x, gain_ref[...]) * (1.0 + scf_ref[0]) + shf_ref[0]
    hf_ref[...] = hf

    hi = hf.astype(BF16)
    lo = (hf - hi.astype(F32)).astype(BF16)
    logits = _dot(hi, wrh_ref[...]) + _dot(lo, wrh_ref[...]) + _dot(hi, wrl_ref[...]) + br_ref[...]

    tq = logits.shape[0]
    lane = lax.broadcasted_iota(jnp.int32, (tq, LANES), 1).astype(F32)
    neg = -jnp.inf
    big = float(LANES)
    is_g = lane < MOE_GROUPS
    gl = jnp.where(is_g, logits, neg)
    mg = jnp.max(gl, axis=1, keepdims=True)
    gidx = jnp.min(jnp.where(gl == mg, lane, big), axis=1, keepdims=True)
    p_group = 1.0 / jnp.sum(jnp.where(is_g, jnp.exp(gl - mg), 0.0), axis=1, keepdims=True)
    first = MOE_GROUPS + gidx * EXPERTS_PER_GROUP
    el = jnp.where((lane >= first) & (lane < first + EXPERTS_PER_GROUP), logits, neg)
    v1 = jnp.max(el, axis=1, keepdims=True)
    i1 = jnp.min(jnp.where(el == v1, lane, big), axis=1, keepdims=True)
    el2 = jnp.where(lane == i1, neg, el)
    v2 = jnp.max(el2, axis=1, keepdims=True)
    i2 = jnp.min(jnp.where(el2 == v2, lane, big), axis=1, keepdims=True)
    ex = jnp.exp(v2 - v1)
    g1 = p_group / (1.0 + ex)
    g2 = p_group * ex / (1.0 + ex)
    e1 = i1 - MOE_GROUPS
    e2 = i2 - MOE_GROUPS

    oh1 = jnp.where(lane == e1, 1.0, 0.0)
    oh2 = jnp.where(lane == e2, 1.0, 0.0)
    oh = oh1 + oh2
    ri = lax.broadcasted_iota(jnp.int32, (tq, tq), 0)
    ci = lax.broadcasted_iota(jnp.int32, (tq, tq), 1)
    lower = jnp.where(ri > ci, 1.0, 0.0).astype(BF16)
    before = _dot(lower, oh.astype(BF16)) + run_ref[...]
    r1 = jnp.sum(before * oh1, axis=1, keepdims=True)
    r2 = jnp.sum(before * oh2, axis=1, keepdims=True)
    run_ref[...] = run_ref[...] + jnp.sum(oh, axis=0, keepdims=True)
    cnt_ref[...] = run_ref[...]

    info = jnp.where(lane == 0, e1, jnp.where(lane == 1, e2, jnp.where(lane == 2, g1, jnp.where(
        lane == 3, g2, jnp.where(lane == 4, r1, jnp.where(lane == 5, r2, 0.0))))))
    info_ref[...] = info


def _route(x, shf, scf, gain, wrh, wrl, br, *, tile, rows_per_mod, proj=None):
    n, d = x.shape
    nt = n // tile
    mrows = shf.shape[1]
    mod_spec = pl.BlockSpec((1, mrows, d), lambda i: ((i * tile) // rows_per_mod, 0, 0))
    row_spec = pl.BlockSpec((tile, d), lambda i: (i, 0))

    def const(a):
        nd = a.ndim
        return pl.BlockSpec(a.shape, lambda i: (0,) * nd)

    info_spec = pl.BlockSpec((tile, LANES), lambda i: (i, 0))
    cnt_spec = pl.BlockSpec((1, LANES), lambda i: (0, 0))
    outs_shape = [jax.ShapeDtypeStruct((n, d), F32), jax.ShapeDtypeStruct((n, LANES), F32),
                  jax.ShapeDtypeStruct((1, LANES), F32)]
    outs_spec = [row_spec, info_spec, cnt_spec]
    if proj is None:
        ins = [x, shf, scf, gain, wrh, wrl, br]
        in_specs = [row_spec, mod_spec, mod_spec, const(gain), const(wrh), const(wrl), const(br)]
    else:
        o, wout, gtm = proj
        ins = [x, o, wout, gtm, shf, scf, gain, wrh, wrl, br]
        in_specs = [row_spec, pl.BlockSpec((tile, o.shape[1]), lambda i: (i, 0)), const(wout), mod_spec,
                    mod_spec, mod_spec, const(gain), const(wrh), const(wrl), const(br)]
        outs_shape = [jax.ShapeDtypeStruct((n, d), F32)] + outs_shape
        outs_spec = [row_spec] + outs_spec
    return pl.pallas_call(
        functools.partial(_route_kernel, with_proj=proj is not None),
        grid=(nt,),
        in_specs=in_specs,
        out_specs=outs_spec,
        out_shape=outs_shape,
        scratch_shapes=[pltpu.VMEM((1, LANES), F32)],
        compiler_params=_cparams(1),
        name="route",
    )(*ins)


def _dispatch_kernel(slot_ref, hf_ref, xs_ref, sem, *, tile):
    def issue(g, c):
        for u in range(ROW_DMA_UNROLL):
            r = g * ROW_DMA_UNROLL + u
            for k in range(TOP_K):
                s = slot_ref[0, 0, TOP_K * r + k]
                pltpu.make_async_copy(hf_ref.at[pl.ds(r, 1), :], xs_ref.at[pl.ds(s, 1), :], sem).start()
        return c

    lax.fori_loop(0, tile // ROW_DMA_UNROLL, issue, 0)
    for k in range(TOP_K):
        pltpu.make_async_copy(hf_ref, xs_ref.at[pl.ds(0, tile), :], sem).wait()


def _dispatch(hf, slots, *, tile):
    n, d = hf.shape
    nt = n // tile
    return pl.pallas_call(
        functools.partial(_dispatch_kernel, tile=tile),
        grid=(nt,),
        in_specs=[pl.BlockSpec((1, 1, TOP_K * tile), lambda i: (i, 0, 0), memory_space=pltpu.SMEM),
                  pl.BlockSpec((tile, d), lambda i: (i, 0))],
        out_specs=pl.BlockSpec(memory_space=pl.ANY),
        out_shape=jax.ShapeDtypeStruct((TOP_K * n, d), F32),
        scratch_shapes=[pltpu.SemaphoreType.DMA(())],
        compiler_params=_cparams(1),
        name="moe_dispatch",
    )(slots.reshape(nt, 1, TOP_K * tile), hf)


def _gmm_kernel(wt_ref, we_ref, wlo_ref, whi_ref, xs_ref, wg_ref, wu_ref, wd_ref, y_ref,
                wgb_ref, wub_ref, wdb_ref):
    w = pl.program_id(0)
    prev = jnp.maximum(w - 1, 0)
    new_expert = (w == 0) | (we_ref[w] != we_ref[prev])
    new_tile = (w == 0) | (wt_ref[w] != wt_ref[prev])

    @pl.when(new_expert)
    def _():
        wgb_ref[...] = wg_ref[0, 0].astype(BF16)
        wub_ref[...] = wu_ref[0, 0].astype(BF16)
        wdb_ref[...] = wd_ref[0, 0].astype(BF16)

    @pl.when(new_tile)
    def _():
        y_ref[...] = jnp.zeros(y_ref.shape, F32)

    lo = wlo_ref[w]
    hi = whi_ref[w]

    @pl.when(hi > lo)
    def _():
        x = xs_ref[...].astype(BF16)
        g = _dot(x, wgb_ref[...])
        u = _dot(x, wub_ref[...])
        yv = _dot((_silu(g) * u).astype(BF16), wdb_ref[...])
        row = lax.broadcasted_iota(jnp.int32, (x.shape[0], 1), 0)
        y_ref[...] = y_ref[...] + jnp.where((row >= lo) & (row < hi), yv, 0.0)


def _gmm(xs, work, w_gate, w_up, w_down, *, tile, layer):
    m, d = xs.shape
    ff = w_gate.shape[-1]
    n_work = work[0].shape[0]
    grid_spec = pltpu.PrefetchScalarGridSpec(
        num_scalar_prefetch=4,
        grid=(n_work,),
        in_specs=[pl.BlockSpec((tile, d), lambda w, wt, we, wlo, whi: (wt[w], 0)),
                  pl.BlockSpec((1, 1, d, ff), lambda w, wt, we, wlo, whi: (layer, we[w], 0, 0)),
                  pl.BlockSpec((1, 1, d, ff), lambda w, wt, we, wlo, whi: (layer, we[w], 0, 0)),
                  pl.BlockSpec((1, 1, ff, d), lambda w, wt, we, wlo, whi: (layer, we[w], 0, 0))],
        out_specs=pl.BlockSpec((tile, d), lambda w, wt, we, wlo, whi: (wt[w], 0)),
        scratch_shapes=[pltpu.VMEM((d, ff), BF16), pltpu.VMEM((d, ff), BF16), pltpu.VMEM((ff, d), BF16)],
    )
    return pl.pallas_call(
        _gmm_kernel,
        grid_spec=grid_spec,
        out_shape=jax.ShapeDtypeStruct((m, d), F32),
        compiler_params=_cparams(1),
        name="moe_gmm",
    )(*work, xs, w_gate, w_up, w_down)


def _work_list(counts, n_slots, tile):
    n_tiles = n_slots // tile
    n_work = n_tiles + N_EXPERTS - 1
    ends = jnp.cumsum(counts)
    starts = ends - counts
    first_tile = starts // tile
    last_tile = jnp.maximum(ends - 1, 0) // tile
    n_items = jnp.where(counts > 0, last_tile - first_tile + 1, 0)
    item_end = jnp.cumsum(n_items)
    item_start = item_end - n_items
    w = jnp.arange(n_work, dtype=jnp.int32)
    used = w < item_end[-1]
    e = jnp.minimum(jnp.sum(w[:, None] >= item_end[None, :], axis=1), N_EXPERTS - 1).astype(jnp.int32)
    is_e = e[:, None] == jnp.arange(N_EXPERTS, dtype=jnp.int32)[None, :]

    def of_e(v):
        return jnp.sum(jnp.where(is_e, v[None, :], 0), axis=1)

    t = of_e(first_tile) + (w - of_e(item_start))
    lo = jnp.maximum(of_e(starts), t * tile) - t * tile
    hi = jnp.minimum(of_e(ends), (t + 1) * tile) - t * tile
    last_e = jnp.max(jnp.where(counts > 0, jnp.arange(N_EXPERTS), 0)).astype(jnp.int32)
    wt = jnp.where(used, t, n_tiles - 1).astype(jnp.int32)
    we = jnp.where(used, e, last_e).astype(jnp.int32)
    wlo = jnp.where(used, lo, 0).astype(jnp.int32)
    whi = jnp.where(used, hi, 0).astype(jnp.int32)
    return (wt, we, wlo, whi), starts


def _combine_kernel(*refs, tile, final):
    if final:
        slot_ref, nslot_ref, x_ref, gt_ref, info_ref, fg_ref, y_ref, out_ref, ybuf, sem = refs
    else:
        slot_ref, nslot_ref, x_ref, gt_ref, info_ref, y_ref, out_ref, ybuf, sem = refs
    i = pl.program_id(0)
    cur = i % 2

    def gather(slots_ref, buf):
        def issue(g, c):
            for u in range(ROW_DMA_UNROLL):
                r = g * ROW_DMA_UNROLL + u
                for k in range(TOP_K):
                    s = slots_ref[0, 0, TOP_K * r + k]
                    pltpu.make_async_copy(y_ref.at[pl.ds(s, 1), :], ybuf.at[buf, k, pl.ds(r, 1), :],
                                          sem.at[buf]).start()
            return c

        lax.fori_loop(0, tile // ROW_DMA_UNROLL, issue, 0)

    @pl.when(i == 0)
    def _():
        gather(slot_ref, cur)

    @pl.when(i + 1 < pl.num_programs(0))
    def _():
        gather(nslot_ref, 1 - cur)

    for k in range(TOP_K):
        pltpu.make_async_copy(y_ref.at[pl.ds(0, tile), :], ybuf.at[cur, k], sem.at[cur]).wait()

    info = info_ref[...]
    ffn = info[:, 2:3] * ybuf[cur, 0] + info[:, 3:4] * ybuf[cur, 1]
    out = x_ref[...] + gt_ref[0] * ffn
    if final:
        out = _rms(out, fg_ref[...])
    out_ref[...] = out


def _combine(x, gt, info, y, slots, *, tile, rows_per_mod, final_gain=None):
    n, d = x.shape
    nt = n // tile
    mrows = gt.shape[1]
    final = final_gain is not None
    slots3 = slots.reshape(nt, 1, TOP_K * tile)
    in_specs = [pl.BlockSpec((1, 1, TOP_K * tile), lambda i: (i, 0, 0), memory_space=pltpu.SMEM),
                pl.BlockSpec((1, 1, TOP_K * tile), lambda i: (jnp.minimum(i + 1, nt - 1), 0, 0),
                             memory_space=pltpu.SMEM),
                pl.BlockSpec((tile, d), lambda i: (i, 0)),
                pl.BlockSpec((1, mrows, d), lambda i: ((i * tile) // rows_per_mod, 0, 0)),
                pl.BlockSpec((tile, LANES), lambda i: (i, 0))]
    ins = [slots3, slots3, x, gt, info]
    if final:
        in_specs.append(pl.BlockSpec(final_gain.shape, lambda i: (0, 0)))
        ins.append(final_gain)
    in_specs.append(pl.BlockSpec(memory_space=pl.ANY))
    ins.append(y)
    return pl.pallas_call(
        functools.partial(_combine_kernel, tile=tile, final=final),
        grid=(nt,),
        in_specs=in_specs,
        out_specs=pl.BlockSpec((tile, d), lambda i: (i, 0)),
        out_shape=jax.ShapeDtypeStruct((n, d), F32),
        scratch_shapes=[pltpu.VMEM((2, TOP_K, tile, d), F32), pltpu.SemaphoreType.DMA((2,))],
        compiler_params=_cparams(1),
        name="moe_combine",
    )(*ins)


def _moe(x, mods, rw, experts, *, layer, tile, gmm_tile, rows_per_mod, proj=None, final_gain=None):
    shf, scf, gtf = mods
    gain, wrh, wrl, br = rw
    res = _route(x, shf, scf, gain, wrh, wrl, br, tile=tile, rows_per_mod=rows_per_mod, proj=proj)
    if proj is not None:
        x, hf, info, cnt = res
    else:
        hf, info, cnt = res
    n = x.shape[0]
    counts = cnt[0, :N_EXPERTS].astype(jnp.int32)
    work, starts = _work_list(counts, TOP_K * n, gmm_tile)
    eid = info[:, 0:TOP_K].astype(jnp.int32)
    rank = info[:, 4:4 + TOP_K].astype(jnp.int32)
    is_e = eid[:, :, None] == jnp.arange(N_EXPERTS, dtype=jnp.int32)
    slots = (jnp.sum(jnp.where(is_e, starts.astype(jnp.int32), 0), axis=-1) + rank).reshape(-1)
    xs = _dispatch(hf, slots, tile=tile)
    y = _gmm(xs, work, *experts, tile=gmm_tile, layer=layer)
    return _combine(x, gtf, info, y, slots, tile=tile, rows_per_mod=rows_per_mod, final_gain=final_gain)


def _rope_turn(blk, c, s):
    return blk * c + pltpu.roll(blk, LANES - QK_ROPE, 1) * s


def _mla_proj_kernel(*refs, sample):
    if sample:
        (x_ref, sh_ref, sc_ref, gain_ref, win_ref, qg_ref, wuq_ref, kvg_ref, c_ref, s_ref,
         ckv_ref, kr_ref, q_ref) = refs
    else:
        (x_ref, sh_ref, sc_ref, gain_ref, win_ref, qg_ref, wuq_ref, kvg_ref, c_ref, s_ref, wuk_ref, wuvt_ref,
         vonet_ref, ct_ref, st_ref, ckv_ref, kr_ref, q_ref, k_ref, v_ref) = refs
    q_lora = qg_ref.shape[1]
    kv_lora = kvg_ref.shape[1]
    h = _rms(x_ref[...], gain_ref[...]) * (1.0 + sc_ref[0]) + sh_ref[0]
    proj = _dot(h.astype(BF16), win_ref[...])
    qn = _rms(proj[:, :q_lora], qg_ref[...])
    ckv = _rms(proj[:, q_lora:q_lora + kv_lora], kvg_ref[...])
    c = c_ref[...]
    s = s_ref[...]
    kf = _rope_turn(proj[:, q_lora + kv_lora:], c, s)
    ckv_ref[...] = ckv
    kr_ref[...] = kf[:, QK_NOPE:QK_NOPE + QK_ROPE]
    qn_b = qn.astype(BF16)
    if sample:
        q = _dot(qn_b, wuq_ref[...])
        for hh in range(MLA_HEADS):
            cols = slice(hh * HEAD_PAD, (hh + 1) * HEAD_PAD)
            q_ref[:, cols] = _rope_turn(q[:, cols], c, s) * SM_SCALE
    else:
        ckv_b = ckv.astype(BF16)
        kn = _dot(ckv_b, wuk_ref[...])
        qt = _dot_nt(wuq_ref[...], qn_b)
        v_ref[0] = (_dot_nt(wuvt_ref[...], ckv_b) + vonet_ref[...]).astype(BF16)
        ct = ct_ref[...]
        st = st_ref[...]
        for hh in range(MLA_HEADS):
            cols = slice(hh * HEAD_PAD, (hh + 1) * HEAD_PAD)
            k_ref[:, cols] = (kn[:, cols] + kf).astype(BF16)
            blk = qt[cols, :]
            turned = jnp.concatenate([blk[QK_ROPE:], blk[:QK_ROPE]], axis=0)
            q_ref[0, cols, :] = ((blk * ct + turned * st) * (SM_SCALE * LOG2E)).astype(BF16)


def _mla_proj(x, sh, sc, gain, win, qg, wuq, kvg, ctab, stab, *, tile, rows_per_mod, tab_tiles, kv_w=None):
    n, d = x.shape
    nt = n // tile
    mrows = sh.shape[1]
    sample = kv_w is None
    kv_lora = kvg.shape[1]

    def const(a):
        nd = a.ndim
        return pl.BlockSpec(a.shape, lambda i: (0,) * nd)

    row = lambda width: pl.BlockSpec((tile, width), lambda i: (i, 0))
    mod_spec = pl.BlockSpec((1, mrows, d), lambda i: ((i * tile) // rows_per_mod, 0, 0))
    tab_spec = pl.BlockSpec((tile, LANES), lambda i: (i % tab_tiles, 0))
    ins = [x, sh, sc, gain, win, qg, wuq, kvg, ctab, stab]
    in_specs = [row(d), mod_spec, mod_spec, const(gain), const(win), const(qg), const(wuq), const(kvg),
                tab_spec, tab_spec]
    qw = MLA_HEADS * HEAD_PAD
    out_shape = [jax.ShapeDtypeStruct((n, kv_lora), F32), jax.ShapeDtypeStruct((n, QK_ROPE), F32)]
    out_specs = [row(kv_lora), row(QK_ROPE)]
    if sample:
        out_shape.append(jax.ShapeDtypeStruct((n, qw), F32))
        out_specs.append(row(qw))
    else:
        wuk, wuvt, vonet, ctab_t, stab_t = kv_w
        seq_len = tab_tiles * tile
        tab_t_spec = pl.BlockSpec((LANES, tile), lambda i: (0, i % tab_tiles))
        by_seq = pl.BlockSpec((1, qw, tile), lambda i: (i // tab_tiles, 0, i % tab_tiles))
        ins += [wuk, wuvt, vonet, ctab_t, stab_t]
        in_specs += [const(wuk), const(wuvt), const(vonet), tab_t_spec, tab_t_spec]
        out_shape += [jax.ShapeDtypeStruct((n // seq_len, qw, seq_len), BF16), jax.ShapeDtypeStruct((n, qw), BF16),
                      jax.ShapeDtypeStruct((n // seq_len, qw, seq_len), BF16)]
        out_specs += [by_seq, row(qw), by_seq]
    return pl.pallas_call(
        functools.partial(_mla_proj_kernel, sample=sample),
        grid=(nt,),
        in_specs=in_specs,
        out_specs=out_specs,
        out_shape=out_shape,
        compiler_params=_cparams(1),
        name="mla_proj",
    )(*ins)


def _attn_kernel(q_ref, k_ref, v_ref, o_ref, sa_ref, sb_ref, m_ref, acc_ref, *, tq, tk):
    i = pl.program_id(2)
    heads = range(2)

    def cols(hh):
        return slice(hh * HEAD_PAD, (hh + 1) * HEAD_PAD)

    def scores(j, dst_ref):
        keys = pl.ds(pl.multiple_of(j * tk, tk), tk)
        for hh in heads:
            dst_ref[hh] = _dot(k_ref[0, keys, cols(hh)], q_ref[0, cols(hh), :])

    def absorb(j, src_ref, masked):
        keys = pl.ds(pl.multiple_of(j * tk, tk), tk)
        if masked:
            keep = lax.broadcasted_iota(jnp.int32, (tk, tq), 0) <= lax.broadcasted_iota(jnp.int32, (tk, tq), 1)
        for hh in heads:
            s = src_ref[hh]
            if masked:
                s = jnp.where(keep, s, -jnp.inf)
            m = m_ref[hh]
            m_new = jnp.maximum(m, jnp.max(s, axis=0, keepdims=True))
            p = jnp.exp2(s - m_new)
            acc_ref[hh] = jnp.exp2(m - m_new) * acc_ref[hh] + _dot(v_ref[0, cols(hh), keys], p.astype(BF16))
            m_ref[hh] = m_new

    m_ref[...] = jnp.full(m_ref.shape, -jnp.inf, F32)
    acc_ref[...] = jnp.zeros(acc_ref.shape, F32)
    scores(0, sa_ref)

    def pair(jj, c):
        j = 2 * jj
        scores(j + 1, sb_ref)
        absorb(j, sa_ref, False)
        scores(j + 2, sa_ref)
        absorb(j + 1, sb_ref, False)
        return c

    lax.fori_loop(0, i // 2, pair, 0)

    @pl.when(i % 2 == 1)
    def _():
        scores(i, sb_ref)
        absorb(i - 1, sa_ref, False)
        absorb(i, sb_ref, True)

    @pl.when(i % 2 == 0)
    def _():
        absorb(i, sa_ref, True)

    out_t = jnp.concatenate([acc_ref[hh][:V_DIM] / acc_ref[hh][V_DIM:V_DIM + 1] for hh in heads], axis=0)
    o_ref[0] = out_t.T.astype(o_ref.dtype)


def _attention(q_t, k, v_t, *, tq, tk):
    assert tq == tk, "one diagonal key tile per query tile"
    b, s, _ = k.shape
    pairs = MLA_HEADS // 2
    return pl.pallas_call(
        functools.partial(_attn_kernel, tq=tq, tk=tk),
        grid=(b, pairs, s // tq),
        in_specs=[pl.BlockSpec((1, 2 * HEAD_PAD, tq), lambda bi, p, i: (bi, p, i)),
                  pl.BlockSpec((1, s, 2 * HEAD_PAD), lambda bi, p, i: (bi, 0, p)),
                  pl.BlockSpec((1, 2 * HEAD_PAD, s), lambda bi, p, i: (bi, p, 0))],
        out_specs=pl.BlockSpec((1, tq, 2 * V_DIM), lambda bi, p, i: (bi, i, p)),
        out_shape=jax.ShapeDtypeStruct((b, s, MLA_HEADS * V_DIM), BF16),
        scratch_shapes=[pltpu.VMEM((2, tk, tq), F32), pltpu.VMEM((2, tk, tq), F32),
                        pltpu.VMEM((2, 1, tq), F32), pltpu.VMEM((2, HEAD_PAD, tq), F32)],
        compiler_params=_cparams(3),
        name="prompt_attn",
    )(q_t, k, v_t)


def _qabs_kernel(q_ref, m_ref, o_ref):
    o_ref[0] = _dot(q_ref[...].astype(BF16), m_ref[0])


def _qabs(q, mats):
    n = q.shape[0]
    width = mats.shape[-1]
    return pl.pallas_call(
        _qabs_kernel,
        grid=(MLA_HEADS,),
        in_specs=[pl.BlockSpec((n, HEAD_PAD), lambda h: (0, h)),
                  pl.BlockSpec((1, HEAD_PAD, width), lambda h: (h, 0, 0))],
        out_specs=pl.BlockSpec((1, n, width), lambda h: (h, 0, 0)),
        out_shape=jax.ShapeDtypeStruct((MLA_HEADS, n, width), F32),
        compiler_params=_cparams(1),
        name="sample_qabs",
    )(q, mats)


def _paged_attn_kernel(pt_ref, q_ref, cn_ref, rn_ref, ck_hbm, kr_hbm, o_ref,
                       ckbuf, krbuf, m_ref, l_ref, acc_ref, sem, *, group, pages, page, n_chunks, t_new, t_pad):
    bg = pl.program_id(0)
    c = pl.program_id(1)
    step = bg * n_chunks + c
    n_steps = pl.num_programs(0) * n_chunks
    slot = step % PAGE_RING
    kv_lora = ckbuf.shape[-1]

    def start_chunk(st):
        bgi, ci, sl = st // n_chunks, st % n_chunks, st % PAGE_RING
        for g in range(group):
            for p in range(pages):
                phys = pt_ref[bgi * group + g, ci * pages + p]
                pltpu.make_async_copy(ck_hbm.at[phys], ckbuf.at[sl, g, pl.ds(p * page, page), :],
                                      sem.at[0, sl]).start()
                pltpu.make_async_copy(kr_hbm.at[phys], krbuf.at[sl, g, p], sem.at[1, sl]).start()

    @pl.when(step == 0)
    def _():
        for ahead in range(PAGE_RING - 1):
            @pl.when(ahead < n_steps)
            def _():
                start_chunk(step + ahead)

    pltpu.make_async_copy(ckbuf.at[slot], ckbuf.at[slot], sem.at[0, slot]).wait()
    pltpu.make_async_copy(krbuf.at[slot], krbuf.at[slot], sem.at[1, slot]).wait()

    @pl.when(c == 0)
    def _():
        m_ref[...] = jnp.full(m_ref.shape, -jnp.inf, F32)
        l_ref[...] = jnp.zeros(l_ref.shape, F32)
        acc_ref[...] = jnp.zeros(acc_ref.shape, F32)

    def absorb(g, s, values):
        m = m_ref[g]
        m_new = jnp.maximum(m, jnp.max(s, axis=1, keepdims=True))
        alpha = jnp.exp(m - m_new)
        p = jnp.exp(s - m_new)
        l_ref[g] = alpha * l_ref[g] + jnp.sum(p, axis=1, keepdims=True)
        acc_ref[g] = alpha * acc_ref[g] + _dot(p.astype(BF16), values)
        m_ref[g] = m_new

    cks, scores = [], []
    for g in range(group):
        qa = q_ref[g]
        q_lat = qa[:, :kv_lora].astype(BF16)
        q_rope = qa[:, kv_lora:kv_lora + QK_ROPE].astype(BF16)
        ck = ckbuf[slot, g].astype(BF16)
        kr_t = jnp.concatenate([krbuf[slot, g, p] for p in range(pages)], axis=1).astype(BF16)
        cks.append(ck)
        scores.append(_dot_nt(q_lat, ck) + _dot(q_rope, kr_t))

    nxt = step + PAGE_RING - 1

    @pl.when(nxt < n_steps)
    def _():
        start_chunk(nxt)

    for g in range(group):
        absorb(g, scores[g], cks[g])

    @pl.when(c == n_chunks - 1)
    def _():
        rows = q_ref.shape[1]
        t_q = lax.broadcasted_iota(jnp.int32, (rows, t_pad), 0) % t_new
        t_k = lax.broadcasted_iota(jnp.int32, (rows, t_pad), 1)
        for g in range(group):
            qa = q_ref[g]
            q_lat = qa[:, :kv_lora].astype(BF16)
            q_rope = qa[:, kv_lora:kv_lora + QK_ROPE].astype(BF16)
            cn = cn_ref[g].astype(BF16)
            s = _dot_nt(q_lat, cn) + _dot_nt(q_rope, rn_ref[g].astype(BF16))
            absorb(g, jnp.where(t_k <= t_q, s, -jnp.inf), cn)
            o_ref[g] = acc_ref[g] / l_ref[g]


def _paged_attn(page_table, qabs, ckv_new, kr_new, cache_ck, cache_kr_t, *, group, pages, t_new):
    b, rows, width = qabs.shape
    n_pages = page_table.shape[1]
    n_chunks = n_pages // pages
    page, kv_lora = cache_ck.shape[1:]
    t_pad = ckv_new.shape[1]
    grid_spec = pltpu.PrefetchScalarGridSpec(
        num_scalar_prefetch=1,
        grid=(b // group, n_chunks),
        in_specs=[pl.BlockSpec((group, rows, width), lambda bi, ci, pt: (bi, 0, 0)),
                  pl.BlockSpec((group, t_pad, kv_lora), lambda bi, ci, pt: (bi, 0, 0)),
                  pl.BlockSpec((group, t_pad, QK_ROPE), lambda bi, ci, pt: (bi, 0, 0)),
                  pl.BlockSpec(memory_space=pl.ANY),
                  pl.BlockSpec(memory_space=pl.ANY)],
        out_specs=pl.BlockSpec((group, rows, kv_lora), lambda bi, ci, pt: (bi, 0, 0)),
        scratch_shapes=[pltpu.VMEM((PAGE_RING, group, pages * page, kv_lora), F32),
                        pltpu.VMEM((PAGE_RING, group, pages, QK_ROPE, page), F32),
                        pltpu.VMEM((group, rows, 1), F32), pltpu.VMEM((group, rows, 1), F32),
                        pltpu.VMEM((group, rows, kv_lora), F32),
                        pltpu.SemaphoreType.DMA((2, PAGE_RING))],
    )
    return pl.pallas_call(
        functools.partial(_paged_attn_kernel, group=group, pages=pages, page=page, n_chunks=n_chunks, t_new=t_new,
                          t_pad=t_pad),
        grid_spec=grid_spec,
        out_shape=jax.ShapeDtypeStruct((b, rows, kv_lora), F32),
        compiler_params=_cparams(2),
        name="paged_attn",
    )(page_table, qabs, ckv_new, kr_new, cache_ck, cache_kr_t)


def _vup_kernel(o_ref, w_ref, out_ref):
    lat = jnp.concatenate([o_ref[0], o_ref[1]], axis=-1).astype(BF16)
    out_ref[...] = _dot(lat, w_ref[0]).astype(out_ref.dtype)


def _vup(o_lat, w_pairs):
    h, n, c = o_lat.shape
    return pl.pallas_call(
        _vup_kernel,
        grid=(h // 2,),
        in_specs=[pl.BlockSpec((2, n, c), lambda p: (p, 0, 0)),
                  pl.BlockSpec((1, 2 * c, 2 * V_DIM), lambda p: (p, 0, 0))],
        out_specs=pl.BlockSpec((n, 2 * V_DIM), lambda p: (0, p)),
        out_shape=jax.ShapeDtypeStruct((n, h * V_DIM), BF16),
        compiler_params=_cparams(1),
        name="sample_vup",
    )(o_lat, w_pairs)


def _block_diag(w):
    g, a, b = w.shape
    out = jnp.zeros((g * a, g * b), w.dtype)
    for i in range(g):
        out = out.at[i * a:(i + 1) * a, i * b:(i + 1) * b].set(w[i])
    return out


def _swap_halves(w):
    half = w.shape[-1] // 2
    return jnp.concatenate([w[..., half:], w[..., :half]], axis=-1)


def _rope_tables(pos):
    half = QK_ROPE // 2
    inv_freq = ROPE_THETA ** (-jnp.arange(half, dtype=F32) / half)
    ang = pos.astype(F32)[:, None] * inv_freq[None, :]
    cos, sin = jnp.cos(ang), jnp.sin(ang)
    n = pos.shape[0]
    ctab = jnp.concatenate([jnp.ones((n, QK_NOPE), F32), cos, cos, jnp.zeros((n, QK_ROPE), F32)], axis=1)
    stab = jnp.concatenate([jnp.zeros((n, QK_NOPE), F32), -sin, sin, jnp.zeros((n, QK_ROPE), F32)], axis=1)
    return ctab, stab


def _router_weights(w_group, b_group, w_expert, b_expert):
    d = w_group.shape[0]
    we = jnp.transpose(w_expert, (1, 0, 2)).reshape(d, N_EXPERTS)
    w = jnp.concatenate([w_group, we, jnp.zeros((d, LANES - MOE_GROUPS - N_EXPERTS), F32)], axis=1)
    bias = jnp.concatenate([b_group, b_expert.reshape(-1), jnp.zeros((LANES - MOE_GROUPS - N_EXPERTS,), F32)])
    hi = w.astype(BF16)
    lo = (w - hi.astype(F32)).astype(BF16)
    return hi, lo, bias.reshape(1, LANES)


def _pick(n, pref):
    t = min(n, pref)
    while n % t:
        t //= 2
    return t


def kernel(x_prompt, x_sample, state_pool, state_conv, state_lru, cache_kv_latent, cache_k_rope, page_table,
           c_prompt, c_sample, ada_w, ada_b, norm_mix, norm_ffn, ab_w_in, ab_w_out, pool_w, pool_scale,
           conv_w, conv_b, lru_w_r, lru_b_r, lru_w_i, lru_b_i, lru_lambda,
           mla_w_in, mla_q_norm, mla_w_uq, mla_kv_norm, mla_w_uk, mla_w_uv, mla_w_out,
           router_w_group, router_b_group, router_w_expert, router_b_expert,
           moe_w_gate, moe_w_up, moe_w_down, final_norm):
    bp, seq, d = x_prompt.shape
    bs, t_new, _ = x_sample.shape
    depth = ada_w.shape[0]
    pw = pool_scale.shape[-1]
    page = cache_kv_latent.shape[2]
    past_len = page_table.shape[1] * page
    q_lora = mla_q_norm.shape[-1]
    kv_lora = mla_kv_norm.shape[-1]

    n_c = bp + bs
    n_c_pad = -(-n_c // 8) * 8
    c_all = jnp.concatenate([c_prompt, c_sample, jnp.zeros((n_c_pad - n_c, d), F32)], axis=0)
    mod = _ada(c_all, ada_w, ada_b)

    def mods(layer, lo, hi):
        return [mod[layer, lo:hi, k * d:(k + 1) * d] for k in range(6)]

    n_p = bp * seq
    n_s = bs * t_new
    tile_p = _pick(seq, 512)
    tile_s = _pick(n_s, 512)
    gmm_tile_p = _pick(TOP_K * n_p, 512)
    gmm_tile_s = _pick(TOP_K * n_s, 256)

    xp = x_prompt.reshape(n_p, d)
    xs = x_sample.reshape(n_s, d)
    outs = {}

    for layer in range(depth):
        sh_m, sc_m, gt_m, sh_f, sc_f, gt_f = mods(layer, 0, bp)
        sh_ms, sc_ms, gt_ms, sh_fs, sc_fs, gt_fs = mods(layer, bp, bp + bs)
        rep = lambda v: jnp.repeat(v, t_new, axis=0)[None]
        per_seq = lambda v: v[:, None, :]
        rw = (norm_ffn[layer].reshape(1, d),) + _router_weights(
            router_w_group[layer], router_b_group[layer], router_w_expert[layer], router_b_expert[layer])
        experts = (moe_w_gate, moe_w_up, moe_w_down)
        final_gain = final_norm.reshape(1, d) if layer == depth - 1 else None
        ffn_mods_p = (per_seq(sh_f), per_seq(sc_f), per_seq(gt_f))
        ffn_mods_s = (rep(sh_fs), rep(sc_fs), rep(gt_fs))

        if layer % 2 == 0:
            e = layer // 2
            wts = (norm_mix[layer].reshape(1, d), ab_w_in[e].astype(BF16), _block_diag(pool_w[e]).astype(BF16),
                   pool_scale[e].reshape(1, pw), conv_w[e], conv_b[e].reshape(1, pw),
                   jnp.concatenate([_block_diag(lru_w_r[e]), _block_diag(lru_w_i[e])], axis=1).astype(BF16),
                   jnp.concatenate([lru_b_r[e], lru_b_i[e]]).reshape(1, 2 * pw),
                   lru_lambda[e].reshape(1, pw), ab_w_out[e].astype(BF16))
            x1p, pool_p, conv_p, lru_p = _mix0(
                xp.reshape(bp, seq, d), per_seq(sh_m), per_seq(sc_m), per_seq(gt_m),
                jnp.zeros((bp, POOL_HALO - 1, pw), F32), jnp.zeros((bp, CONV_WIDTH - 1, pw), F32),
                jnp.zeros((bp, 1, pw), F32), wts, tt=tile_p, bb=1, start=0)
            tm = lambda a: jnp.swapaxes(a, 0, 1).reshape(1, -1, a.shape[-1])
            x1s, pool_s, conv_s, lru_s = _mix0(
                tm(xs.reshape(bs, t_new, d)), sh_ms[None], sc_ms[None], gt_ms[None],
                tm(state_pool[e]), tm(state_conv[e]), state_lru[e][None], wts, tt=t_new, bb=bs, start=past_len)
            bm = lambda a, n: jnp.swapaxes(a.reshape(n, bs, a.shape[-1]), 0, 1)
            outs.setdefault("pool_p", []).append(pool_p)
            outs.setdefault("pool_s", []).append(bm(pool_s, POOL_HALO - 1))
            outs.setdefault("conv_p", []).append(conv_p)
            outs.setdefault("conv_s", []).append(bm(conv_s, CONV_WIDTH - 1))
            outs.setdefault("lru_p", []).append(lru_p.reshape(bp, pw))
            outs.setdefault("lru_s", []).append(lru_s.reshape(bs, pw))
            xp = x1p.reshape(n_p, d)
            xs = bm(x1s, t_new).reshape(n_s, d)
            proj_p = proj_s = None
        else:
            o = layer // 2
            gain = norm_mix[layer].reshape(1, d)
            w_in = mla_w_in[o]
            w_kr = w_in[:, q_lora + kv_lora:]
            win = jnp.concatenate([w_in[:, :q_lora + kv_lora], jnp.zeros((d, QK_NOPE), F32), w_kr,
                                   _swap_halves(w_kr)], axis=1).astype(BF16)
            wq = mla_w_uq[o].reshape(q_lora, MLA_HEADS, QK_NOPE + QK_ROPE)
            wuq = jnp.concatenate([wq, _swap_halves(wq[..., QK_NOPE:])], axis=-1).reshape(
                q_lora, MLA_HEADS * HEAD_PAD).astype(BF16)
            wuk = jnp.concatenate([mla_w_uk[o], jnp.zeros((kv_lora, MLA_HEADS, HEAD_PAD - QK_NOPE), F32)],
                                  axis=-1).reshape(kv_lora, MLA_HEADS * HEAD_PAD).astype(BF16)
            wuv_t = jnp.concatenate([mla_w_uv[o], jnp.zeros((kv_lora, MLA_HEADS, HEAD_PAD - V_DIM), F32)],
                                    axis=-1).reshape(kv_lora, MLA_HEADS * HEAD_PAD).T.astype(BF16)
            vone_t = jnp.tile((jnp.arange(HEAD_PAD) == V_DIM).astype(F32), MLA_HEADS).reshape(-1, 1)
            qg = mla_q_norm[o].reshape(1, q_lora)
            kvg = mla_kv_norm[o].reshape(1, kv_lora)
            wout = mla_w_out[o].astype(BF16)

            ctab, stab = _rope_tables(jnp.arange(seq, dtype=jnp.int32))
            ckv_p, kr_p, qt_p, k_p, vt_p = _mla_proj(
                xp, per_seq(sh_m), per_seq(sc_m), gain, win, qg, wuq.T, kvg, ctab, stab,
                tile=tile_p, rows_per_mod=seq, tab_tiles=seq // tile_p, kv_w=(wuk, wuv_t, vone_t, ctab.T, stab.T))
            qw = MLA_HEADS * HEAD_PAD
            o_p = _attention(qt_p, k_p.reshape(bp, seq, qw), vt_p, tq=tile_p, tk=tile_p)
            proj_p = (o_p.reshape(n_p, MLA_HEADS * V_DIM), wout, per_seq(gt_m))
            outs.setdefault("lat_p", []).append(ckv_p.reshape(bp, seq, kv_lora))
            outs.setdefault("rope_p", []).append(kr_p.reshape(bp, seq, QK_ROPE))

            pos_s = jnp.tile(past_len + jnp.arange(t_new, dtype=jnp.int32), bs)
            ctab_s, stab_s = _rope_tables(pos_s)
            ckv_s, kr_s, q_s = _mla_proj(
                xs, rep(sh_ms), rep(sc_ms), gain, win, qg, wuq, kvg, ctab_s, stab_s,
                tile=tile_s, rows_per_mod=n_s, tab_tiles=n_s // tile_s)
            aw = kv_lora + HEAD_PAD
            wk_t = jnp.transpose(mla_w_uk[o], (1, 2, 0))
            mats = jnp.zeros((MLA_HEADS, HEAD_PAD, aw), F32)
            mats = mats.at[:, :QK_NOPE, :kv_lora].set(wk_t)
            mats = mats.at[:, QK_NOPE:QK_NOPE + QK_ROPE, kv_lora:kv_lora + QK_ROPE].set(
                jnp.broadcast_to(jnp.eye(QK_ROPE, dtype=F32), (MLA_HEADS, QK_ROPE, QK_ROPE)))
            qabs = _qabs(q_s, mats.astype(BF16))
            qabs = jnp.transpose(qabs.reshape(MLA_HEADS, bs, t_new, aw), (1, 0, 2, 3)).reshape(
                bs, MLA_HEADS * t_new, aw)
            t_pad = 8
            pad_t = lambda a: jnp.pad(a.reshape(bs, t_new, -1), ((0, 0), (0, t_pad - t_new), (0, 0)))
            o_lat = _paged_attn(page_table, qabs, pad_t(ckv_s), pad_t(kr_s), cache_kv_latent[o],
                                jnp.swapaxes(cache_k_rope[o], 1, 2),
                                group=_pick(bs, 4), pages=_pick(page_table.shape[1], 8), t_new=t_new)
            o_lat = jnp.transpose(o_lat.reshape(bs, MLA_HEADS, t_new, kv_lora), (1, 0, 2, 3)).reshape(
                MLA_HEADS, n_s, kv_lora)
            wv = jnp.transpose(mla_w_uv[o], (1, 0, 2))
            w_pairs = jnp.zeros((MLA_HEADS // 2, 2 * kv_lora, 2 * V_DIM), F32)
            w_pairs = w_pairs.at[:, :kv_lora, :V_DIM].set(wv[0::2]).at[:, kv_lora:, V_DIM:].set(wv[1::2])
            o_s = _vup(o_lat, w_pairs.astype(BF16))
            proj_s = (o_s, wout, rep(gt_ms))
            outs.setdefault("lat_s", []).append(ckv_s.reshape(bs, t_new, kv_lora))
            outs.setdefault("rope_s", []).append(kr_s.reshape(bs, t_new, QK_ROPE))

        xp = _moe(xp, ffn_mods_p, rw, experts, layer=layer, tile=tile_p, gmm_tile=gmm_tile_p, rows_per_mod=seq,
                  proj=proj_p, final_gain=final_gain)
        xs = _moe(xs, ffn_mods_s, rw, experts, layer=layer, tile=tile_s, gmm_tile=gmm_tile_s, rows_per_mod=n_s,
                  proj=proj_s, final_gain=final_gain)

    st = lambda k: jnp.stack(outs[k])
    return (xp.reshape(bp, seq, d), xs.reshape(bs, t_new, d),
            st("pool_p"), st("pool_s"), st("conv_p"), st("conv_s"), st("lru_p"), st("lru_s"),
            st("lat_p"), st("lat_s"), st("rope_p"), st("rope_s"))
```

```python
import functools

import jax
import jax.numpy as jnp
from jax import lax
from jax.experimental import pallas as pl
from jax.experimental.pallas import tpu as pltpu

F32 = jnp.float32
BF16 = jnp.bfloat16

EPS = 1e-6
POOL_WINDOWS = (2, 4, 8, 16)
POOL_HALO = 16
CONV_WIDTH = 4
CONV_HALO = 8
LRU_C = 8.0
MLA_HEADS = 16
QK_NOPE = 64
QK_ROPE = 32
V_DIM = 64
HEAD_PAD = 128
ROPE_THETA = 10000.0
SM_SCALE = (QK_NOPE + QK_ROPE) ** -0.5
MOE_GROUPS = 4
EXPERTS_PER_GROUP = 8
N_EXPERTS = MOE_GROUPS * EXPERTS_PER_GROUP
TOP_K = 2
LANES = 128
ROW_DMA_UNROLL = 8
PAGE_RING = 3
LOG2E = 1.4426950408889634
VMEM_LIMIT = 56 * 1024 * 1024


def _cparams(n_axes):
    return pltpu.CompilerParams(dimension_semantics=("arbitrary",) * n_axes,
                                vmem_limit_bytes=VMEM_LIMIT)


def _dot(a, b):
    return jnp.dot(a, b, preferred_element_type=F32)


def _dot_nt(a, b):
    return lax.dot_general(a, b, (((1,), (1,)), ((), ())), preferred_element_type=F32)


def _rms(x, gain):
    return x * lax.rsqrt(jnp.mean(x * x, axis=-1, keepdims=True) + EPS) * gain


def _silu(x):
    return x * jax.nn.sigmoid(x)


def _ada_kernel(c_ref, w_ref, b_ref, o_ref):
    a = _silu(c_ref[...]).astype(BF16)
    o_ref[0] = _dot(a, w_ref[0].astype(BF16)) + b_ref[0]


def _ada(c_all, ada_w, ada_b):
    depth, d, n6 = ada_w.shape
    rows = c_all.shape[0]
    tn = 1024
    return pl.pallas_call(
        _ada_kernel,
        grid=(depth, n6 // tn),
        in_specs=[pl.BlockSpec((rows, d), lambda l, j: (0, 0)),
                  pl.BlockSpec((1, d, tn), lambda l, j: (l, 0, j)),
                  pl.BlockSpec((1, 1, tn), lambda l, j: (l, 0, j))],
        out_specs=pl.BlockSpec((1, rows, tn), lambda l, j: (l, 0, j)),
        out_shape=jax.ShapeDtypeStruct((depth, rows, n6), F32),
        compiler_params=_cparams(2),
        name="ada_mod",
    )(c_all, ada_w, ada_b.reshape(depth, 1, n6))


def _mix0_kernel(x_ref, sh_ref, sc_ref, gt_ref, pool0_ref, conv0_ref, lru0_ref,
                 gain_ref, win_ref, poolw_ref, pscale_ref, convw_ref, convb_ref,
                 wri_ref, bri_ref, lam_ref, wout_ref,
                 x1_ref, pooln_ref, convn_ref, lrun_ref,
                 zp_ref, zc_ref, h_ref, *, tt, bb, start, n_t):
    t = pl.program_id(1)
    rows = tt * bb
    pw = zp_ref.shape[1]
    p0 = POOL_HALO * bb
    c0 = CONV_HALO * bb

    @pl.when(t == 0)
    def _():
        zp_ref[0:bb, :] = jnp.zeros((bb, pw), F32)
        zp_ref[bb:p0, :] = pool0_ref[0]
        zc_ref[0:c0 - (CONV_WIDTH - 1) * bb, :] = jnp.zeros((c0 - (CONV_WIDTH - 1) * bb, pw), F32)
        zc_ref[c0 - (CONV_WIDTH - 1) * bb:c0, :] = conv0_ref[0]
        h_ref[...] = lru0_ref[0]

    def per_row(v):
        return v if bb == 1 else jnp.concatenate([v] * tt, axis=0)

    x = x_ref[0]
    h = _rms(x, gain_ref[...]) * (1.0 + per_row(sc_ref[0])) + per_row(sh_ref[0])
    proj = _dot(h.astype(BF16), win_ref[...])
    u_pool = proj[:, :pw]
    u_x = proj[:, pw:2 * pw]
    u_g = proj[:, 2 * pw:]
    zp_ref[p0:p0 + rows, :] = u_pool
    zc_ref[c0:c0 + rows, :] = u_x

    if bb == 1:
        tix = lax.broadcasted_iota(jnp.int32, (rows, 1), 0)
    else:
        tix = jnp.concatenate([jnp.full((bb, 1), i, jnp.int32) for i in range(tt)], axis=0)
    pos = start + t * tt + tix

    gd = pw // len(POOL_WINDOWS)
    means = []
    if bb == 1 and POOL_WINDOWS == (2, 4, 8, 16):
        sw = zp_ref[0:p0 + rows, :]
        for g, w in enumerate(POOL_WINDOWS):
            sw = sw + pltpu.roll(sw, w // 2, 0)
            cnt = jnp.minimum(pos + 1, w).astype(F32)
            means.append(sw[p0:, :gd] / cnt)
            if g + 1 < len(POOL_WINDOWS):
                sw = sw[:, gd:]
    else:
        for g, w in enumerate(POOL_WINDOWS):
            cols = slice(g * gd, (g + 1) * gd)
            acc = zp_ref[p0:p0 + rows, cols]
            for i in range(1, w):
                acc = acc + zp_ref[p0 - i * bb:p0 - i * bb + rows, cols]
            cnt = jnp.minimum(pos + 1, w).astype(F32)
            means.append(acc / cnt)
    pooled = jnp.concatenate(means, axis=-1) - u_pool
    y_a = _dot(pooled.astype(BF16), poolw_ref[...]) * pscale_ref[...]

    xc = convb_ref[...]
    if bb == 1:
        z_al = zc_ref[0:c0 + rows, :]
        for k in range(CONV_WIDTH):
            back = CONV_WIDTH - 1 - k
            z_k = z_al if back == 0 else pltpu.roll(z_al, back, 0)
            xc = xc + z_k[c0:, :] * convw_ref[k:k + 1, :]
    else:
        for k in range(CONV_WIDTH):
            off = c0 - (CONV_WIDTH - 1 - k) * bb
            xc = xc + zc_ref[off:off + rows, :] * convw_ref[k:k + 1, :]
    pre = _dot(xc.astype(BF16), wri_ref[...]) + bri_ref[...]
    r = jax.nn.sigmoid(pre[:, :pw])
    gi = jax.nn.sigmoid(pre[:, pw:])
    lam = lam_ref[...]
    softplus_neg = jnp.maximum(-lam, 0.0) + jnp.log1p(jnp.exp(-jnp.abs(lam)))
    log_a = -LRU_C * r * softplus_neg
    a = jnp.exp(log_a)
    one_m = 1.0 - a * a
    b = jnp.where(one_m > 0.0, one_m * lax.rsqrt(one_m), 0.0) * gi * xc

    if bb == 1:
        rowi = lax.broadcasted_iota(jnp.int32, (rows, 1), 0)
        s = 1
        while s < rows:
            if s < 8:
                keep = rowi >= s
                b = jnp.where(keep, a * pltpu.roll(b, s, 0) + b, b)
                a = jnp.where(keep, a * pltpu.roll(a, s, 0), a)
            else:
                b = jnp.concatenate([b[:s], a[s:] * b[:rows - s] + b[s:]], axis=0)
                a = jnp.concatenate([a[:s], a[s:] * a[:rows - s]], axis=0)
            s *= 2
        hs = b + a * h_ref[...]
        h_ref[...] = hs[rows - 1:rows, :]
    else:
        hprev = h_ref[...]
        parts = []
        for i in range(tt):
            hprev = a[i * bb:(i + 1) * bb] * hprev + b[i * bb:(i + 1) * bb]
            parts.append(hprev)
        hs = jnp.concatenate(parts, axis=0)
        h_ref[...] = hprev
    y_b = hs * jax.nn.gelu(u_g)

    mix = _dot(y_a.astype(BF16), wout_ref[0:pw, :]) + _dot(y_b.astype(BF16), wout_ref[pw:2 * pw, :])
    x1_ref[0] = x + per_row(gt_ref[0]) * mix

    @pl.when(t == n_t - 1)
    def _():
        pooln_ref[0] = zp_ref[p0 + rows - (POOL_HALO - 1) * bb:p0 + rows, :]
        convn_ref[0] = zc_ref[c0 + rows - (CONV_WIDTH - 1) * bb:c0 + rows, :]
        lrun_ref[0] = h_ref[...]

    if n_t > 1:
        @pl.when(t < n_t - 1)
        def _():
            zp_ref[bb:p0, :] = zp_ref[rows + bb:rows + p0, :]
            zc_ref[c0 - (CONV_WIDTH - 1) * bb:c0, :] = zc_ref[c0 + rows - (CONV_WIDTH - 1) * bb:c0 + rows, :]


def _mix0(x, sh, sc, gt, pool0, conv0, lru0, wts, *, tt, bb, start):
    nbb, tot, d = x.shape
    n_t = tot // (tt * bb)
    rows = tt * bb
    pw = pool0.shape[-1]
    hp = (POOL_HALO - 1) * bb
    hc = (CONV_WIDTH - 1) * bb

    def const(a):
        nd = a.ndim
        return pl.BlockSpec(a.shape, lambda i, j: (0,) * nd)

    def per_b(n_rows, width):
        return pl.BlockSpec((1, n_rows, width), lambda i, j: (i, 0, 0))

    kern = functools.partial(_mix0_kernel, tt=tt, bb=bb, start=start, n_t=n_t)
    return pl.pallas_call(
        kern,
        grid=(nbb, n_t),
        in_specs=[pl.BlockSpec((1, rows, d), lambda i, j: (i, j, 0)),
                  per_b(bb, d), per_b(bb, d), per_b(bb, d),
                  per_b(hp, pw), per_b(hc, pw), per_b(bb, pw)] + [const(w) for w in wts],
        out_specs=[pl.BlockSpec((1, rows, d), lambda i, j: (i, j, 0)),
                   per_b(hp, pw), per_b(hc, pw), per_b(bb, pw)],
        out_shape=[jax.ShapeDtypeStruct((nbb, tot, d), F32),
                   jax.ShapeDtypeStruct((nbb, hp, pw), F32),
                   jax.ShapeDtypeStruct((nbb, hc, pw), F32),
                   jax.ShapeDtypeStruct((nbb, bb, pw), F32)],
        scratch_shapes=[pltpu.VMEM(((POOL_HALO + tt) * bb, pw), F32),
                        pltpu.VMEM(((CONV_HALO + tt) * bb, pw), F32),
                        pltpu.VMEM((bb, pw), F32)],
        compiler_params=_cparams(2),
        name="mix0",
    )(x, sh, sc, gt, pool0, conv0, lru0, *wts)


ROUTE_ROWS = 40
def _route_kernel(*refs, with_proj):
    if with_proj:
        (x_ref, o_ref, wout_ref, gtm_ref, shf_ref, scf_ref, gain_ref, wrh_ref, wrl_ref, br_ref,
         x1_ref, hf_ref, info_ref, meta_ref, cnt_ref, run_ref) = refs
    else:
        (x_ref, shf_ref, scf_ref, gain_ref, wrh_ref, wrl_ref, br_ref,
         hf_ref, info_ref, meta_ref, cnt_ref, run_ref) = refs

    @pl.when(pl.program_id(0) == 0)
    def _():
        run_ref[...] = jnp.zeros(run_ref.shape, F32)

    x = x_ref[...]
    if with_proj:
        x = x + gtm_ref[0] * _dot(o_ref[...], wout_ref[...])
        x1_ref[...] = x
    hf = _rms(x, gain_ref[...]) * (1.0 + scf_ref[0]) + shf_ref[0]
    hf_ref[...] = hf

    hi = hf.astype(BF16)
    lo = (hf - hi.astype(F32)).astype(BF16)
    logits = _dot(hi, wrh_ref[...]) + _dot(lo, wrh_ref[...]) + _dot(hi, wrl_ref[...]) + br_ref[...]

    tq = logits.shape[0]
    lt = logits.T[:ROUTE_ROWS]
    row = lax.broadcasted_iota(jnp.int32, (ROUTE_ROWS, tq), 0).astype(F32)
    neg = -jnp.inf
    big = float(ROUTE_ROWS)
    is_g = row < MOE_GROUPS
    gl = jnp.where(is_g, lt, neg)
    mg = jnp.max(gl, axis=0, keepdims=True)
    gidx = jnp.min(jnp.where(gl == mg, row, big), axis=0, keepdims=True)
    p_group = 1.0 / jnp.sum(jnp.where(is_g, jnp.exp(gl - mg), 0.0), axis=0, keepdims=True)
    first = MOE_GROUPS + gidx * EXPERTS_PER_GROUP
    el = jnp.where((row >= first) & (row < first + EXPERTS_PER_GROUP), lt, neg)
    v1 = jnp.max(el, axis=0, keepdims=True)
    i1 = jnp.min(jnp.where(el == v1, row, big), axis=0, keepdims=True)
    el2 = jnp.where(row == i1, neg, el)
    v2 = jnp.max(el2, axis=0, keepdims=True)
    i2 = jnp.min(jnp.where(el2 == v2, row, big), axis=0, keepdims=True)
    ex = jnp.exp(v2 - v1)
    g1 = p_group / (1.0 + ex)
    g2 = p_group * ex / (1.0 + ex)
    e1 = i1 - MOE_GROUPS
    e2 = i2 - MOE_GROUPS

    erow = lax.broadcasted_iota(jnp.int32, (N_EXPERTS, tq), 0).astype(F32)
    oh1 = jnp.where(erow == e1, 1.0, 0.0)
    oh2 = jnp.where(erow == e2, 1.0, 0.0)
    oh = oh1 + oh2
    ri = lax.broadcasted_iota(jnp.int32, (tq, tq), 0)
    ci = lax.broadcasted_iota(jnp.int32, (tq, tq), 1)
    earlier = jnp.where(ri < ci, 1.0, 0.0).astype(BF16)
    before = _dot(oh.astype(BF16), earlier) + run_ref[...]
    r1 = jnp.sum(before * oh1, axis=0, keepdims=True)
    r2 = jnp.sum(before * oh2, axis=0, keepdims=True)
    run_ref[...] = run_ref[...] + jnp.sum(oh, axis=1, keepdims=True)
    cnt_ref[...] = jnp.broadcast_to(run_ref[...], cnt_ref.shape)

    meta = jnp.concatenate([e1, e2, g1, g2, r1, r2, jnp.zeros((2, tq), F32)], axis=0)
    meta_ref[0] = meta
    info_ref[...] = jnp.concatenate([meta, jnp.zeros((LANES - 8, tq), F32)], axis=0).T


def _route(x, shf, scf, gain, wrh, wrl, br, *, tile, rows_per_mod, proj=None):
    n, d = x.shape
    nt = n // tile
    mrows = shf.shape[1]
    mod_spec = pl.BlockSpec((1, mrows, d), lambda i: ((i * tile) // rows_per_mod, 0, 0))
    row_spec = pl.BlockSpec((tile, d), lambda i: (i, 0))

    def const(a):
        nd = a.ndim
        return pl.BlockSpec(a.shape, lambda i: (0,) * nd)

    info_spec = pl.BlockSpec((tile, LANES), lambda i: (i, 0))
    meta_spec = pl.BlockSpec((1, 8, tile), lambda i: (i, 0, 0))
    cnt_spec = pl.BlockSpec((N_EXPERTS, LANES), lambda i: (0, 0))
    outs_shape = [jax.ShapeDtypeStruct((n, d), F32), jax.ShapeDtypeStruct((n, LANES), F32),
                  jax.ShapeDtypeStruct((nt, 8, tile), F32), jax.ShapeDtypeStruct((N_EXPERTS, LANES), F32)]
    outs_spec = [row_spec, info_spec, meta_spec, cnt_spec]
    if proj is None:
        ins = [x, shf, scf, gain, wrh, wrl, br]
        in_specs = [row_spec, mod_spec, mod_spec, const(gain), const(wrh), const(wrl), const(br)]
    else:
        o, wout, gtm = proj
        ins = [x, o, wout, gtm, shf, scf, gain, wrh, wrl, br]
        in_specs = [row_spec, pl.BlockSpec((tile, o.shape[1]), lambda i: (i, 0)), const(wout), mod_spec,
                    mod_spec, mod_spec, const(gain), const(wrh), const(wrl), const(br)]
        outs_shape = [jax.ShapeDtypeStruct((n, d), F32)] + outs_shape
        outs_spec = [row_spec] + outs_spec
    return pl.pallas_call(
        functools.partial(_route_kernel, with_proj=proj is not None),
        grid=(nt,),
        in_specs=in_specs,
        out_specs=outs_spec,
        out_shape=outs_shape,
        scratch_shapes=[pltpu.VMEM((N_EXPERTS, 1), F32)],
        compiler_params=_cparams(1),
        name="route",
    )(*ins)


def _dispatch_kernel(slot_ref, hf_ref, xs_ref, sem, *, tile):
    def issue(g, c):
        for u in range(ROW_DMA_UNROLL):
            r = g * ROW_DMA_UNROLL + u
            for k in range(TOP_K):
                s = slot_ref[0, k, r]
                pltpu.make_async_copy(hf_ref.at[pl.ds(r, 1), :], xs_ref.at[pl.ds(s, 1), :], sem).start()
        return c

    lax.fori_loop(0, tile // ROW_DMA_UNROLL, issue, 0)
    for k in range(TOP_K):
        pltpu.make_async_copy(hf_ref, xs_ref.at[pl.ds(0, tile), :], sem).wait()


def _dispatch(hf, slots, *, tile):
    n, d = hf.shape
    nt = n // tile
    return pl.pallas_call(
        functools.partial(_dispatch_kernel, tile=tile),
        grid=(nt,),
        in_specs=[pl.BlockSpec((1, TOP_K, tile), lambda i: (i, 0, 0), memory_space=pltpu.SMEM),
                  pl.BlockSpec((tile, d), lambda i: (i, 0))],
        out_specs=pl.BlockSpec(memory_space=pl.ANY),
        out_shape=jax.ShapeDtypeStruct((TOP_K * n, d), F32),
        scratch_shapes=[pltpu.SemaphoreType.DMA(())],
        compiler_params=_cparams(1),
        name="moe_dispatch",
    )(slots, hf)


def _gmm_kernel(wt_ref, we_ref, wlo_ref, whi_ref, xs_ref, wg_ref, wu_ref, wd_ref, y_ref,
                wgb_ref, wub_ref, wdb_ref):
    w = pl.program_id(0)
    prev = jnp.maximum(w - 1, 0)
    new_expert = (w == 0) | (we_ref[w] != we_ref[prev])
    new_tile = (w == 0) | (wt_ref[w] != wt_ref[prev])

    @pl.when(new_expert)
    def _():
        wgb_ref[...] = wg_ref[0, 0].astype(BF16)
        wub_ref[...] = wu_ref[0, 0].astype(BF16)
        wdb_ref[...] = wd_ref[0, 0].astype(BF16)

    @pl.when(new_tile)
    def _():
        y_ref[...] = jnp.zeros(y_ref.shape, F32)

    lo = wlo_ref[w]
    hi = whi_ref[w]

    @pl.when(hi > lo)
    def _():
        x = xs_ref[...].astype(BF16)
        g = _dot(x, wgb_ref[...])
        u = _dot(x, wub_ref[...])
        yv = _dot((_silu(g) * u).astype(BF16), wdb_ref[...])
        row = lax.broadcasted_iota(jnp.int32, (x.shape[0], 1), 0)
        y_ref[...] = y_ref[...] + jnp.where((row >= lo) & (row < hi), yv, 0.0)


def _gmm(xs, work, w_gate, w_up, w_down, *, tile, layer):
    m, d = xs.shape
    ff = w_gate.shape[-1]
    n_work = work[0].shape[0]
    grid_spec = pltpu.PrefetchScalarGridSpec(
        num_scalar_prefetch=4,
        grid=(n_work,),
        in_specs=[pl.BlockSpec((tile, d), lambda w, wt, we, wlo, whi: (wt[w], 0)),
                  pl.BlockSpec((1, 1, d, ff), lambda w, wt, we, wlo, whi: (layer, we[w], 0, 0)),
                  pl.BlockSpec((1, 1, d, ff), lambda w, wt, we, wlo, whi: (layer, we[w], 0, 0)),
                  pl.BlockSpec((1, 1, ff, d), lambda w, wt, we, wlo, whi: (layer, we[w], 0, 0))],
        out_specs=pl.BlockSpec((tile, d), lambda w, wt, we, wlo, whi: (wt[w], 0)),
        scratch_shapes=[pltpu.VMEM((d, ff), BF16), pltpu.VMEM((d, ff), BF16), pltpu.VMEM((ff, d), BF16)],
    )
    return pl.pallas_call(
        _gmm_kernel,
        grid_spec=grid_spec,
        out_shape=jax.ShapeDtypeStruct((m, d), F32),
        compiler_params=_cparams(1),
        name="moe_gmm",
    )(*work, xs, w_gate, w_up, w_down)


def _work_list(counts, n_slots, tile):
    n_tiles = n_slots // tile
    n_work = n_tiles + N_EXPERTS - 1
    ends = jnp.cumsum(counts)
    starts = ends - counts
    first_tile = starts // tile
    last_tile = jnp.maximum(ends - 1, 0) // tile
    n_items = jnp.where(counts > 0, last_tile - first_tile + 1, 0)
    item_end = jnp.cumsum(n_items)
    item_start = item_end - n_items
    w = jnp.arange(n_work, dtype=jnp.int32)
    used = w < item_end[-1]
    e = jnp.minimum(jnp.sum(w[:, None] >= item_end[None, :], axis=1), N_EXPERTS - 1).astype(jnp.int32)
    is_e = e[:, None] == jnp.arange(N_EXPERTS, dtype=jnp.int32)[None, :]

    def of_e(v):
        return jnp.sum(jnp.where(is_e, v[None, :], 0), axis=1)

    t = of_e(first_tile) + (w - of_e(item_start))
    lo = jnp.maximum(of_e(starts), t * tile) - t * tile
    hi = jnp.minimum(of_e(ends), (t + 1) * tile) - t * tile
    last_e = jnp.max(jnp.where(counts > 0, jnp.arange(N_EXPERTS), 0)).astype(jnp.int32)
    wt = jnp.where(used, t, n_tiles - 1).astype(jnp.int32)
    we = jnp.where(used, e, last_e).astype(jnp.int32)
    wlo = jnp.where(used, lo, 0).astype(jnp.int32)
    whi = jnp.where(used, hi, 0).astype(jnp.int32)
    return (wt, we, wlo, whi), starts


def _combine_kernel(*refs, tile, final):
    if final:
        slot_ref, nslot_ref, x_ref, gt_ref, info_ref, fg_ref, y_ref, out_ref, ybuf, sem = refs
    else:
        slot_ref, nslot_ref, x_ref, gt_ref, info_ref, y_ref, out_ref, ybuf, sem = refs
    i = pl.program_id(0)
    cur = i % 2

    def gather(slots_ref, buf):
        def issue(g, c):
            for u in range(ROW_DMA_UNROLL):
                r = g * ROW_DMA_UNROLL + u
                for k in range(TOP_K):
                    s = slots_ref[0, k, r]
                    pltpu.make_async_copy(y_ref.at[pl.ds(s, 1), :], ybuf.at[buf, k, pl.ds(r, 1), :],
                                          sem.at[buf]).start()
            return c

        lax.fori_loop(0, tile // ROW_DMA_UNROLL, issue, 0)

    @pl.when(i == 0)
    def _():
        gather(slot_ref, cur)

    @pl.when(i + 1 < pl.num_programs(0))
    def _():
        gather(nslot_ref, 1 - cur)

    for k in range(TOP_K):
        pltpu.make_async_copy(y_ref.at[pl.ds(0, tile), :], ybuf.at[cur, k], sem.at[cur]).wait()

    info = info_ref[...]
    ffn = info[:, 2:3] * ybuf[cur, 0] + info[:, 3:4] * ybuf[cur, 1]
    out = x_ref[...] + gt_ref[0] * ffn
    if final:
        out = _rms(out, fg_ref[...])
    out_ref[...] = out


def _combine(x, gt, info, y, slots, *, tile, rows_per_mod, final_gain=None):
    n, d = x.shape
    nt = n // tile
    mrows = gt.shape[1]
    final = final_gain is not None
    slots3 = slots
    in_specs = [pl.BlockSpec((1, TOP_K, tile), lambda i: (i, 0, 0), memory_space=pltpu.SMEM),
                pl.BlockSpec((1, TOP_K, tile), lambda i: (jnp.minimum(i + 1, nt - 1), 0, 0),
                             memory_space=pltpu.SMEM),
                pl.BlockSpec((tile, d), lambda i: (i, 0)),
                pl.BlockSpec((1, mrows, d), lambda i: ((i * tile) // rows_per_mod, 0, 0)),
                pl.BlockSpec((tile, LANES), lambda i: (i, 0))]
    ins = [slots3, slots3, x, gt, info]
    if final:
        in_specs.append(pl.BlockSpec(final_gain.shape, lambda i: (0, 0)))
        ins.append(final_gain)
    in_specs.append(pl.BlockSpec(memory_space=pl.ANY))
    ins.append(y)
    return pl.pallas_call(
        functools.partial(_combine_kernel, tile=tile, final=final),
        grid=(nt,),
        in_specs=in_specs,
        out_specs=pl.BlockSpec((tile, d), lambda i: (i, 0)),
        out_shape=jax.ShapeDtypeStruct((n, d), F32),
        scratch_shapes=[pltpu.VMEM((2, TOP_K, tile, d), F32), pltpu.SemaphoreType.DMA((2,))],
        compiler_params=_cparams(1),
        name="moe_combine",
    )(*ins)


def _moe(x, mods, rw, experts, *, layer, tile, gmm_tile, rows_per_mod, proj=None, final_gain=None):
    shf, scf, gtf = mods
    gain, wrh, wrl, br = rw
    res = _route(x, shf, scf, gain, wrh, wrl, br, tile=tile, rows_per_mod=rows_per_mod, proj=proj)
    if proj is not None:
        x, hf, info, meta, cnt = res
    else:
        hf, info, meta, cnt = res
    n = x.shape[0]
    counts = cnt[:, 0].astype(jnp.int32)
    work, starts = _work_list(counts, TOP_K * n, gmm_tile)
    eid = meta[:, 0:TOP_K, :].astype(jnp.int32)
    rank = meta[:, 4:4 + TOP_K, :].astype(jnp.int32)
    start_of = jnp.zeros(eid.shape, jnp.int32)
    for e in range(N_EXPERTS):
        start_of = jnp.where(eid == e, starts[e].astype(jnp.int32), start_of)
    slots = start_of + rank
    xs = _dispatch(hf, slots, tile=tile)
    y = _gmm(xs, work, *experts, tile=gmm_tile, layer=layer)
    return _combine(x, gtf, info, y, slots, tile=tile, rows_per_mod=rows_per_mod, final_gain=final_gain)


def _rope_turn(blk, c, s):
    return blk * c + pltpu.roll(blk, LANES - QK_ROPE, 1) * s


def _mla_proj_kernel(*refs, sample):
    if sample:
        (x_ref, sh_ref, sc_ref, gain_ref, win_ref, qg_ref, wuq_ref, kvg_ref, c_ref, s_ref,
         ckv_ref, kr_ref, q_ref) = refs
    else:
        (x_ref, sh_ref, sc_ref, gain_ref, win_ref, qg_ref, wuq_ref, kvg_ref, c_ref, s_ref, wuk_ref, wuvt_ref,
         vonet_ref, ct_ref, st_ref, ckv_ref, kr_ref, q_ref, k_ref, v_ref) = refs
    q_lora = qg_ref.shape[1]
    kv_lora = kvg_ref.shape[1]
    h = _rms(x_ref[...], gain_ref[...]) * (1.0 + sc_ref[0]) + sh_ref[0]
    proj = _dot(h.astype(BF16), win_ref[...])
    qn = _rms(proj[:, :q_lora], qg_ref[...])
    ckv = _rms(proj[:, q_lora:q_lora + kv_lora], kvg_ref[...])
    c = c_ref[...]
    s = s_ref[...]
    kf = _rope_turn(proj[:, q_lora + kv_lora:], c, s)
    ckv_ref[...] = ckv
    kr_ref[...] = kf[:, QK_NOPE:QK_NOPE + QK_ROPE]
    qn_b = qn.astype(BF16)
    if sample:
        q = _dot(qn_b, wuq_ref[...])
        for hh in range(MLA_HEADS):
            cols = slice(hh * HEAD_PAD, (hh + 1) * HEAD_PAD)
            q_ref[:, cols] = _rope_turn(q[:, cols], c, s) * SM_SCALE
    else:
        ckv_b = ckv.astype(BF16)
        kn = _dot(ckv_b, wuk_ref[...])
        qt = _dot_nt(wuq_ref[...], qn_b)
        v_ref[0] = (_dot_nt(wuvt_ref[...], ckv_b) + vonet_ref[...]).astype(BF16)
        ct = ct_ref[...]
        st = st_ref[...]
        for hh in range(MLA_HEADS):
            cols = slice(hh * HEAD_PAD, (hh + 1) * HEAD_PAD)
            k_ref[:, cols] = (kn[:, cols] + kf).astype(BF16)
            blk = qt[cols, :]
            turned = jnp.concatenate([blk[QK_ROPE:], blk[:QK_ROPE]], axis=0)
            q_ref[0, cols, :] = ((blk * ct + turned * st) * (SM_SCALE * LOG2E)).astype(BF16)


def _mla_proj(x, sh, sc, gain, win, qg, wuq, kvg, ctab, stab, *, tile, rows_per_mod, tab_tiles, kv_w=None):
    n, d = x.shape
    nt = n // tile
    mrows = sh.shape[1]
    sample = kv_w is None
    kv_lora = kvg.shape[1]

    def const(a):
        nd = a.ndim
        return pl.BlockSpec(a.shape, lambda i: (0,) * nd)

    row = lambda width: pl.BlockSpec((tile, width), lambda i: (i, 0))
    mod_spec = pl.BlockSpec((1, mrows, d), lambda i: ((i * tile) // rows_per_mod, 0, 0))
    tab_spec = pl.BlockSpec((tile, LANES), lambda i: (i % tab_tiles, 0))
    ins = [x, sh, sc, gain, win, qg, wuq, kvg, ctab, stab]
    in_specs = [row(d), mod_spec, mod_spec, const(gain), const(win), const(qg), const(wuq), const(kvg),
                tab_spec, tab_spec]
    qw = MLA_HEADS * HEAD_PAD
    out_shape = [jax.ShapeDtypeStruct((n, kv_lora), F32), jax.ShapeDtypeStruct((n, QK_ROPE), F32)]
    out_specs = [row(kv_lora), row(QK_ROPE)]
    if sample:
        out_shape.append(jax.ShapeDtypeStruct((n, qw), F32))
        out_specs.append(row(qw))
    else:
        wuk, wuvt, vonet, ctab_t, stab_t = kv_w
        seq_len = tab_tiles * tile
        tab_t_spec = pl.BlockSpec((LANES, tile), lambda i: (0, i % tab_tiles))
        by_seq = pl.BlockSpec((1, qw, tile), lambda i: (i // tab_tiles, 0, i % tab_tiles))
        ins += [wuk, wuvt, vonet, ctab_t, stab_t]
        in_specs += [const(wuk), const(wuvt), const(vonet), tab_t_spec, tab_t_spec]
        out_shape += [jax.ShapeDtypeStruct((n // seq_len, qw, seq_len), BF16), jax.ShapeDtypeStruct((n, qw), BF16),
                      jax.ShapeDtypeStruct((n // seq_len, qw, seq_len), BF16)]
        out_specs += [by_seq, row(qw), by_seq]
    return pl.pallas_call(
        functools.partial(_mla_proj_kernel, sample=sample),
        grid=(nt,),
        in_specs=in_specs,
        out_specs=out_specs,
        out_shape=out_shape,
        compiler_params=_cparams(1),
        name="mla_proj",
    )(*ins)


def _attn_kernel(q_ref, k_ref, v_ref, o_ref, sa_ref, sb_ref, m_ref, acc_ref, *, tq, tk):
    i = pl.program_id(2)
    heads = range(2)

    def cols(hh):
        return slice(hh * HEAD_PAD, (hh + 1) * HEAD_PAD)

    def scores(j, dst_ref):
        keys = pl.ds(pl.multiple_of(j * tk, tk), tk)
        for hh in heads:
            dst_ref[hh] = _dot(k_ref[0, keys, cols(hh)], q_ref[0, cols(hh), :])

    def absorb(j, src_ref, masked):
        keys = pl.ds(pl.multiple_of(j * tk, tk), tk)
        if masked:
            keep = lax.broadcasted_iota(jnp.int32, (tk, tq), 0) <= lax.broadcasted_iota(jnp.int32, (tk, tq), 1)
        for hh in heads:
            s = src_ref[hh]
            if masked:
                s = jnp.where(keep, s, -jnp.inf)
            m = m_ref[hh]
            m_new = jnp.maximum(m, jnp.max(s, axis=0, keepdims=True))
            p = jnp.exp2(s - m_new)
            acc_ref[hh] = jnp.exp2(m - m_new) * acc_ref[hh] + _dot(v_ref[0, cols(hh), keys], p.astype(BF16))
            m_ref[hh] = m_new

    m_ref[...] = jnp.full(m_ref.shape, -jnp.inf, F32)
    acc_ref[...] = jnp.zeros(acc_ref.shape, F32)
    scores(0, sa_ref)

    def pair(jj, c):
        j = 2 * jj
        scores(j + 1, sb_ref)
        absorb(j, sa_ref, False)
        scores(j + 2, sa_ref)
        absorb(j + 1, sb_ref, False)
        return c

    lax.fori_loop(0, i // 2, pair, 0)

    @pl.when(i % 2 == 1)
    def _():
        scores(i, sb_ref)
        absorb(i - 1, sa_ref, False)
        absorb(i, sb_ref, True)

    @pl.when(i % 2 == 0)
    def _():
        absorb(i, sa_ref, True)

    out_t = jnp.concatenate([acc_ref[hh][:V_DIM] / acc_ref[hh][V_DIM:V_DIM + 1] for hh in heads], axis=0)
    o_ref[0] = out_t.T.astype(o_ref.dtype)


def _attention(q_t, k, v_t, *, tq, tk):
    assert tq == tk, "one diagonal key tile per query tile"
    b, s, _ = k.shape
    pairs = MLA_HEADS // 2
    return pl.pallas_call(
        functools.partial(_attn_kernel, tq=tq, tk=tk),
        grid=(b, pairs, s // tq),
        in_specs=[pl.BlockSpec((1, 2 * HEAD_PAD, tq), lambda bi, p, i: (bi, p, i)),
                  pl.BlockSpec((1, s, 2 * HEAD_PAD), lambda bi, p, i: (bi, 0, p)),
                  pl.BlockSpec((1, 2 * HEAD_PAD, s), lambda bi, p, i: (bi, p, 0))],
        out_specs=pl.BlockSpec((1, tq, 2 * V_DIM), lambda bi, p, i: (bi, i, p)),
        out_shape=jax.ShapeDtypeStruct((b, s, MLA_HEADS * V_DIM), BF16),
        scratch_shapes=[pltpu.VMEM((2, tk, tq), F32), pltpu.VMEM((2, tk, tq), F32),
                        pltpu.VMEM((2, 1, tq), F32), pltpu.VMEM((2, HEAD_PAD, tq), F32)],
        compiler_params=_cparams(3),
        name="prompt_attn",
    )(q_t, k, v_t)


def _qabs_kernel(q_ref, m_ref, o_ref):
    o_ref[0] = _dot(q_ref[...].astype(BF16), m_ref[0])


def _qabs(q, mats):
    n = q.shape[0]
    width = mats.shape[-1]
    return pl.pallas_call(
        _qabs_kernel,
        grid=(MLA_HEADS,),
        in_specs=[pl.BlockSpec((n, HEAD_PAD), lambda h: (0, h)),
                  pl.BlockSpec((1, HEAD_PAD, width), lambda h: (h, 0, 0))],
        out_specs=pl.BlockSpec((1, n, width), lambda h: (h, 0, 0)),
        out_shape=jax.ShapeDtypeStruct((MLA_HEADS, n, width), F32),
        compiler_params=_cparams(1),
        name="sample_qabs",
    )(q, mats)


def _paged_attn_kernel(pt_ref, q_ref, cn_ref, rn_ref, ck_hbm, kr_hbm, o_ref,
                       ckbuf, krbuf, m_ref, l_ref, acc_ref, sem, *, group, pages, page, n_chunks, t_new, t_pad):
    bg = pl.program_id(0)
    c = pl.program_id(1)
    step = bg * n_chunks + c
    n_steps = pl.num_programs(0) * n_chunks
    slot = step % PAGE_RING
    kv_lora = ckbuf.shape[-1]

    def start_chunk(st):
        bgi, ci, sl = st // n_chunks, st % n_chunks, st % PAGE_RING
        for g in range(group):
            for p in range(pages):
                phys = pt_ref[bgi * group + g, ci * pages + p]
                pltpu.make_async_copy(ck_hbm.at[phys], ckbuf.at[sl, g, pl.ds(p * page, page), :],
                                      sem.at[0, sl]).start()
                pltpu.make_async_copy(kr_hbm.at[phys], krbuf.at[sl, g, p], sem.at[1, sl]).start()

    @pl.when(step == 0)
    def _():
        for ahead in range(PAGE_RING - 1):
            @pl.when(ahead < n_steps)
            def _():
                start_chunk(step + ahead)

    pltpu.make_async_copy(ckbuf.at[slot], ckbuf.at[slot], sem.at[0, slot]).wait()
    pltpu.make_async_copy(krbuf.at[slot], krbuf.at[slot], sem.at[1, slot]).wait()

    @pl.when(c == 0)
    def _():
        m_ref[...] = jnp.full(m_ref.shape, -jnp.inf, F32)
        l_ref[...] = jnp.zeros(l_ref.shape, F32)
        acc_ref[...] = jnp.zeros(acc_ref.shape, F32)

    def absorb(g, s, values):
        m = m_ref[g]
        m_new = jnp.maximum(m, jnp.max(s, axis=1, keepdims=True))
        alpha = jnp.exp(m - m_new)
        p = jnp.exp(s - m_new)
        l_ref[g] = alpha * l_ref[g] + jnp.sum(p, axis=1, keepdims=True)
        acc_ref[g] = alpha * acc_ref[g] + _dot(p.astype(BF16), values)
        m_ref[g] = m_new

    cks, scores = [], []
    for g in range(group):
        qa = q_ref[g]
        q_lat = qa[:, :kv_lora].astype(BF16)
        q_rope = qa[:, kv_lora:kv_lora + QK_ROPE].astype(BF16)
        ck = ckbuf[slot, g].astype(BF16)
        kr_t = jnp.concatenate([krbuf[slot, g, p] for p in range(pages)], axis=1).astype(BF16)
        cks.append(ck)
        scores.append(_dot_nt(q_lat, ck) + _dot(q_rope, kr_t))

    nxt = step + PAGE_RING - 1

    @pl.when(nxt < n_steps)
    def _():
        start_chunk(nxt)

    for g in range(group):
        absorb(g, scores[g], cks[g])

    @pl.when(c == n_chunks - 1)
    def _():
        rows = q_ref.shape[1]
        t_q = lax.broadcasted_iota(jnp.int32, (rows, t_pad), 0) % t_new
        t_k = lax.broadcasted_iota(jnp.int32, (rows, t_pad), 1)
        for g in range(group):
            qa = q_ref[g]
            q_lat = qa[:, :kv_lora].astype(BF16)
            q_rope = qa[:, kv_lora:kv_lora + QK_ROPE].astype(BF16)
            cn = cn_ref[g].astype(BF16)
            s = _dot_nt(q_lat, cn) + _dot_nt(q_rope, rn_ref[g].astype(BF16))
            absorb(g, jnp.where(t_k <= t_q, s, -jnp.inf), cn)
            o_ref[g] = acc_ref[g] / l_ref[g]


def _paged_attn(page_table, qabs, ckv_new, kr_new, cache_ck, cache_kr_t, *, group, pages, t_new):
    b, rows, width = qabs.shape
    n_pages = page_table.shape[1]
    n_chunks = n_pages // pages
    page, kv_lora = cache_ck.shape[1:]
    t_pad = ckv_new.shape[1]
    grid_spec = pltpu.PrefetchScalarGridSpec(
        num_scalar_prefetch=1,
        grid=(b // group, n_chunks),
        in_specs=[pl.BlockSpec((group, rows, width), lambda bi, ci, pt: (bi, 0, 0)),
                  pl.BlockSpec((group, t_pad, kv_lora), lambda bi, ci, pt: (bi, 0, 0)),
                  pl.BlockSpec((group, t_pad, QK_ROPE), lambda bi, ci, pt: (bi, 0, 0)),
                  pl.BlockSpec(memory_space=pl.ANY),
                  pl.BlockSpec(memory_space=pl.ANY)],
        out_specs=pl.BlockSpec((group, rows, kv_lora), lambda bi, ci, pt: (bi, 0, 0)),
        scratch_shapes=[pltpu.VMEM((PAGE_RING, group, pages * page, kv_lora), F32),
                        pltpu.VMEM((PAGE_RING, group, pages, QK_ROPE, page), F32),
                        pltpu.VMEM((group, rows, 1), F32), pltpu.VMEM((group, rows, 1), F32),
                        pltpu.VMEM((group, rows, kv_lora), F32),
                        pltpu.SemaphoreType.DMA((2, PAGE_RING))],
    )
    return pl.pallas_call(
        functools.partial(_paged_attn_kernel, group=group, pages=pages, page=page, n_chunks=n_chunks, t_new=t_new,
                          t_pad=t_pad),
        grid_spec=grid_spec,
        out_shape=jax.ShapeDtypeStruct((b, rows, kv_lora), F32),
        compiler_params=_cparams(2),
        name="paged_attn",
    )(page_table, qabs, ckv_new, kr_new, cache_ck, cache_kr_t)


def _vup_kernel(o_ref, w_ref, out_ref):
    lat = jnp.concatenate([o_ref[0], o_ref[1]], axis=-1).astype(BF16)
    out_ref[...] = _dot(lat, w_ref[0]).astype(out_ref.dtype)


def _vup(o_lat, w_pairs):
    h, n, c = o_lat.shape
    return pl.pallas_call(
        _vup_kernel,
        grid=(h // 2,),
        in_specs=[pl.BlockSpec((2, n, c), lambda p: (p, 0, 0)),
                  pl.BlockSpec((1, 2 * c, 2 * V_DIM), lambda p: (p, 0, 0))],
        out_specs=pl.BlockSpec((n, 2 * V_DIM), lambda p: (0, p)),
        out_shape=jax.ShapeDtypeStruct((n, h * V_DIM), BF16),
        compiler_params=_cparams(1),
        name="sample_vup",
    )(o_lat, w_pairs)


def _block_diag(w):
    g, a, b = w.shape
    out = jnp.zeros((g * a, g * b), w.dtype)
    for i in range(g):
        out = out.at[i * a:(i + 1) * a, i * b:(i + 1) * b].set(w[i])
    return out


def _swap_halves(w):
    half = w.shape[-1] // 2
    return jnp.concatenate([w[..., half:], w[..., :half]], axis=-1)


def _rope_tables(pos):
    half = QK_ROPE // 2
    inv_freq = ROPE_THETA ** (-jnp.arange(half, dtype=F32) / half)
    ang = pos.astype(F32)[:, None] * inv_freq[None, :]
    cos, sin = jnp.cos(ang), jnp.sin(ang)
    n = pos.shape[0]
    ctab = jnp.concatenate([jnp.ones((n, QK_NOPE), F32), cos, cos, jnp.zeros((n, QK_ROPE), F32)], axis=1)
    stab = jnp.concatenate([jnp.zeros((n, QK_NOPE), F32), -sin, sin, jnp.zeros((n, QK_ROPE), F32)], axis=1)
    return ctab, stab


def _router_weights(w_group, b_group, w_expert, b_expert):
    d = w_group.shape[0]
    we = jnp.transpose(w_expert, (1, 0, 2)).reshape(d, N_EXPERTS)
    w = jnp.concatenate([w_group, we, jnp.zeros((d, LANES - MOE_GROUPS - N_EXPERTS), F32)], axis=1)
    bias = jnp.concatenate([b_group, b_expert.reshape(-1), jnp.zeros((LANES - MOE_GROUPS - N_EXPERTS,), F32)])
    hi = w.astype(BF16)
    lo = (w - hi.astype(F32)).astype(BF16)
    return hi, lo, bias.reshape(1, LANES)


def _pick(n, pref):
    t = min(n, pref)
    while n % t:
        t //= 2
    return t


def kernel(x_prompt, x_sample, state_pool, state_conv, state_lru, cache_kv_latent, cache_k_rope, page_table,
           c_prompt, c_sample, ada_w, ada_b, norm_mix, norm_ffn, ab_w_in, ab_w_out, pool_w, pool_scale,
           conv_w, conv_b, lru_w_r, lru_b_r, lru_w_i, lru_b_i, lru_lambda,
           mla_w_in, mla_q_norm, mla_w_uq, mla_kv_norm, mla_w_uk, mla_w_uv, mla_w_out,
           router_w_group, router_b_group, router_w_expert, router_b_expert,
           moe_w_gate, moe_w_up, moe_w_down, final_norm):
    bp, seq, d = x_prompt.shape
    bs, t_new, _ = x_sample.shape
    depth = ada_w.shape[0]
    pw = pool_scale.shape[-1]
    page = cache_kv_latent.shape[2]
    past_len = page_table.shape[1] * page
    q_lora = mla_q_norm.shape[-1]
    kv_lora = mla_kv_norm.shape[-1]

    n_c = bp + bs
    n_c_pad = -(-n_c // 8) * 8
    c_all = jnp.concatenate([c_prompt, c_sample, jnp.zeros((n_c_pad - n_c, d), F32)], axis=0)
    mod = _ada(c_all, ada_w, ada_b)

    def mods(layer, lo, hi):
        return [mod[layer, lo:hi, k * d:(k + 1) * d] for k in range(6)]

    n_p = bp * seq
    n_s = bs * t_new
    tile_p = _pick(seq, 512)
    tile_s = _pick(n_s, 512)
    gmm_tile_p = _pick(TOP_K * n_p, 512)
    gmm_tile_s = _pick(TOP_K * n_s, 256)

    xp = x_prompt.reshape(n_p, d)
    xs = x_sample.reshape(n_s, d)
    outs = {}

    for layer in range(depth):
        sh_m, sc_m, gt_m, sh_f, sc_f, gt_f = mods(layer, 0, bp)
        sh_ms, sc_ms, gt_ms, sh_fs, sc_fs, gt_fs = mods(layer, bp, bp + bs)
        rep = lambda v: jnp.repeat(v, t_new, axis=0)[None]
        per_seq = lambda v: v[:, None, :]
        rw = (norm_ffn[layer].reshape(1, d),) + _router_weights(
            router_w_group[layer], router_b_group[layer], router_w_expert[layer], router_b_expert[layer])
        experts = (moe_w_gate, moe_w_up, moe_w_down)
        final_gain = final_norm.reshape(1, d) if layer == depth - 1 else None
        ffn_mods_p = (per_seq(sh_f), per_seq(sc_f), per_seq(gt_f))
        ffn_mods_s = (rep(sh_fs), rep(sc_fs), rep(gt_fs))

        if layer % 2 == 0:
            e = layer // 2
            wts = (norm_mix[layer].reshape(1, d), ab_w_in[e].astype(BF16), _block_diag(pool_w[e]).astype(BF16),
                   pool_scale[e].reshape(1, pw), conv_w[e], conv_b[e].reshape(1, pw),
                   jnp.concatenate([_block_diag(lru_w_r[e]), _block_diag(lru_w_i[e])], axis=1).astype(BF16),
                   jnp.concatenate([lru_b_r[e], lru_b_i[e]]).reshape(1, 2 * pw),
                   lru_lambda[e].reshape(1, pw), ab_w_out[e].astype(BF16))
            x1p, pool_p, conv_p, lru_p = _mix0(
                xp.reshape(bp, seq, d), per_seq(sh_m), per_seq(sc_m), per_seq(gt_m),
                jnp.zeros((bp, POOL_HALO - 1, pw), F32), jnp.zeros((bp, CONV_WIDTH - 1, pw), F32),
                jnp.zeros((bp, 1, pw), F32), wts, tt=tile_p, bb=1, start=0)
            tm = lambda a: jnp.swapaxes(a, 0, 1).reshape(1, -1, a.shape[-1])
            x1s, pool_s, conv_s, lru_s = _mix0(
                tm(xs.reshape(bs, t_new, d)), sh_ms[None], sc_ms[None], gt_ms[None],
                tm(state_pool[e]), tm(state_conv[e]), state_lru[e][None], wts, tt=t_new, bb=bs, start=past_len)
            bm = lambda a, n: jnp.swapaxes(a.reshape(n, bs, a.shape[-1]), 0, 1)
            outs.setdefault("pool_p", []).append(pool_p)
            outs.setdefault("pool_s", []).append(bm(pool_s, POOL_HALO - 1))
            outs.setdefault("conv_p", []).append(conv_p)
            outs.setdefault("conv_s", []).append(bm(conv_s, CONV_WIDTH - 1))
            outs.setdefault("lru_p", []).append(lru_p.reshape(bp, pw))
            outs.setdefault("lru_s", []).append(lru_s.reshape(bs, pw))
            xp = x1p.reshape(n_p, d)
            xs = bm(x1s, t_new).reshape(n_s, d)
            proj_p = proj_s = None
        else:
            o = layer // 2
            gain = norm_mix[layer].reshape(1, d)
            w_in = mla_w_in[o]
            w_kr = w_in[:, q_lora + kv_lora:]
            win = jnp.concatenate([w_in[:, :q_lora + kv_lora], jnp.zeros((d, QK_NOPE), F32), w_kr,
                                   _swap_halves(w_kr)], axis=1).astype(BF16)
            wq = mla_w_uq[o].reshape(q_lora, MLA_HEADS, QK_NOPE + QK_ROPE)
            wuq = jnp.concatenate([wq, _swap_halves(wq[..., QK_NOPE:])], axis=-1).reshape(
                q_lora, MLA_HEADS * HEAD_PAD).astype(BF16)
            wuk = jnp.concatenate([mla_w_uk[o], jnp.zeros((kv_lora, MLA_HEADS, HEAD_PAD - QK_NOPE), F32)],
                                  axis=-1).reshape(kv_lora, MLA_HEADS * HEAD_PAD).astype(BF16)
            wuv_t = jnp.concatenate([mla_w_uv[o], jnp.zeros((kv_lora, MLA_HEADS, HEAD_PAD - V_DIM), F32)],
                                    axis=-1).reshape(kv_lora, MLA_HEADS * HEAD_PAD).T.astype(BF16)
            vone_t = jnp.tile((jnp.arange(HEAD_PAD) == V_DIM).astype(F32), MLA_HEADS).reshape(-1, 1)
            qg = mla_q_norm[o].reshape(1, q_lora)
            kvg = mla_kv_norm[o].reshape(1, kv_lora)
            wout = mla_w_out[o].astype(BF16)

            ctab, stab = _rope_tables(jnp.arange(seq, dtype=jnp.int32))
            ckv_p, kr_p, qt_p, k_p, vt_p = _mla_proj(
                xp, per_seq(sh_m), per_seq(sc_m), gain, win, qg, wuq.T, kvg, ctab, stab,
                tile=tile_p, rows_per_mod=seq, tab_tiles=seq // tile_p, kv_w=(wuk, wuv_t, vone_t, ctab.T, stab.T))
            qw = MLA_HEADS * HEAD_PAD
            o_p = _attention(qt_p, k_p.reshape(bp, seq, qw), vt_p, tq=tile_p, tk=tile_p)
            proj_p = (o_p.reshape(n_p, MLA_HEADS * V_DIM), wout, per_seq(gt_m))
            outs.setdefault("lat_p", []).append(ckv_p.reshape(bp, seq, kv_lora))
            outs.setdefault("rope_p", []).append(kr_p.reshape(bp, seq, QK_ROPE))

            pos_s = jnp.tile(past_len + jnp.arange(t_new, dtype=jnp.int32), bs)
            ctab_s, stab_s = _rope_tables(pos_s)
            ckv_s, kr_s, q_s = _mla_proj(
                xs, rep(sh_ms), rep(sc_ms), gain, win, qg, wuq, kvg, ctab_s, stab_s,
                tile=tile_s, rows_per_mod=n_s, tab_tiles=n_s // tile_s)
            aw = kv_lora + HEAD_PAD
            wk_t = jnp.transpose(mla_w_uk[o], (1, 2, 0))
            mats = jnp.zeros((MLA_HEADS, HEAD_PAD, aw), F32)
            mats = mats.at[:, :QK_NOPE, :kv_lora].set(wk_t)
            mats = mats.at[:, QK_NOPE:QK_NOPE + QK_ROPE, kv_lora:kv_lora + QK_ROPE].set(
                jnp.broadcast_to(jnp.eye(QK_ROPE, dtype=F32), (MLA_HEADS, QK_ROPE, QK_ROPE)))
            qabs = _qabs(q_s, mats.astype(BF16))
            qabs = jnp.transpose(qabs.reshape(MLA_HEADS, bs, t_new, aw), (1, 0, 2, 3)).reshape(
                bs, MLA_HEADS * t_new, aw)
            t_pad = 8
            pad_t = lambda a: jnp.pad(a.reshape(bs, t_new, -1), ((0, 0), (0, t_pad - t_new), (0, 0)))
            o_lat = _paged_attn(page_table, qabs, pad_t(ckv_s), pad_t(kr_s), cache_kv_latent[o],
                                jnp.swapaxes(cache_k_rope[o], 1, 2),
                                group=_pick(bs, 4), pages=_pick(page_table.shape[1], 8), t_new=t_new)
            o_lat = jnp.transpose(o_lat.reshape(bs, MLA_HEADS, t_new, kv_lora), (1, 0, 2, 3)).reshape(
                MLA_HEADS, n_s, kv_lora)
            wv = jnp.transpose(mla_w_uv[o], (1, 0, 2))
            w_pairs = jnp.zeros((MLA_HEADS // 2, 2 * kv_lora, 2 * V_DIM), F32)
            w_pairs = w_pairs.at[:, :kv_lora, :V_DIM].set(wv[0::2]).at[:, kv_lora:, V_DIM:].set(wv[1::2])
            o_s = _vup(o_lat, w_pairs.astype(BF16))
            proj_s = (o_s, wout, rep(gt_ms))
            outs.setdefault("lat_s", []).append(ckv_s.reshape(bs, t_new, kv_lora))
            outs.setdefault("rope_s", []).append(kr_s.reshape(bs, t_new, QK_ROPE))

        xp = _moe(xp, ffn_mods_p, rw, experts, layer=layer, tile=tile_p, gmm_tile=gmm_tile_p, rows_per_mod=seq,
                  proj=proj_p, final_gain=final_gain)
        xs = _moe(xs, ffn_mods_s, rw, experts, layer=layer, tile=tile_s, gmm_tile=gmm_tile_s, rows_per_mod=n_s,
                  proj=proj_s, final_gain=final_gain)

    st = lambda k: jnp.stack(outs[k])
    return (xp.reshape(bp, seq, d), xs.reshape(bs, t_new, d),
            st("pool_p"), st("pool_s"), st("conv_p"), st("conv_s"), st("lru_p"), st("lru_s"),
            st("lat_p"), st("lat_s"), st("rope_p"), st("rope_s"))
```

```python
import functools

import jax
import jax.numpy as jnp
from jax import lax
from jax.experimental import pallas as pl
from jax.experimental.pallas import tpu as pltpu

F32 = jnp.float32
BF16 = jnp.bfloat16

EPS = 1e-6
POOL_WINDOWS = (2, 4, 8, 16)
POOL_HALO = 16
CONV_WIDTH = 4
CONV_HALO = 8
LRU_C = 8.0
MLA_HEADS = 16
QK_NOPE = 64
QK_ROPE = 32
V_DIM = 64
HEAD_PAD = 128
ROPE_THETA = 10000.0
SM_SCALE = (QK_NOPE + QK_ROPE) ** -0.5
MOE_GROUPS = 4
EXPERTS_PER_GROUP = 8
N_EXPERTS = MOE_GROUPS * EXPERTS_PER_GROUP
TOP_K = 2
LANES = 128
ROW_DMA_UNROLL = 8
PAGE_RING = 3
LOG2E = 1.4426950408889634
VMEM_LIMIT = 56 * 1024 * 1024


def _cparams(n_axes):
    return pltpu.CompilerParams(dimension_semantics=("arbitrary",) * n_axes,
                                vmem_limit_bytes=VMEM_LIMIT)


def _dot(a, b):
    return jnp.dot(a, b, preferred_element_type=F32)


def _dot_nt(a, b):
    return lax.dot_general(a, b, (((1,), (1,)), ((), ())), preferred_element_type=F32)


def _rms(x, gain):
    return x * lax.rsqrt(jnp.mean(x * x, axis=-1, keepdims=True) + EPS) * gain


def _silu(x):
    return x * jax.nn.sigmoid(x)


def _ada_kernel(c_ref, w_ref, b_ref, o_ref):
    a = _silu(c_ref[...]).astype(BF16)
    o_ref[0] = _dot(a, w_ref[0].astype(BF16)) + b_ref[0]


def _ada(c_all, ada_w, ada_b):
    depth, d, n6 = ada_w.shape
    rows = c_all.shape[0]
    tn = 1024
    return pl.pallas_call(
        _ada_kernel,
        grid=(depth, n6 // tn),
        in_specs=[pl.BlockSpec((rows, d), lambda l, j: (0, 0)),
                  pl.BlockSpec((1, d, tn), lambda l, j: (l, 0, j)),
                  pl.BlockSpec((1, 1, tn), lambda l, j: (l, 0, j))],
        out_specs=pl.BlockSpec((1, rows, tn), lambda l, j: (l, 0, j)),
        out_shape=jax.ShapeDtypeStruct((depth, rows, n6), F32),
        compiler_params=_cparams(2),
        name="ada_mod",
    )(c_all, ada_w, ada_b.reshape(depth, 1, n6))


def _mix0_kernel(x_ref, sh_ref, sc_ref, gt_ref, pool0_ref, conv0_ref, lru0_ref,
                 gain_ref, win_ref, poolw_ref, pscale_ref, convw_ref, convb_ref,
                 wri_ref, bri_ref, lam_ref, wout_ref,
                 x1_ref, pooln_ref, convn_ref, lrun_ref,
                 zp_ref, zc_ref, h_ref, *, tt, bb, start, n_t):
    t = pl.program_id(1)
    rows = tt * bb
    pw = zp_ref.shape[1]
    p0 = POOL_HALO * bb
    c0 = CONV_HALO * bb

    @pl.when(t == 0)
    def _():
        zp_ref[0:bb, :] = jnp.zeros((bb, pw), F32)
        zp_ref[bb:p0, :] = pool0_ref[0]
        zc_ref[0:c0 - (CONV_WIDTH - 1) * bb, :] = jnp.zeros((c0 - (CONV_WIDTH - 1) * bb, pw), F32)
        zc_ref[c0 - (CONV_WIDTH - 1) * bb:c0, :] = conv0_ref[0]
        h_ref[...] = lru0_ref[0]

    def per_row(v):
        return v if bb == 1 else jnp.concatenate([v] * tt, axis=0)

    x = x_ref[0]
    h = _rms(x, gain_ref[...]) * (1.0 + per_row(sc_ref[0])) + per_row(sh_ref[0])
    proj = _dot(h.astype(BF16), win_ref[...])
    u_pool = proj[:, :pw]
    u_x = proj[:, pw:2 * pw]
    u_g = proj[:, 2 * pw:]
    zp_ref[p0:p0 + rows, :] = u_pool
    zc_ref[c0:c0 + rows, :] = u_x

    if bb == 1:
        tix = lax.broadcasted_iota(jnp.int32, (rows, 1), 0)
    else:
        tix = jnp.concatenate([jnp.full((bb, 1), i, jnp.int32) for i in range(tt)], axis=0)
    pos = start + t * tt + tix

    gd = pw // len(POOL_WINDOWS)
    means = []
    if bb == 1 and POOL_WINDOWS == (2, 4, 8, 16):
        sw = zp_ref[0:p0 + rows, :]
        for g, w in enumerate(POOL_WINDOWS):
            sw = sw + pltpu.roll(sw, w // 2, 0)
            cnt = jnp.minimum(pos + 1, w).astype(F32)
            means.append(sw[p0:, :gd] / cnt)
            if g + 1 < len(POOL_WINDOWS):
                sw = sw[:, gd:]
    else:
        for g, w in enumerate(POOL_WINDOWS):
            cols = slice(g * gd, (g + 1) * gd)
            acc = zp_ref[p0:p0 + rows, cols]
            for i in range(1, w):
                acc = acc + zp_ref[p0 - i * bb:p0 - i * bb + rows, cols]
            cnt = jnp.minimum(pos + 1, w).astype(F32)
            means.append(acc / cnt)
    pooled = jnp.concatenate(means, axis=-1) - u_pool
    y_a = _dot(pooled.astype(BF16), poolw_ref[...]) * pscale_ref[...]

    xc = convb_ref[...]
    if bb == 1:
        z_al = zc_ref[0:c0 + rows, :]
        for k in range(CONV_WIDTH):
            back = CONV_WIDTH - 1 - k
            z_k = z_al if back == 0 else pltpu.roll(z_al, back, 0)
            xc = xc + z_k[c0:, :] * convw_ref[k:k + 1, :]
    else:
        for k in range(CONV_WIDTH):
            off = c0 - (CONV_WIDTH - 1 - k) * bb
            xc = xc + zc_ref[off:off + rows, :] * convw_ref[k:k + 1, :]
    pre = _dot(xc.astype(BF16), wri_ref[...]) + bri_ref[...]
    r = jax.nn.sigmoid(pre[:, :pw])
    gi = jax.nn.sigmoid(pre[:, pw:])
    lam = lam_ref[...]
    softplus_neg = jnp.maximum(-lam, 0.0) + jnp.log1p(jnp.exp(-jnp.abs(lam)))
    log_a = -LRU_C * r * softplus_neg
    a = jnp.exp(log_a)
    one_m = 1.0 - a * a
    b = jnp.where(one_m > 0.0, one_m * lax.rsqrt(one_m), 0.0) * gi * xc

    if bb == 1:
        rowi = lax.broadcasted_iota(jnp.int32, (rows, 1), 0)
        s = 1
        while s < rows:
            if s < 8:
                keep = rowi >= s
                b = jnp.where(keep, a * pltpu.roll(b, s, 0) + b, b)
                a = jnp.where(keep, a * pltpu.roll(a, s, 0), a)
            else:
                b = jnp.concatenate([b[:s], a[s:] * b[:rows - s] + b[s:]], axis=0)
                a = jnp.concatenate([a[:s], a[s:] * a[:rows - s]], axis=0)
            s *= 2
        hs = b + a * h_ref[...]
        h_ref[...] = hs[rows - 1:rows, :]
    else:
        hprev = h_ref[...]
        parts = []
        for i in range(tt):
            hprev = a[i * bb:(i + 1) * bb] * hprev + b[i * bb:(i + 1) * bb]
            parts.append(hprev)
        hs = jnp.concatenate(parts, axis=0)
        h_ref[...] = hprev
    y_b = hs * jax.nn.gelu(u_g)

    mix = _dot(y_a.astype(BF16), wout_ref[0:pw, :]) + _dot(y_b.astype(BF16), wout_ref[pw:2 * pw, :])
    x1_ref[0] = x + per_row(gt_ref[0]) * mix

    @pl.when(t == n_t - 1)
    def _():
        pooln_ref[0] = zp_ref[p0 + rows - (POOL_HALO - 1) * bb:p0 + rows, :]
        convn_ref[0] = zc_ref[c0 + rows - (CONV_WIDTH - 1) * bb:c0 + rows, :]
        lrun_ref[0] = h_ref[...]

    if n_t > 1:
        @pl.when(t < n_t - 1)
        def _():
            zp_ref[bb:p0, :] = zp_ref[rows + bb:rows + p0, :]
            zc_ref[c0 - (CONV_WIDTH - 1) * bb:c0, :] = zc_ref[c0 + rows - (CONV_WIDTH - 1) * bb:c0 + rows, :]


def _mix0(x, sh, sc, gt, pool0, conv0, lru0, wts, *, tt, bb, start):
    nbb, tot, d = x.shape
    n_t = tot // (tt * bb)
    rows = tt * bb
    pw = pool0.shape[-1]
    hp = (POOL_HALO - 1) * bb
    hc = (CONV_WIDTH - 1) * bb

    def const(a):
        nd = a.ndim
        return pl.BlockSpec(a.shape, lambda i, j: (0,) * nd)

    def per_b(n_rows, width):
        return pl.BlockSpec((1, n_rows, width), lambda i, j: (i, 0, 0))

    kern = functools.partial(_mix0_kernel, tt=tt, bb=bb, start=start, n_t=n_t)
    return pl.pallas_call(
        kern,
        grid=(nbb, n_t),
        in_specs=[pl.BlockSpec((1, rows, d), lambda i, j: (i, j, 0)),
                  per_b(bb, d), per_b(bb, d), per_b(bb, d),
                  per_b(hp, pw), per_b(hc, pw), per_b(bb, pw)] + [const(w) for w in wts],
        out_specs=[pl.BlockSpec((1, rows, d), lambda i, j: (i, j, 0)),
                   per_b(hp, pw), per_b(hc, pw), per_b(bb, pw)],
        out_shape=[jax.ShapeDtypeStruct((nbb, tot, d), F32),
                   jax.ShapeDtypeStruct((nbb, hp, pw), F32),
                   jax.ShapeDtypeStruct((nbb, hc, pw), F32),
                   jax.ShapeDtypeStruct((nbb, bb, pw), F32)],
        scratch_shapes=[pltpu.VMEM(((POOL_HALO + tt) * bb, pw), F32),
                        pltpu.VMEM(((CONV_HALO + tt) * bb, pw), F32),
                        pltpu.VMEM((bb, pw), F32)],
        compiler_params=_cparams(2),
        name="mix0",
    )(x, sh, sc, gt, pool0, conv0, lru0, *wts)


ROUTE_ROWS = 40
def _route_kernel(*refs, with_proj):
    if with_proj:
        (x_ref, o_ref, wout_ref, gtm_ref, shf_ref, scf_ref, gain_ref, wrh_ref, wrl_ref, br_ref,
         x1_ref, hf_ref, info_ref, meta_ref, cnt_ref, run_ref) = refs
    else:
        (x_ref, shf_ref, scf_ref, gain_ref, wrh_ref, wrl_ref, br_ref,
         hf_ref, info_ref, meta_ref, cnt_ref, run_ref) = refs

    @pl.when(pl.program_id(0) == 0)
    def _():
        run_ref[...] = jnp.zeros(run_ref.shape, F32)

    x = x_ref[...]
    if with_proj:
        x = x + gtm_ref[0] * _dot(o_ref[...], wout_ref[...])
        x1_ref[...] = x
    hf = _rms(x, gain_ref[...]) * (1.0 + scf_ref[0]) + shf_ref[0]
    hf_ref[...] = hf

    hi = hf.astype(BF16)
    lo = (hf - hi.astype(F32)).astype(BF16)
    logits = _dot(hi, wrh_ref[...]) + _dot(lo, wrh_ref[...]) + _dot(hi, wrl_ref[...]) + br_ref[...]

    tq = logits.shape[0]
    lt = logits.T[:ROUTE_ROWS]
    row = lax.broadcasted_iota(jnp.int32, (ROUTE_ROWS, tq), 0).astype(F32)
    neg = -jnp.inf
    big = float(ROUTE_ROWS)
    is_g = row < MOE_GROUPS
    gl = jnp.where(is_g, lt, neg)
    mg = jnp.max(gl, axis=0, keepdims=True)
    gidx = jnp.min(jnp.where(gl == mg, row, big), axis=0, keepdims=True)
    p_group = 1.0 / jnp.sum(jnp.where(is_g, jnp.exp(gl - mg), 0.0), axis=0, keepdims=True)
    first = MOE_GROUPS + gidx * EXPERTS_PER_GROUP
    el = jnp.where((row >= first) & (row < first + EXPERTS_PER_GROUP), lt, neg)
    v1 = jnp.max(el, axis=0, keepdims=True)
    i1 = jnp.min(jnp.where(el == v1, row, big), axis=0, keepdims=True)
    el2 = jnp.where(row == i1, neg, el)
    v2 = jnp.max(el2, axis=0, keepdims=True)
    i2 = jnp.min(jnp.where(el2 == v2, row, big), axis=0, keepdims=True)
    ex = jnp.exp(v2 - v1)
    g1 = p_group / (1.0 + ex)
    g2 = p_group * ex / (1.0 + ex)
    e1 = i1 - MOE_GROUPS
    e2 = i2 - MOE_GROUPS

    erow = lax.broadcasted_iota(jnp.int32, (N_EXPERTS, tq), 0).astype(F32)
    oh1 = jnp.where(erow == e1, 1.0, 0.0)
    oh2 = jnp.where(erow == e2, 1.0, 0.0)
    oh = oh1 + oh2
    ri = lax.broadcasted_iota(jnp.int32, (tq, tq), 0)
    ci = lax.broadcasted_iota(jnp.int32, (tq, tq), 1)
    earlier = jnp.where(ri < ci, 1.0, 0.0).astype(BF16)
    before = _dot(oh.astype(BF16), earlier) + run_ref[...]
    r1 = jnp.sum(before * oh1, axis=0, keepdims=True)
    r2 = jnp.sum(before * oh2, axis=0, keepdims=True)
    run_ref[...] = run_ref[...] + jnp.sum(oh, axis=1, keepdims=True)
    cnt_ref[...] = jnp.broadcast_to(run_ref[...], cnt_ref.shape)

    meta = jnp.concatenate([e1, e2, g1, g2, r1, r2, jnp.zeros((2, tq), F32)], axis=0)
    meta_ref[0] = meta
    info_ref[...] = jnp.concatenate([meta, jnp.zeros((LANES - 8, tq), F32)], axis=0).T


def _route(x, shf, scf, gain, wrh, wrl, br, *, tile, rows_per_mod, proj=None):
    n, d = x.shape
    nt = n // tile
    mrows = shf.shape[1]
    mod_spec = pl.BlockSpec((1, mrows, d), lambda i: ((i * tile) // rows_per_mod, 0, 0))
    row_spec = pl.BlockSpec((tile, d), lambda i: (i, 0))

    def const(a):
        nd = a.ndim
        return pl.BlockSpec(a.shape, lambda i: (0,) * nd)

    info_spec = pl.BlockSpec((tile, LANES), lambda i: (i, 0))
    meta_spec = pl.BlockSpec((1, 8, tile), lambda i: (i, 0, 0))
    cnt_spec = pl.BlockSpec((N_EXPERTS, LANES), lambda i: (0, 0))
    outs_shape = [jax.ShapeDtypeStruct((n, d), F32), jax.ShapeDtypeStruct((n, LANES), F32),
                  jax.ShapeDtypeStruct((nt, 8, tile), F32), jax.ShapeDtypeStruct((N_EXPERTS, LANES), F32)]
    outs_spec = [row_spec, info_spec, meta_spec, cnt_spec]
    if proj is None:
        ins = [x, shf, scf, gain, wrh, wrl, br]
        in_specs = [row_spec, mod_spec, mod_spec, const(gain), const(wrh), const(wrl), const(br)]
    else:
        o, wout, gtm = proj
        ins = [x, o, wout, gtm, shf, scf, gain, wrh, wrl, br]
        in_specs = [row_spec, pl.BlockSpec((tile, o.shape[1]), lambda i: (i, 0)), const(wout), mod_spec,
                    mod_spec, mod_spec, const(gain), const(wrh), const(wrl), const(br)]
        outs_shape = [jax.ShapeDtypeStruct((n, d), F32)] + outs_shape
        outs_spec = [row_spec] + outs_spec
    return pl.pallas_call(
        functools.partial(_route_kernel, with_proj=proj is not None),
        grid=(nt,),
        in_specs=in_specs,
        out_specs=outs_spec,
        out_shape=outs_shape,
        scratch_shapes=[pltpu.VMEM((N_EXPERTS, 1), F32)],
        compiler_params=_cparams(1),
        name="route",
    )(*ins)


def _dispatch_kernel(slot_ref, hf_ref, xs_ref, sem, *, tile):
    def issue(g, c):
        for u in range(ROW_DMA_UNROLL):
            r = g * ROW_DMA_UNROLL + u
            for k in range(TOP_K):
                s = slot_ref[0, 0, k * tile + r]
                pltpu.make_async_copy(hf_ref.at[pl.ds(r, 1), :], xs_ref.at[pl.ds(s, 1), :], sem).start()
        return c

    lax.fori_loop(0, tile // ROW_DMA_UNROLL, issue, 0)
    for k in range(TOP_K):
        pltpu.make_async_copy(hf_ref, xs_ref.at[pl.ds(0, tile), :], sem).wait()


def _dispatch(hf, slots, *, tile):
    n, d = hf.shape
    nt = n // tile
    return pl.pallas_call(
        functools.partial(_dispatch_kernel, tile=tile),
        grid=(nt,),
        in_specs=[pl.BlockSpec((1, 1, TOP_K * tile), lambda i: (i, 0, 0), memory_space=pltpu.SMEM),
                  pl.BlockSpec((tile, d), lambda i: (i, 0))],
        out_specs=pl.BlockSpec(memory_space=pl.ANY),
        out_shape=jax.ShapeDtypeStruct((TOP_K * n, d), F32),
        scratch_shapes=[pltpu.SemaphoreType.DMA(())],
        compiler_params=_cparams(1),
        name="moe_dispatch",
    )(slots.reshape(nt, 1, TOP_K * tile), hf)


def _gmm_kernel(wt_ref, we_ref, wlo_ref, whi_ref, xs_ref, wg_ref, wu_ref, wd_ref, y_ref,
                wgb_ref, wub_ref, wdb_ref):
    w = pl.program_id(0)
    prev = jnp.maximum(w - 1, 0)
    new_expert = (w == 0) | (we_ref[w] != we_ref[prev])
    new_tile = (w == 0) | (wt_ref[w] != wt_ref[prev])

    @pl.when(new_expert)
    def _():
        wgb_ref[...] = wg_ref[0, 0].astype(BF16)
        wub_ref[...] = wu_ref[0, 0].astype(BF16)
        wdb_ref[...] = wd_ref[0, 0].astype(BF16)

    @pl.when(new_tile)
    def _():
        y_ref[...] = jnp.zeros(y_ref.shape, F32)

    lo = wlo_ref[w]
    hi = whi_ref[w]

    @pl.when(hi > lo)
    def _():
        x = xs_ref[...].astype(BF16)
        g = _dot(x, wgb_ref[...])
        u = _dot(x, wub_ref[...])
        yv = _dot((_silu(g) * u).astype(BF16), wdb_ref[...])
        row = lax.broadcasted_iota(jnp.int32, (x.shape[0], 1), 0)
        y_ref[...] = y_ref[...] + jnp.where((row >= lo) & (row < hi), yv, 0.0)


def _gmm(xs, work, w_gate, w_up, w_down, *, tile, layer):
    m, d = xs.shape
    ff = w_gate.shape[-1]
    n_work = work[0].shape[0]
    grid_spec = pltpu.PrefetchScalarGridSpec(
        num_scalar_prefetch=4,
        grid=(n_work,),
        in_specs=[pl.BlockSpec((tile, d), lambda w, wt, we, wlo, whi: (wt[w], 0)),
                  pl.BlockSpec((1, 1, d, ff), lambda w, wt, we, wlo, whi: (layer, we[w], 0, 0)),
                  pl.BlockSpec((1, 1, d, ff), lambda w, wt, we, wlo, whi: (layer, we[w], 0, 0)),
                  pl.BlockSpec((1, 1, ff, d), lambda w, wt, we, wlo, whi: (layer, we[w], 0, 0))],
        out_specs=pl.BlockSpec((tile, d), lambda w, wt, we, wlo, whi: (wt[w], 0)),
        scratch_shapes=[pltpu.VMEM((d, ff), BF16), pltpu.VMEM((d, ff), BF16), pltpu.VMEM((ff, d), BF16)],
    )
    return pl.pallas_call(
        _gmm_kernel,
        grid_spec=grid_spec,
        out_shape=jax.ShapeDtypeStruct((m, d), F32),
        compiler_params=_cparams(1),
        name="moe_gmm",
    )(*work, xs, w_gate, w_up, w_down)


def _work_list(counts, n_slots, tile):
    n_tiles = n_slots // tile
    n_work = n_tiles + N_EXPERTS - 1
    ends = jnp.cumsum(counts)
    starts = ends - counts
    first_tile = starts // tile
    last_tile = jnp.maximum(ends - 1, 0) // tile
    n_items = jnp.where(counts > 0, last_tile - first_tile + 1, 0)
    item_end = jnp.cumsum(n_items)
    item_start = item_end - n_items
    w = jnp.arange(n_work, dtype=jnp.int32)
    used = w < item_end[-1]
    e = jnp.minimum(jnp.sum(w[:, None] >= item_end[None, :], axis=1), N_EXPERTS - 1).astype(jnp.int32)
    is_e = e[:, None] == jnp.arange(N_EXPERTS, dtype=jnp.int32)[None, :]

    def of_e(v):
        return jnp.sum(jnp.where(is_e, v[None, :], 0), axis=1)

    t = of_e(first_tile) + (w - of_e(item_start))
    lo = jnp.maximum(of_e(starts), t * tile) - t * tile
    hi = jnp.minimum(of_e(ends), (t + 1) * tile) - t * tile
    last_e = jnp.max(jnp.where(counts > 0, jnp.arange(N_EXPERTS), 0)).astype(jnp.int32)
    wt = jnp.where(used, t, n_tiles - 1).astype(jnp.int32)
    we = jnp.where(used, e, last_e).astype(jnp.int32)
    wlo = jnp.where(used, lo, 0).astype(jnp.int32)
    whi = jnp.where(used, hi, 0).astype(jnp.int32)
    return (wt, we, wlo, whi), starts


def _combine_kernel(*refs, tile, final):
    if final:
        slot_ref, nslot_ref, x_ref, gt_ref, info_ref, fg_ref, y_ref, out_ref, ybuf, sem = refs
    else:
        slot_ref, nslot_ref, x_ref, gt_ref, info_ref, y_ref, out_ref, ybuf, sem = refs
    i = pl.program_id(0)
    cur = i % 2

    def gather(slots_ref, buf):
        def issue(g, c):
            for u in range(ROW_DMA_UNROLL):
                r = g * ROW_DMA_UNROLL + u
                for k in range(TOP_K):
                    s = slots_ref[0, 0, k * tile + r]
                    pltpu.make_async_copy(y_ref.at[pl.ds(s, 1), :], ybuf.at[buf, k, pl.ds(r, 1), :],
                                          sem.at[buf]).start()
            return c

        lax.fori_loop(0, tile // ROW_DMA_UNROLL, issue, 0)

    @pl.when(i == 0)
    def _():
        gather(slot_ref, cur)

    @pl.when(i + 1 < pl.num_programs(0))
    def _():
        gather(nslot_ref, 1 - cur)

    for k in range(TOP_K):
        pltpu.make_async_copy(y_ref.at[pl.ds(0, tile), :], ybuf.at[cur, k], sem.at[cur]).wait()

    info = info_ref[...]
    ffn = info[:, 2:3] * ybuf[cur, 0] + info[:, 3:4] * ybuf[cur, 1]
    out = x_ref[...] + gt_ref[0] * ffn
    if final:
        out = _rms(out, fg_ref[...])
    out_ref[...] = out


def _combine(x, gt, info, y, slots, *, tile, rows_per_mod, final_gain=None):
    n, d = x.shape
    nt = n // tile
    mrows = gt.shape[1]
    final = final_gain is not None
    slots3 = slots.reshape(nt, 1, TOP_K * tile)
    in_specs = [pl.BlockSpec((1, 1, TOP_K * tile), lambda i: (i, 0, 0), memory_space=pltpu.SMEM),
                pl.BlockSpec((1, 1, TOP_K * tile), lambda i: (jnp.minimum(i + 1, nt - 1), 0, 0),
                             memory_space=pltpu.SMEM),
                pl.BlockSpec((tile, d), lambda i: (i, 0)),
                pl.BlockSpec((1, mrows, d), lambda i: ((i * tile) // rows_per_mod, 0, 0)),
                pl.BlockSpec((tile, LANES), lambda i: (i, 0))]
    ins = [slots3, slots3, x, gt, info]
    if final:
        in_specs.append(pl.BlockSpec(final_gain.shape, lambda i: (0, 0)))
        ins.append(final_gain)
    in_specs.append(pl.BlockSpec(memory_space=pl.ANY))
    ins.append(y)
    return pl.pallas_call(
        functools.partial(_combine_kernel, tile=tile, final=final),
        grid=(nt,),
        in_specs=in_specs,
        out_specs=pl.BlockSpec((tile, d), lambda i: (i, 0)),
        out_shape=jax.ShapeDtypeStruct((n, d), F32),
        scratch_shapes=[pltpu.VMEM((2, TOP_K, tile, d), F32), pltpu.SemaphoreType.DMA((2,))],
        compiler_params=_cparams(1),
        name="moe_combine",
    )(*ins)


def _moe(x, mods, rw, experts, *, layer, tile, gmm_tile, rows_per_mod, proj=None, final_gain=None):
    shf, scf, gtf = mods
    gain, wrh, wrl, br = rw
    res = _route(x, shf, scf, gain, wrh, wrl, br, tile=tile, rows_per_mod=rows_per_mod, proj=proj)
    if proj is not None:
        x, hf, info, meta, cnt = res
    else:
        hf, info, meta, cnt = res
    n = x.shape[0]
    counts = cnt[:, 0].astype(jnp.int32)
    work, starts = _work_list(counts, TOP_K * n, gmm_tile)
    eid = meta[:, 0:TOP_K, :].astype(jnp.int32)
    rank = meta[:, 4:4 + TOP_K, :].astype(jnp.int32)
    start_of = jnp.zeros(eid.shape, jnp.int32)
    for e in range(N_EXPERTS):
        start_of = jnp.where(eid == e, starts[e].astype(jnp.int32), start_of)
    slots = start_of + rank
    xs = _dispatch(hf, slots, tile=tile)
    y = _gmm(xs, work, *experts, tile=gmm_tile, layer=layer)
    return _combine(x, gtf, info, y, slots, tile=tile, rows_per_mod=rows_per_mod, final_gain=final_gain)


def _rope_turn(blk, c, s):
    return blk * c + pltpu.roll(blk, LANES - QK_ROPE, 1) * s


def _mla_proj_kernel(*refs, sample):
    if sample:
        (x_ref, sh_ref, sc_ref, gain_ref, win_ref, qg_ref, wuq_ref, kvg_ref, c_ref, s_ref,
         ckv_ref, kr_ref, q_ref) = refs
    else:
        (x_ref, sh_ref, sc_ref, gain_ref, win_ref, qg_ref, wuq_ref, kvg_ref, c_ref, s_ref, wuk_ref, wuvt_ref,
         vonet_ref, ct_ref, st_ref, ckv_ref, kr_ref, q_ref, k_ref, v_ref) = refs
    q_lora = qg_ref.shape[1]
    kv_lora = kvg_ref.shape[1]
    h = _rms(x_ref[...], gain_ref[...]) * (1.0 + sc_ref[0]) + sh_ref[0]
    proj = _dot(h.astype(BF16), win_ref[...])
    qn = _rms(proj[:, :q_lora], qg_ref[...])
    ckv = _rms(proj[:, q_lora:q_lora + kv_lora], kvg_ref[...])
    c = c_ref[...]
    s = s_ref[...]
    kf = _rope_turn(proj[:, q_lora + kv_lora:], c, s)
    ckv_ref[...] = ckv
    kr_ref[...] = kf[:, QK_NOPE:QK_NOPE + QK_ROPE]
    qn_b = qn.astype(BF16)
    if sample:
        q = _dot(qn_b, wuq_ref[...])
        for hh in range(MLA_HEADS):
            cols = slice(hh * HEAD_PAD, (hh + 1) * HEAD_PAD)
            q_ref[:, cols] = _rope_turn(q[:, cols], c, s) * SM_SCALE
    else:
        ckv_b = ckv.astype(BF16)
        kn = _dot(ckv_b, wuk_ref[...])
        qt = _dot_nt(wuq_ref[...], qn_b)
        v_ref[0] = (_dot_nt(wuvt_ref[...], ckv_b) + vonet_ref[...]).astype(BF16)
        ct = ct_ref[...]
        st = st_ref[...]
        for hh in range(MLA_HEADS):
            cols = slice(hh * HEAD_PAD, (hh + 1) * HEAD_PAD)
            k_ref[:, cols] = (kn[:, cols] + kf).astype(BF16)
            blk = qt[cols, :]
            turned = jnp.concatenate([blk[QK_ROPE:], blk[:QK_ROPE]], axis=0)
            q_ref[0, cols, :] = ((blk * ct + turned * st) * (SM_SCALE * LOG2E)).astype(BF16)


def _mla_proj(x, sh, sc, gain, win, qg, wuq, kvg, ctab, stab, *, tile, rows_per_mod, tab_tiles, kv_w=None):
    n, d = x.shape
    nt = n // tile
    mrows = sh.shape[1]
    sample = kv_w is None
    kv_lora = kvg.shape[1]

    def const(a):
        nd = a.ndim
        return pl.BlockSpec(a.shape, lambda i: (0,) * nd)

    row = lambda width: pl.BlockSpec((tile, width), lambda i: (i, 0))
    mod_spec = pl.BlockSpec((1, mrows, d), lambda i: ((i * tile) // rows_per_mod, 0, 0))
    tab_spec = pl.BlockSpec((tile, LANES), lambda i: (i % tab_tiles, 0))
    ins = [x, sh, sc, gain, win, qg, wuq, kvg, ctab, stab]
    in_specs = [row(d), mod_spec, mod_spec, const(gain), const(win), const(qg), const(wuq), const(kvg),
                tab_spec, tab_spec]
    qw = MLA_HEADS * HEAD_PAD
    out_shape = [jax.ShapeDtypeStruct((n, kv_lora), F32), jax.ShapeDtypeStruct((n, QK_ROPE), F32)]
    out_specs = [row(kv_lora), row(QK_ROPE)]
    if sample:
        out_shape.append(jax.ShapeDtypeStruct((n, qw), F32))
        out_specs.append(row(qw))
    else:
        wuk, wuvt, vonet, ctab_t, stab_t = kv_w
        seq_len = tab_tiles * tile
        tab_t_spec = pl.BlockSpec((LANES, tile), lambda i: (0, i % tab_tiles))
        by_seq = pl.BlockSpec((1, qw, tile), lambda i: (i // tab_tiles, 0, i % tab_tiles))
        ins += [wuk, wuvt, vonet, ctab_t, stab_t]
        in_specs += [const(wuk), const(wuvt), const(vonet), tab_t_spec, tab_t_spec]
        out_shape += [jax.ShapeDtypeStruct((n // seq_len, qw, seq_len), BF16), jax.ShapeDtypeStruct((n, qw), BF16),
                      jax.ShapeDtypeStruct((n // seq_len, qw, seq_len), BF16)]
        out_specs += [by_seq, row(qw), by_seq]
    return pl.pallas_call(
        functools.partial(_mla_proj_kernel, sample=sample),
        grid=(nt,),
        in_specs=in_specs,
        out_specs=out_specs,
        out_shape=out_shape,
        compiler_params=_cparams(1),
        name="mla_proj",
    )(*ins)


def _attn_kernel(q_ref, k_ref, v_ref, o_ref, sa_ref, sb_ref, m_ref, acc_ref, *, tq, tk):
    i = pl.program_id(2)
    heads = range(2)

    def cols(hh):
        return slice(hh * HEAD_PAD, (hh + 1) * HEAD_PAD)

    def scores(j, dst_ref):
        keys = pl.ds(pl.multiple_of(j * tk, tk), tk)
        for hh in heads:
            dst_ref[hh] = _dot(k_ref[0, keys, cols(hh)], q_ref[0, cols(hh), :])

    def absorb(j, src_ref, masked):
        keys = pl.ds(pl.multiple_of(j * tk, tk), tk)
        if masked:
            keep = lax.broadcasted_iota(jnp.int32, (tk, tq), 0) <= lax.broadcasted_iota(jnp.int32, (tk, tq), 1)
        for hh in heads:
            s = src_ref[hh]
            if masked:
                s = jnp.where(keep, s, -jnp.inf)
            m = m_ref[hh]
            m_new = jnp.maximum(m, jnp.max(s, axis=0, keepdims=True))
            p = jnp.exp2(s - m_new)
            acc_ref[hh] = jnp.exp2(m - m_new) * acc_ref[hh] + _dot(v_ref[0, cols(hh), keys], p.astype(BF16))
            m_ref[hh] = m_new

    m_ref[...] = jnp.full(m_ref.shape, -jnp.inf, F32)
    acc_ref[...] = jnp.zeros(acc_ref.shape, F32)
    scores(0, sa_ref)

    def pair(j):
        scores(j + 1, sb_ref)
        absorb(j, sa_ref, False)
        scores(j + 2, sa_ref)
        absorb(j + 1, sb_ref, False)

    def two_pairs(jj, c):
        pair(4 * jj)
        pair(4 * jj + 2)
        return c

    n_quads = i // 4
    lax.fori_loop(0, n_quads, two_pairs, 0)

    @pl.when(i % 4 >= 2)
    def _():
        pair(4 * n_quads)

    @pl.when(i % 2 == 1)
    def _():
        scores(i, sb_ref)
        absorb(i - 1, sa_ref, False)
        absorb(i, sb_ref, True)

    @pl.when(i % 2 == 0)
    def _():
        absorb(i, sa_ref, True)

    out_t = jnp.concatenate([acc_ref[hh][:V_DIM] / acc_ref[hh][V_DIM:V_DIM + 1] for hh in heads], axis=0)
    o_ref[0] = out_t.T.astype(o_ref.dtype)


def _attention(q_t, k, v_t, *, tq, tk):
    assert tq == tk, "one diagonal key tile per query tile"
    b, s, _ = k.shape
    pairs = MLA_HEADS // 2
    return pl.pallas_call(
        functools.partial(_attn_kernel, tq=tq, tk=tk),
        grid=(b, pairs, s // tq),
        in_specs=[pl.BlockSpec((1, 2 * HEAD_PAD, tq), lambda bi, p, i: (bi, p, i)),
                  pl.BlockSpec((1, s, 2 * HEAD_PAD), lambda bi, p, i: (bi, 0, p)),
                  pl.BlockSpec((1, 2 * HEAD_PAD, s), lambda bi, p, i: (bi, p, 0))],
        out_specs=pl.BlockSpec((1, tq, 2 * V_DIM), lambda bi, p, i: (bi, i, p)),
        out_shape=jax.ShapeDtypeStruct((b, s, MLA_HEADS * V_DIM), BF16),
        scratch_shapes=[pltpu.VMEM((2, tk, tq), F32), pltpu.VMEM((2, tk, tq), F32),
                        pltpu.VMEM((2, 1, tq), F32), pltpu.VMEM((2, HEAD_PAD, tq), F32)],
        compiler_params=_cparams(3),
        name="prompt_attn",
    )(q_t, k, v_t)


def _qabs_kernel(q_ref, m_ref, o_ref):
    o_ref[0] = _dot(q_ref[...].astype(BF16), m_ref[0])


def _qabs(q, mats):
    n = q.shape[0]
    width = mats.shape[-1]
    return pl.pallas_call(
        _qabs_kernel,
        grid=(MLA_HEADS,),
        in_specs=[pl.BlockSpec((n, HEAD_PAD), lambda h: (0, h)),
                  pl.BlockSpec((1, HEAD_PAD, width), lambda h: (h, 0, 0))],
        out_specs=pl.BlockSpec((1, n, width), lambda h: (h, 0, 0)),
        out_shape=jax.ShapeDtypeStruct((MLA_HEADS, n, width), F32),
        compiler_params=_cparams(1),
        name="sample_qabs",
    )(q, mats)


def _paged_attn_kernel(pt_ref, q_ref, cn_ref, rn_ref, ck_hbm, kr_hbm, o_ref,
                       ckbuf, krbuf, m_ref, l_ref, acc_ref, sem, *, group, pages, page, n_chunks, t_new, t_pad):
    bg = pl.program_id(0)
    c = pl.program_id(1)
    step = bg * n_chunks + c
    n_steps = pl.num_programs(0) * n_chunks
    slot = step % PAGE_RING
    kv_lora = ckbuf.shape[-1]

    def start_chunk(st):
        bgi, ci, sl = st // n_chunks, st % n_chunks, st % PAGE_RING
        for g in range(group):
            for p in range(pages):
                phys = pt_ref[bgi * group + g, ci * pages + p]
                pltpu.make_async_copy(ck_hbm.at[phys], ckbuf.at[sl, g, pl.ds(p * page, page), :],
                                      sem.at[0, sl]).start()
                pltpu.make_async_copy(kr_hbm.at[phys], krbuf.at[sl, g, p], sem.at[1, sl]).start()

    @pl.when(step == 0)
    def _():
        for ahead in range(PAGE_RING - 1):
            @pl.when(ahead < n_steps)
            def _():
                start_chunk(step + ahead)

    pltpu.make_async_copy(ckbuf.at[slot], ckbuf.at[slot], sem.at[0, slot]).wait()
    pltpu.make_async_copy(krbuf.at[slot], krbuf.at[slot], sem.at[1, slot]).wait()

    @pl.when(c == 0)
    def _():
        m_ref[...] = jnp.full(m_ref.shape, -jnp.inf, F32)
        l_ref[...] = jnp.zeros(l_ref.shape, F32)
        acc_ref[...] = jnp.zeros(acc_ref.shape, F32)

    def absorb(g, s, values):
        m = m_ref[g]
        m_new = jnp.maximum(m, jnp.max(s, axis=1, keepdims=True))
        alpha = jnp.exp(m - m_new)
        p = jnp.exp(s - m_new)
        l_ref[g] = alpha * l_ref[g] + jnp.sum(p, axis=1, keepdims=True)
        acc_ref[g] = alpha * acc_ref[g] + _dot(p.astype(BF16), values)
        m_ref[g] = m_new

    cks, scores = [], []
    for g in range(group):
        qa = q_ref[g]
        q_lat = qa[:, :kv_lora].astype(BF16)
        q_rope = qa[:, kv_lora:kv_lora + QK_ROPE].astype(BF16)
        ck = ckbuf[slot, g].astype(BF16)
        kr_t = jnp.concatenate([krbuf[slot, g, p] for p in range(pages)], axis=1).astype(BF16)
        cks.append(ck)
        scores.append(_dot_nt(q_lat, ck) + _dot(q_rope, kr_t))

    nxt = step + PAGE_RING - 1

    @pl.when(nxt < n_steps)
    def _():
        start_chunk(nxt)

    for g in range(group):
        absorb(g, scores[g], cks[g])

    @pl.when(c == n_chunks - 1)
    def _():
        rows = q_ref.shape[1]
        t_q = lax.broadcasted_iota(jnp.int32, (rows, t_pad), 0) % t_new
        t_k = lax.broadcasted_iota(jnp.int32, (rows, t_pad), 1)
        for g in range(group):
            qa = q_ref[g]
            q_lat = qa[:, :kv_lora].astype(BF16)
            q_rope = qa[:, kv_lora:kv_lora + QK_ROPE].astype(BF16)
            cn = cn_ref[g].astype(BF16)
            s = _dot_nt(q_lat, cn) + _dot_nt(q_rope, rn_ref[g].astype(BF16))
            absorb(g, jnp.where(t_k <= t_q, s, -jnp.inf), cn)
            o_ref[g] = acc_ref[g] / l_ref[g]


def _paged_attn(page_table, qabs, ckv_new, kr_new, cache_ck, cache_kr_t, *, group, pages, t_new):
    b, rows, width = qabs.shape
    n_pages = page_table.shape[1]
    n_chunks = n_pages // pages
    page, kv_lora = cache_ck.shape[1:]
    t_pad = ckv_new.shape[1]
    grid_spec = pltpu.PrefetchScalarGridSpec(
        num_scalar_prefetch=1,
        grid=(b // group, n_chunks),
        in_specs=[pl.BlockSpec((group, rows, width), lambda bi, ci, pt: (bi, 0, 0)),
                  pl.BlockSpec((group, t_pad, kv_lora), lambda bi, ci, pt: (bi, 0, 0)),
                  pl.BlockSpec((group, t_pad, QK_ROPE), lambda bi, ci, pt: (bi, 0, 0)),
                  pl.BlockSpec(memory_space=pl.ANY),
                  pl.BlockSpec(memory_space=pl.ANY)],
        out_specs=pl.BlockSpec((group, rows, kv_lora), lambda bi, ci, pt: (bi, 0, 0)),
        scratch_shapes=[pltpu.VMEM((PAGE_RING, group, pages * page, kv_lora), F32),
                        pltpu.VMEM((PAGE_RING, group, pages, QK_ROPE, page), F32),
                        pltpu.VMEM((group, rows, 1), F32), pltpu.VMEM((group, rows, 1), F32),
                        pltpu.VMEM((group, rows, kv_lora), F32),
                        pltpu.SemaphoreType.DMA((2, PAGE_RING))],
    )
    return pl.pallas_call(
        functools.partial(_paged_attn_kernel, group=group, pages=pages, page=page, n_chunks=n_chunks, t_new=t_new,
                          t_pad=t_pad),
        grid_spec=grid_spec,
        out_shape=jax.ShapeDtypeStruct((b, rows, kv_lora), F32),
        compiler_params=_cparams(2),
        name="paged_attn",
    )(page_table, qabs, ckv_new, kr_new, cache_ck, cache_kr_t)


def _vup_kernel(o_ref, w_ref, out_ref):
    lat = jnp.concatenate([o_ref[0], o_ref[1]], axis=-1).astype(BF16)
    out_ref[...] = _dot(lat, w_ref[0]).astype(out_ref.dtype)


def _vup(o_lat, w_pairs):
    h, n, c = o_lat.shape
    return pl.pallas_call(
        _vup_kernel,
        grid=(h // 2,),
        in_specs=[pl.BlockSpec((2, n, c), lambda p: (p, 0, 0)),
                  pl.BlockSpec((1, 2 * c, 2 * V_DIM), lambda p: (p, 0, 0))],
        out_specs=pl.BlockSpec((n, 2 * V_DIM), lambda p: (0, p)),
        out_shape=jax.ShapeDtypeStruct((n, h * V_DIM), BF16),
        compiler_params=_cparams(1),
        name="sample_vup",
    )(o_lat, w_pairs)


def _block_diag(w):
    g, a, b = w.shape
    out = jnp.zeros((g * a, g * b), w.dtype)
    for i in range(g):
        out = out.at[i * a:(i + 1) * a, i * b:(i + 1) * b].set(w[i])
    return out


def _swap_halves(w):
    half = w.shape[-1] // 2
    return jnp.concatenate([w[..., half:], w[..., :half]], axis=-1)


def _rope_tables(pos):
    half = QK_ROPE // 2
    inv_freq = ROPE_THETA ** (-jnp.arange(half, dtype=F32) / half)
    ang = pos.astype(F32)[:, None] * inv_freq[None, :]
    cos, sin = jnp.cos(ang), jnp.sin(ang)
    n = pos.shape[0]
    ctab = jnp.concatenate([jnp.ones((n, QK_NOPE), F32), cos, cos, jnp.zeros((n, QK_ROPE), F32)], axis=1)
    stab = jnp.concatenate([jnp.zeros((n, QK_NOPE), F32), -sin, sin, jnp.zeros((n, QK_ROPE), F32)], axis=1)
    return ctab, stab


def _router_weights(w_group, b_group, w_expert, b_expert):
    d = w_group.shape[0]
    we = jnp.transpose(w_expert, (1, 0, 2)).reshape(d, N_EXPERTS)
    w = jnp.concatenate([w_group, we, jnp.zeros((d, LANES - MOE_GROUPS - N_EXPERTS), F32)], axis=1)
    bias = jnp.concatenate([b_group, b_expert.reshape(-1), jnp.zeros((LANES - MOE_GROUPS - N_EXPERTS,), F32)])
    hi = w.astype(BF16)
    lo = (w - hi.astype(F32)).astype(BF16)
    return hi, lo, bias.reshape(1, LANES)


def _pick(n, pref):
    t = min(n, pref)
    while n % t:
        t //= 2
    return t


def kernel(x_prompt, x_sample, state_pool, state_conv, state_lru, cache_kv_latent, cache_k_rope, page_table,
           c_prompt, c_sample, ada_w, ada_b, norm_mix, norm_ffn, ab_w_in, ab_w_out, pool_w, pool_scale,
           conv_w, conv_b, lru_w_r, lru_b_r, lru_w_i, lru_b_i, lru_lambda,
           mla_w_in, mla_q_norm, mla_w_uq, mla_kv_norm, mla_w_uk, mla_w_uv, mla_w_out,
           router_w_group, router_b_group, router_w_expert, router_b_expert,
           moe_w_gate, moe_w_up, moe_w_down, final_norm):
    bp, seq, d = x_prompt.shape
    bs, t_new, _ = x_sample.shape
    depth = ada_w.shape[0]
    pw = pool_scale.shape[-1]
    page = cache_kv_latent.shape[2]
    past_len = page_table.shape[1] * page
    q_lora = mla_q_norm.shape[-1]
    kv_lora = mla_kv_norm.shape[-1]

    n_c = bp + bs
    n_c_pad = -(-n_c // 8) * 8
    c_all = jnp.concatenate([c_prompt, c_sample, jnp.zeros((n_c_pad - n_c, d), F32)], axis=0)
    mod = _ada(c_all, ada_w, ada_b)

    def mods(layer, lo, hi):
        return [mod[layer, lo:hi, k * d:(k + 1) * d] for k in range(6)]

    n_p = bp * seq
    n_s = bs * t_new
    tile_p = _pick(seq, 512)
    tile_s = _pick(n_s, 512)
    gmm_tile_p = _pick(TOP_K * n_p, 512)
    gmm_tile_s = _pick(TOP_K * n_s, 256)

    xp = x_prompt.reshape(n_p, d)
    xs = x_sample.reshape(n_s, d)
    outs = {}

    for layer in range(depth):
        sh_m, sc_m, gt_m, sh_f, sc_f, gt_f = mods(layer, 0, bp)
        sh_ms, sc_ms, gt_ms, sh_fs, sc_fs, gt_fs = mods(layer, bp, bp + bs)
        rep = lambda v: jnp.repeat(v, t_new, axis=0)[None]
        per_seq = lambda v: v[:, None, :]
        rw = (norm_ffn[layer].reshape(1, d),) + _router_weights(
            router_w_group[layer], router_b_group[layer], router_w_expert[layer], router_b_expert[layer])
        experts = (moe_w_gate, moe_w_up, moe_w_down)
        final_gain = final_norm.reshape(1, d) if layer == depth - 1 else None
        ffn_mods_p = (per_seq(sh_f), per_seq(sc_f), per_seq(gt_f))
        ffn_mods_s = (rep(sh_fs), rep(sc_fs), rep(gt_fs))

        if layer % 2 == 0:
            e = layer // 2
            wts = (norm_mix[layer].reshape(1, d), ab_w_in[e].astype(BF16), _block_diag(pool_w[e]).astype(BF16),
                   pool_scale[e].reshape(1, pw), conv_w[e], conv_b[e].reshape(1, pw),
                   jnp.concatenate([_block_diag(lru_w_r[e]), _block_diag(lru_w_i[e])], axis=1).astype(BF16),
                   jnp.concatenate([lru_b_r[e], lru_b_i[e]]).reshape(1, 2 * pw),
                   lru_lambda[e].reshape(1, pw), ab_w_out[e].astype(BF16))
            x1p, pool_p, conv_p, lru_p = _mix0(
                xp.reshape(bp, seq, d), per_seq(sh_m), per_seq(sc_m), per_seq(gt_m),
                jnp.zeros((bp, POOL_HALO - 1, pw), F32), jnp.zeros((bp, CONV_WIDTH - 1, pw), F32),
                jnp.zeros((bp, 1, pw), F32), wts, tt=tile_p, bb=1, start=0)
            tm = lambda a: jnp.swapaxes(a, 0, 1).reshape(1, -1, a.shape[-1])
            x1s, pool_s, conv_s, lru_s = _mix0(
                tm(xs.reshape(bs, t_new, d)), sh_ms[None], sc_ms[None], gt_ms[None],
                tm(state_pool[e]), tm(state_conv[e]), state_lru[e][None], wts, tt=t_new, bb=bs, start=past_len)
            bm = lambda a, n: jnp.swapaxes(a.reshape(n, bs, a.shape[-1]), 0, 1)
            outs.setdefault("pool_p", []).append(pool_p)
            outs.setdefault("pool_s", []).append(bm(pool_s, POOL_HALO - 1))
            outs.setdefault("conv_p", []).append(conv_p)
            outs.setdefault("conv_s", []).append(bm(conv_s, CONV_WIDTH - 1))
            outs.setdefault("lru_p", []).append(lru_p.reshape(bp, pw))
            outs.setdefault("lru_s", []).append(lru_s.reshape(bs, pw))
            xp = x1p.reshape(n_p, d)
            xs = bm(x1s, t_new).reshape(n_s, d)
            proj_p = proj_s = None
        else:
            o = layer // 2
            gain = norm_mix[layer].reshape(1, d)
            w_in = mla_w_in[o]
            w_kr = w_in[:, q_lora + kv_lora:]
            win = jnp.concatenate([w_in[:, :q_lora + kv_lora], jnp.zeros((d, QK_NOPE), F32), w_kr,
                                   _swap_halves(w_kr)], axis=1).astype(BF16)
            wq = mla_w_uq[o].reshape(q_lora, MLA_HEADS, QK_NOPE + QK_ROPE)
            wuq = jnp.concatenate([wq, _swap_halves(wq[..., QK_NOPE:])], axis=-1).reshape(
                q_lora, MLA_HEADS * HEAD_PAD).astype(BF16)
            wuk = jnp.concatenate([mla_w_uk[o], jnp.zeros((kv_lora, MLA_HEADS, HEAD_PAD - QK_NOPE), F32)],
                                  axis=-1).reshape(kv_lora, MLA_HEADS * HEAD_PAD).astype(BF16)
            wuv_t = jnp.concatenate([mla_w_uv[o], jnp.zeros((kv_lora, MLA_HEADS, HEAD_PAD - V_DIM), F32)],
                                    axis=-1).reshape(kv_lora, MLA_HEADS * HEAD_PAD).T.astype(BF16)
            vone_t = jnp.tile((jnp.arange(HEAD_PAD) == V_DIM).astype(F32), MLA_HEADS).reshape(-1, 1)
            qg = mla_q_norm[o].reshape(1, q_lora)
            kvg = mla_kv_norm[o].reshape(1, kv_lora)
            wout = mla_w_out[o].astype(BF16)

            ctab, stab = _rope_tables(jnp.arange(seq, dtype=jnp.int32))
            ckv_p, kr_p, qt_p, k_p, vt_p = _mla_proj(
                xp, per_seq(sh_m), per_seq(sc_m), gain, win, qg, wuq.T, kvg, ctab, stab,
                tile=tile_p, rows_per_mod=seq, tab_tiles=seq // tile_p, kv_w=(wuk, wuv_t, vone_t, ctab.T, stab.T))
            qw = MLA_HEADS * HEAD_PAD
            o_p = _attention(qt_p, k_p.reshape(bp, seq, qw), vt_p, tq=tile_p, tk=tile_p)
            proj_p = (o_p.reshape(n_p, MLA_HEADS * V_DIM), wout, per_seq(gt_m))
            outs.setdefault("lat_p", []).append(ckv_p.reshape(bp, seq, kv_lora))
            outs.setdefault("rope_p", []).append(kr_p.reshape(bp, seq, QK_ROPE))

            pos_s = jnp.tile(past_len + jnp.arange(t_new, dtype=jnp.int32), bs)
            ctab_s, stab_s = _rope_tables(pos_s)
            ckv_s, kr_s, q_s = _mla_proj(
                xs, rep(sh_ms), rep(sc_ms), gain, win, qg, wuq, kvg, ctab_s, stab_s,
                tile=tile_s, rows_per_mod=n_s, tab_tiles=n_s // tile_s)
            aw = kv_lora + HEAD_PAD
            wk_t = jnp.transpose(mla_w_uk[o], (1, 2, 0))
            mats = jnp.zeros((MLA_HEADS, HEAD_PAD, aw), F32)
            mats = mats.at[:, :QK_NOPE, :kv_lora].set(wk_t)
            mats = mats.at[:, QK_NOPE:QK_NOPE + QK_ROPE, kv_lora:kv_lora + QK_ROPE].set(
                jnp.broadcast_to(jnp.eye(QK_ROPE, dtype=F32), (MLA_HEADS, QK_ROPE, QK_ROPE)))
            qabs = _qabs(q_s, mats.astype(BF16))
            qabs = jnp.transpose(qabs.reshape(MLA_HEADS, bs, t_new, aw), (1, 0, 2, 3)).reshape(
                bs, MLA_HEADS * t_new, aw)
            t_pad = 8
            pad_t = lambda a: jnp.pad(a.reshape(bs, t_new, -1), ((0, 0), (0, t_pad - t_new), (0, 0)))
            o_lat = _paged_attn(page_table, qabs, pad_t(ckv_s), pad_t(kr_s), cache_kv_latent[o],
                                jnp.swapaxes(cache_k_rope[o], 1, 2),
                                group=_pick(bs, 4), pages=_pick(page_table.shape[1], 8), t_new=t_new)
            o_lat = jnp.transpose(o_lat.reshape(bs, MLA_HEADS, t_new, kv_lora), (1, 0, 2, 3)).reshape(
                MLA_HEADS, n_s, kv_lora)
            wv = jnp.transpose(mla_w_uv[o], (1, 0, 2))
            w_pairs = jnp.zeros((MLA_HEADS // 2, 2 * kv_lora, 2 * V_DIM), F32)
            w_pairs = w_pairs.at[:, :kv_lora, :V_DIM].set(wv[0::2]).at[:, kv_lora:, V_DIM:].set(wv[1::2])
            o_s = _vup(o_lat, w_pairs.astype(BF16))
            proj_s = (o_s, wout, rep(gt_ms))
            outs.setdefault("lat_s", []).append(ckv_s.reshape(bs, t_new, kv_lora))
            outs.setdefault("rope_s", []).append(kr_s.reshape(bs, t_new, QK_ROPE))

        xp = _moe(xp, ffn_mods_p, rw, experts, layer=layer, tile=tile_p, gmm_tile=gmm_tile_p, rows_per_mod=seq,
                  proj=proj_p, final_gain=final_gain)
        xs = _moe(xs, ffn_mods_s, rw, experts, layer=layer, tile=tile_s, gmm_tile=gmm_tile_s, rows_per_mod=n_s,
                  proj=proj_s, final_gain=final_gain)

    st = lambda k: jnp.stack(outs[k])
    return (xp.reshape(bp, seq, d), xs.reshape(bs, t_new, d),
            st("pool_p"), st("pool_s"), st("conv_p"), st("conv_s"), st("lru_p"), st("lru_s"),
            st("lat_p"), st("lat_s"), st("rope_p"), st("rope_s"))
```

```python
import functools

import jax
import jax.numpy as jnp
from jax import lax
from jax.experimental import pallas as pl
from jax.experimental.pallas import tpu as pltpu

F32 = jnp.float32
BF16 = jnp.bfloat16

EPS = 1e-6
POOL_WINDOWS = (2, 4, 8, 16)
POOL_HALO = 16
CONV_WIDTH = 4
CONV_HALO = 8
LRU_C = 8.0
MLA_HEADS = 16
QK_NOPE = 64
QK_ROPE = 32
V_DIM = 64
HEAD_PAD = 128
ROPE_THETA = 10000.0
SM_SCALE = (QK_NOPE + QK_ROPE) ** -0.5
MOE_GROUPS = 4
EXPERTS_PER_GROUP = 8
N_EXPERTS = MOE_GROUPS * EXPERTS_PER_GROUP
TOP_K = 2
LANES = 128
ROW_DMA_UNROLL = 8
PAGE_RING = 3
ATTN_PAIRS_PER_TRIP = 4
LOG2E = 1.4426950408889634
VMEM_LIMIT = 56 * 1024 * 1024


def _cparams(n_axes):
    return pltpu.CompilerParams(dimension_semantics=("arbitrary",) * n_axes,
                                vmem_limit_bytes=VMEM_LIMIT)


def _dot(a, b):
    return jnp.dot(a, b, preferred_element_type=F32)


def _dot_nt(a, b):
    return lax.dot_general(a, b, (((1,), (1,)), ((), ())), preferred_element_type=F32)


def _rms(x, gain):
    return x * lax.rsqrt(jnp.mean(x * x, axis=-1, keepdims=True) + EPS) * gain


def _silu(x):
    return x * jax.nn.sigmoid(x)


def _ada_kernel(c_ref, w_ref, b_ref, o_ref):
    a = _silu(c_ref[...]).astype(BF16)
    o_ref[0] = _dot(a, w_ref[0].astype(BF16)) + b_ref[0]


def _ada(c_all, ada_w, ada_b):
    depth, d, n6 = ada_w.shape
    rows = c_all.shape[0]
    tn = 1024
    return pl.pallas_call(
        _ada_kernel,
        grid=(depth, n6 // tn),
        in_specs=[pl.BlockSpec((rows, d), lambda l, j: (0, 0)),
                  pl.BlockSpec((1, d, tn), lambda l, j: (l, 0, j)),
                  pl.BlockSpec((1, 1, tn), lambda l, j: (l, 0, j))],
        out_specs=pl.BlockSpec((1, rows, tn), lambda l, j: (l, 0, j)),
        out_shape=jax.ShapeDtypeStruct((depth, rows, n6), F32),
        compiler_params=_cparams(2),
        name="ada_mod",
    )(c_all, ada_w, ada_b.reshape(depth, 1, n6))


def _mix0_kernel(x_ref, sh_ref, sc_ref, gt_ref, pool0_ref, conv0_ref, lru0_ref,
                 gain_ref, win_ref, poolw_ref, pscale_ref, convw_ref, convb_ref,
                 wri_ref, bri_ref, lam_ref, wout_ref,
                 x1_ref, pooln_ref, convn_ref, lrun_ref,
                 zp_ref, zc_ref, h_ref, *, tt, bb, start, n_t):
    t = pl.program_id(1)
    rows = tt * bb
    pw = zp_ref.shape[1]
    p0 = POOL_HALO * bb
    c0 = CONV_HALO * bb

    @pl.when(t == 0)
    def _():
        zp_ref[0:bb, :] = jnp.zeros((bb, pw), F32)
        zp_ref[bb:p0, :] = pool0_ref[0]
        zc_ref[0:c0 - (CONV_WIDTH - 1) * bb, :] = jnp.zeros((c0 - (CONV_WIDTH - 1) * bb, pw), F32)
        zc_ref[c0 - (CONV_WIDTH - 1) * bb:c0, :] = conv0_ref[0]
        h_ref[...] = lru0_ref[0]

    def per_row(v):
        return v if bb == 1 else jnp.concatenate([v] * tt, axis=0)

    x = x_ref[0]
    h = _rms(x, gain_ref[...]) * (1.0 + per_row(sc_ref[0])) + per_row(sh_ref[0])
    proj = _dot(h.astype(BF16), win_ref[...])
    u_pool = proj[:, :pw]
    u_x = proj[:, pw:2 * pw]
    u_g = proj[:, 2 * pw:]
    zp_ref[p0:p0 + rows, :] = u_pool
    zc_ref[c0:c0 + rows, :] = u_x

    if bb == 1:
        tix = lax.broadcasted_iota(jnp.int32, (rows, 1), 0)
    else:
        tix = jnp.concatenate([jnp.full((bb, 1), i, jnp.int32) for i in range(tt)], axis=0)
    pos = start + t * tt + tix

    gd = pw // len(POOL_WINDOWS)
    means = []
    if bb == 1 and POOL_WINDOWS == (2, 4, 8, 16):
        sw = zp_ref[0:p0 + rows, :]
        for g, w in enumerate(POOL_WINDOWS):
            sw = sw + pltpu.roll(sw, w // 2, 0)
            cnt = jnp.minimum(pos + 1, w).astype(F32)
            means.append(sw[p0:, :gd] / cnt)
            if g + 1 < len(POOL_WINDOWS):
                sw = sw[:, gd:]
    else:
        for g, w in enumerate(POOL_WINDOWS):
            cols = slice(g * gd, (g + 1) * gd)
            acc = zp_ref[p0:p0 + rows, cols]
            for i in range(1, w):
                acc = acc + zp_ref[p0 - i * bb:p0 - i * bb + rows, cols]
            cnt = jnp.minimum(pos + 1, w).astype(F32)
            means.append(acc / cnt)
    pooled = jnp.concatenate(means, axis=-1) - u_pool
    y_a = _dot(pooled.astype(BF16), poolw_ref[...]) * pscale_ref[...]

    xc = convb_ref[...]
    if bb == 1:
        z_al = zc_ref[0:c0 + rows, :]
        for k in range(CONV_WIDTH):
            back = CONV_WIDTH - 1 - k
            z_k = z_al if back == 0 else pltpu.roll(z_al, back, 0)
            xc = xc + z_k[c0:, :] * convw_ref[k:k + 1, :]
    else:
        for k in range(CONV_WIDTH):
            off = c0 - (CONV_WIDTH - 1 - k) * bb
            xc = xc + zc_ref[off:off + rows, :] * convw_ref[k:k + 1, :]
    pre = _dot(xc.astype(BF16), wri_ref[...]) + bri_ref[...]
    r = jax.nn.sigmoid(pre[:, :pw])
    gi = jax.nn.sigmoid(pre[:, pw:])
    lam = lam_ref[...]
    softplus_neg = jnp.maximum(-lam, 0.0) + jnp.log1p(jnp.exp(-jnp.abs(lam)))
    log_a = -LRU_C * r * softplus_neg
    a = jnp.exp(log_a)
    one_m = 1.0 - a * a
    b = jnp.where(one_m > 0.0, one_m * lax.rsqrt(one_m), 0.0) * gi * xc

    if bb == 1:
        rowi = lax.broadcasted_iota(jnp.int32, (rows, 1), 0)
        s = 1
        while s < rows:
            if s < 8:
                keep = rowi >= s
                b = jnp.where(keep, a * pltpu.roll(b, s, 0) + b, b)
                a = jnp.where(keep, a * pltpu.roll(a, s, 0), a)
            else:
                b = jnp.concatenate([b[:s], a[s:] * b[:rows - s] + b[s:]], axis=0)
                a = jnp.concatenate([a[:s], a[s:] * a[:rows - s]], axis=0)
            s *= 2
        hs = b + a * h_ref[...]
        h_ref[...] = hs[rows - 1:rows, :]
    else:
        hprev = h_ref[...]
        parts = []
        for i in range(tt):
            hprev = a[i * bb:(i + 1) * bb] * hprev + b[i * bb:(i + 1) * bb]
            parts.append(hprev)
        hs = jnp.concatenate(parts, axis=0)
        h_ref[...] = hprev
    y_b = hs * jax.nn.gelu(u_g)

    mix = _dot(y_a.astype(BF16), wout_ref[0:pw, :]) + _dot(y_b.astype(BF16), wout_ref[pw:2 * pw, :])
    x1_ref[0] = x + per_row(gt_ref[0]) * mix

    @pl.when(t == n_t - 1)
    def _():
        pooln_ref[0] = zp_ref[p0 + rows - (POOL_HALO - 1) * bb:p0 + rows, :]
        convn_ref[0] = zc_ref[c0 + rows - (CONV_WIDTH - 1) * bb:c0 + rows, :]
        lrun_ref[0] = h_ref[...]

    if n_t > 1:
        @pl.when(t < n_t - 1)
        def _():
            zp_ref[bb:p0, :] = zp_ref[rows + bb:rows + p0, :]
            zc_ref[c0 - (CONV_WIDTH - 1) * bb:c0, :] = zc_ref[c0 + rows - (CONV_WIDTH - 1) * bb:c0 + rows, :]


def _mix0(x, sh, sc, gt, pool0, conv0, lru0, wts, *, tt, bb, start):
    nbb, tot, d = x.shape
    n_t = tot // (tt * bb)
    rows = tt * bb
    pw = pool0.shape[-1]
    hp = (POOL_HALO - 1) * bb
    hc = (CONV_WIDTH - 1) * bb

    def const(a):
        nd = a.ndim
        return pl.BlockSpec(a.shape, lambda i, j: (0,) * nd)

    def per_b(n_rows, width):
        return pl.BlockSpec((1, n_rows, width), lambda i, j: (i, 0, 0))

    kern = functools.partial(_mix0_kernel, tt=tt, bb=bb, start=start, n_t=n_t)
    return pl.pallas_call(
        kern,
        grid=(nbb, n_t),
        in_specs=[pl.BlockSpec((1, rows, d), lambda i, j: (i, j, 0)),
                  per_b(bb, d), per_b(bb, d), per_b(bb, d),
                  per_b(hp, pw), per_b(hc, pw), per_b(bb, pw)] + [const(w) for w in wts],
        out_specs=[pl.BlockSpec((1, rows, d), lambda i, j: (i, j, 0)),
                   per_b(hp, pw), per_b(hc, pw), per_b(bb, pw)],
        out_shape=[jax.ShapeDtypeStruct((nbb, tot, d), F32),
                   jax.ShapeDtypeStruct((nbb, hp, pw), F32),
                   jax.ShapeDtypeStruct((nbb, hc, pw), F32),
                   jax.ShapeDtypeStruct((nbb, bb, pw), F32)],
        scratch_shapes=[pltpu.VMEM(((POOL_HALO + tt) * bb, pw), F32),
                        pltpu.VMEM(((CONV_HALO + tt) * bb, pw), F32),
                        pltpu.VMEM((bb, pw), F32)],
        compiler_params=_cparams(2),
        name="mix0",
    )(x, sh, sc, gt, pool0, conv0, lru0, *wts)


ROUTE_ROWS = 40
def _route_kernel(*refs, with_proj):
    if with_proj:
        (x_ref, o_ref, wout_ref, gtm_ref, shf_ref, scf_ref, gain_ref, wrh_ref, wrl_ref, br_ref,
         x1_ref, hf_ref, info_ref, meta_ref, cnt_ref, run_ref) = refs
    else:
        (x_ref, shf_ref, scf_ref, gain_ref, wrh_ref, wrl_ref, br_ref,
         hf_ref, info_ref, meta_ref, cnt_ref, run_ref) = refs

    @pl.when(pl.program_id(0) == 0)
    def _():
        run_ref[...] = jnp.zeros(run_ref.shape, F32)

    x = x_ref[...]
    if with_proj:
        x = x + gtm_ref[0] * _dot(o_ref[...], wout_ref[...])
        x1_ref[...] = x
    hf = _rms(x, gain_ref[...]) * (1.0 + scf_ref[0]) + shf_ref[0]
    hf_ref[...] = hf

    hi = hf.astype(BF16)
    lo = (hf - hi.astype(F32)).astype(BF16)
    logits = _dot(hi, wrh_ref[...]) + _dot(lo, wrh_ref[...]) + _dot(hi, wrl_ref[...]) + br_ref[...]

    tq = logits.shape[0]
    lt = logits.T[:ROUTE_ROWS]
    row = lax.broadcasted_iota(jnp.int32, (ROUTE_ROWS, tq), 0).astype(F32)
    neg = -jnp.inf
    big = float(ROUTE_ROWS)
    is_g = row < MOE_GROUPS
    gl = jnp.where(is_g, lt, neg)
    mg = jnp.max(gl, axis=0, keepdims=True)
    gidx = jnp.min(jnp.where(gl == mg, row, big), axis=0, keepdims=True)
    p_group = 1.0 / jnp.sum(jnp.where(is_g, jnp.exp(gl - mg), 0.0), axis=0, keepdims=True)
    first = MOE_GROUPS + gidx * EXPERTS_PER_GROUP
    el = jnp.where((row >= first) & (row < first + EXPERTS_PER_GROUP), lt, neg)
    v1 = jnp.max(el, axis=0, keepdims=True)
    i1 = jnp.min(jnp.where(el == v1, row, big), axis=0, keepdims=True)
    el2 = jnp.where(row == i1, neg, el)
    v2 = jnp.max(el2, axis=0, keepdims=True)
    i2 = jnp.min(jnp.where(el2 == v2, row, big), axis=0, keepdims=True)
    ex = jnp.exp(v2 - v1)
    g1 = p_group / (1.0 + ex)
    g2 = p_group * ex / (1.0 + ex)
    e1 = i1 - MOE_GROUPS
    e2 = i2 - MOE_GROUPS

    erow = lax.broadcasted_iota(jnp.int32, (N_EXPERTS, tq), 0).astype(F32)
    oh1 = jnp.where(erow == e1, 1.0, 0.0)
    oh2 = jnp.where(erow == e2, 1.0, 0.0)
    oh = oh1 + oh2
    ri = lax.broadcasted_iota(jnp.int32, (tq, tq), 0)
    ci = lax.broadcasted_iota(jnp.int32, (tq, tq), 1)
    earlier = jnp.where(ri < ci, 1.0, 0.0).astype(BF16)
    before = _dot(oh.astype(BF16), earlier) + run_ref[...]
    r1 = jnp.sum(before * oh1, axis=0, keepdims=True)
    r2 = jnp.sum(before * oh2, axis=0, keepdims=True)
    run_ref[...] = run_ref[...] + jnp.sum(oh, axis=1, keepdims=True)
    cnt_ref[...] = jnp.broadcast_to(run_ref[...], cnt_ref.shape)

    meta = jnp.concatenate([e1, e2, g1, g2, r1, r2, jnp.zeros((2, tq), F32)], axis=0)
    meta_ref[0] = meta
    info_ref[...] = jnp.concatenate([meta, jnp.zeros((LANES - 8, tq), F32)], axis=0).T


def _route(x, shf, scf, gain, wrh, wrl, br, *, tile, rows_per_mod, proj=None):
    n, d = x.shape
    nt = n // tile
    mrows = shf.shape[1]
    mod_spec = pl.BlockSpec((1, mrows, d), lambda i: ((i * tile) // rows_per_mod, 0, 0))
    row_spec = pl.BlockSpec((tile, d), lambda i: (i, 0))

    def const(a):
        nd = a.ndim
        return pl.BlockSpec(a.shape, lambda i: (0,) * nd)

    info_spec = pl.BlockSpec((tile, LANES), lambda i: (i, 0))
    meta_spec = pl.BlockSpec((1, 8, tile), lambda i: (i, 0, 0))
    cnt_spec = pl.BlockSpec((N_EXPERTS, LANES), lambda i: (0, 0))
    outs_shape = [jax.ShapeDtypeStruct((n, d), F32), jax.ShapeDtypeStruct((n, LANES), F32),
                  jax.ShapeDtypeStruct((nt, 8, tile), F32), jax.ShapeDtypeStruct((N_EXPERTS, LANES), F32)]
    outs_spec = [row_spec, info_spec, meta_spec, cnt_spec]
    if proj is None:
        ins = [x, shf, scf, gain, wrh, wrl, br]
        in_specs = [row_spec, mod_spec, mod_spec, const(gain), const(wrh), const(wrl), const(br)]
    else:
        o, wout, gtm = proj
        ins = [x, o, wout, gtm, shf, scf, gain, wrh, wrl, br]
        in_specs = [row_spec, pl.BlockSpec((tile, o.shape[1]), lambda i: (i, 0)), const(wout), mod_spec,
                    mod_spec, mod_spec, const(gain), const(wrh), const(wrl), const(br)]
        outs_shape = [jax.ShapeDtypeStruct((n, d), F32)] + outs_shape
        outs_spec = [row_spec] + outs_spec
    return pl.pallas_call(
        functools.partial(_route_kernel, with_proj=proj is not None),
        grid=(nt,),
        in_specs=in_specs,
        out_specs=outs_spec,
        out_shape=outs_shape,
        scratch_shapes=[pltpu.VMEM((N_EXPERTS, 1), F32)],
        compiler_params=_cparams(1),
        name="route",
    )(*ins)


def _dispatch_kernel(*refs, tiles, first_steps):
    xs_ref, sem = refs[-2:]
    i = pl.program_id(0)

    def copy_tile(slot_ref, hf_ref, tile):
        def issue(g, c):
            for u in range(ROW_DMA_UNROLL):
                r = g * ROW_DMA_UNROLL + u
                for k in range(TOP_K):
                    s = slot_ref[0, 0, k * tile + r]
                    pltpu.make_async_copy(hf_ref.at[pl.ds(r, 1), :], xs_ref.at[pl.ds(s, 1), :], sem).start()
            return c

        lax.fori_loop(0, tile // ROW_DMA_UNROLL, issue, 0)
        for k in range(TOP_K):
            pltpu.make_async_copy(hf_ref, xs_ref.at[pl.ds(0, tile), :], sem).wait()

    for g, tile in enumerate(tiles):
        lo, hi = first_steps[g], first_steps[g + 1]

        @pl.when((i >= lo) & (i < hi))
        def _(g=g, tile=tile):
            copy_tile(refs[2 * g], refs[2 * g + 1], tile)


def _dispatch(hfs, slots, tiles):
    d = hfs[0].shape[1]
    n_tiles = [hf.shape[0] // t for hf, t in zip(hfs, tiles)]
    first_steps = [sum(n_tiles[:g]) for g in range(len(hfs) + 1)]
    ins, in_specs = [], []
    for g, (hf, sl, tile) in enumerate(zip(hfs, slots, tiles)):
        def blk(i, lo=first_steps[g], last=n_tiles[g] - 1):
            return jnp.clip(i - lo, 0, last)

        ins += [sl.reshape(n_tiles[g], 1, TOP_K * tile), hf]
        in_specs += [pl.BlockSpec((1, 1, TOP_K * tile), lambda i, blk=blk: (blk(i), 0, 0),
                                  memory_space=pltpu.SMEM),
                     pl.BlockSpec((tile, d), lambda i, blk=blk: (blk(i), 0))]
    return pl.pallas_call(
        functools.partial(_dispatch_kernel, tiles=tuple(tiles), first_steps=tuple(first_steps)),
        grid=(first_steps[-1],),
        in_specs=in_specs,
        out_specs=pl.BlockSpec(memory_space=pl.ANY),
        out_shape=jax.ShapeDtypeStruct((TOP_K * sum(hf.shape[0] for hf in hfs), d), F32),
        scratch_shapes=[pltpu.SemaphoreType.DMA(())],
        compiler_params=_cparams(1),
        name="moe_dispatch",
    )(*ins)


def _gmm_kernel(wt_ref, we_ref, wlo_ref, whi_ref, xs_ref, wg_ref, wu_ref, wd_ref, y_ref,
                wgb_ref, wub_ref, wdb_ref):
    w = pl.program_id(0)
    prev = jnp.maximum(w - 1, 0)
    new_expert = (w == 0) | (we_ref[w] != we_ref[prev])
    new_tile = (w == 0) | (wt_ref[w] != wt_ref[prev])

    @pl.when(new_expert)
    def _():
        wgb_ref[...] = wg_ref[0, 0].astype(BF16)
        wub_ref[...] = wu_ref[0, 0].astype(BF16)
        wdb_ref[...] = wd_ref[0, 0].astype(BF16)

    @pl.when(new_tile)
    def _():
        y_ref[...] = jnp.zeros(y_ref.shape, F32)

    lo = wlo_ref[w]
    hi = whi_ref[w]

    @pl.when(hi > lo)
    def _():
        x = xs_ref[...].astype(BF16)
        g = _dot(x, wgb_ref[...])
        u = _dot(x, wub_ref[...])
        yv = _dot((_silu(g) * u).astype(BF16), wdb_ref[...])
        row = lax.broadcasted_iota(jnp.int32, (x.shape[0], 1), 0)
        y_ref[...] = y_ref[...] + jnp.where((row >= lo) & (row < hi), yv, 0.0)


def _gmm(xs, work, w_gate, w_up, w_down, *, tile, layer):
    m, d = xs.shape
    ff = w_gate.shape[-1]
    n_work = work[0].shape[0]
    grid_spec = pltpu.PrefetchScalarGridSpec(
        num_scalar_prefetch=4,
        grid=(n_work,),
        in_specs=[pl.BlockSpec((tile, d), lambda w, wt, we, wlo, whi: (wt[w], 0)),
                  pl.BlockSpec((1, 1, d, ff), lambda w, wt, we, wlo, whi: (layer, we[w], 0, 0)),
                  pl.BlockSpec((1, 1, d, ff), lambda w, wt, we, wlo, whi: (layer, we[w], 0, 0)),
                  pl.BlockSpec((1, 1, ff, d), lambda w, wt, we, wlo, whi: (layer, we[w], 0, 0))],
        out_specs=pl.BlockSpec((tile, d), lambda w, wt, we, wlo, whi: (wt[w], 0)),
        scratch_shapes=[pltpu.VMEM((d, ff), BF16), pltpu.VMEM((d, ff), BF16), pltpu.VMEM((ff, d), BF16)],
    )
    return pl.pallas_call(
        _gmm_kernel,
        grid_spec=grid_spec,
        out_shape=jax.ShapeDtypeStruct((m, d), F32),
        compiler_params=_cparams(1),
        name="moe_gmm",
    )(*work, xs, w_gate, w_up, w_down)


def _work_list(counts, n_slots, tile):
    n_tiles = n_slots // tile
    n_work = n_tiles + N_EXPERTS - 1
    ends = jnp.cumsum(counts)
    starts = ends - counts
    first_tile = starts // tile
    last_tile = jnp.maximum(ends - 1, 0) // tile
    n_items = jnp.where(counts > 0, last_tile - first_tile + 1, 0)
    item_end = jnp.cumsum(n_items)
    item_start = item_end - n_items
    w = jnp.arange(n_work, dtype=jnp.int32)
    used = w < item_end[-1]
    e = jnp.minimum(jnp.sum(w[:, None] >= item_end[None, :], axis=1), N_EXPERTS - 1).astype(jnp.int32)
    is_e = e[:, None] == jnp.arange(N_EXPERTS, dtype=jnp.int32)[None, :]

    def of_e(v):
        return jnp.sum(jnp.where(is_e, v[None, :], 0), axis=1)

    t = of_e(first_tile) + (w - of_e(item_start))
    lo = jnp.maximum(of_e(starts), t * tile) - t * tile
    hi = jnp.minimum(of_e(ends), (t + 1) * tile) - t * tile
    last_e = jnp.max(jnp.where(counts > 0, jnp.arange(N_EXPERTS), 0)).astype(jnp.int32)
    wt = jnp.where(used, t, n_tiles - 1).astype(jnp.int32)
    we = jnp.where(used, e, last_e).astype(jnp.int32)
    wlo = jnp.where(used, lo, 0).astype(jnp.int32)
    whi = jnp.where(used, hi, 0).astype(jnp.int32)
    return (wt, we, wlo, whi), starts


def _combine_kernel(*refs, tile, final):
    if final:
        slot_ref, nslot_ref, x_ref, gt_ref, info_ref, fg_ref, y_ref, out_ref, ybuf, sem = refs
    else:
        slot_ref, nslot_ref, x_ref, gt_ref, info_ref, y_ref, out_ref, ybuf, sem = refs
    i = pl.program_id(0)
    cur = i % 2

    def gather(slots_ref, buf):
        def issue(g, c):
            for u in range(ROW_DMA_UNROLL):
                r = g * ROW_DMA_UNROLL + u
                for k in range(TOP_K):
                    s = slots_ref[0, 0, k * tile + r]
                    pltpu.make_async_copy(y_ref.at[pl.ds(s, 1), :], ybuf.at[buf, k, pl.ds(r, 1), :],
                                          sem.at[buf]).start()
            return c

        lax.fori_loop(0, tile // ROW_DMA_UNROLL, issue, 0)

    @pl.when(i == 0)
    def _():
        gather(slot_ref, cur)

    @pl.when(i + 1 < pl.num_programs(0))
    def _():
        gather(nslot_ref, 1 - cur)

    for k in range(TOP_K):
        pltpu.make_async_copy(y_ref.at[pl.ds(0, tile), :], ybuf.at[cur, k], sem.at[cur]).wait()

    info = info_ref[...]
    ffn = info[:, 2:3] * ybuf[cur, 0] + info[:, 3:4] * ybuf[cur, 1]
    out = x_ref[...] + gt_ref[0] * ffn
    if final:
        out = _rms(out, fg_ref[...])
    out_ref[...] = out


def _combine(x, gt, info, y, slots, *, tile, rows_per_mod, final_gain=None):
    n, d = x.shape
    nt = n // tile
    mrows = gt.shape[1]
    final = final_gain is not None
    slots3 = slots.reshape(nt, 1, TOP_K * tile)
    in_specs = [pl.BlockSpec((1, 1, TOP_K * tile), lambda i: (i, 0, 0), memory_space=pltpu.SMEM),
                pl.BlockSpec((1, 1, TOP_K * tile), lambda i: (jnp.minimum(i + 1, nt - 1), 0, 0),
                             memory_space=pltpu.SMEM),
                pl.BlockSpec((tile, d), lambda i: (i, 0)),
                pl.BlockSpec((1, mrows, d), lambda i: ((i * tile) // rows_per_mod, 0, 0)),
                pl.BlockSpec((tile, LANES), lambda i: (i, 0))]
    ins = [slots3, slots3, x, gt, info]
    if final:
        in_specs.append(pl.BlockSpec(final_gain.shape, lambda i: (0, 0)))
        ins.append(final_gain)
    in_specs.append(pl.BlockSpec(memory_space=pl.ANY))
    ins.append(y)
    return pl.pallas_call(
        functools.partial(_combine_kernel, tile=tile, final=final),
        grid=(nt,),
        in_specs=in_specs,
        out_specs=pl.BlockSpec((tile, d), lambda i: (i, 0)),
        out_shape=jax.ShapeDtypeStruct((n, d), F32),
        scratch_shapes=[pltpu.VMEM((2, TOP_K, tile, d), F32), pltpu.SemaphoreType.DMA((2,))],
        compiler_params=_cparams(1),
        name="moe_combine",
    )(*ins)


def _moe(groups, rw, experts, *, layer, final_gain=None):
    gain, wrh, wrl, br = rw
    routed = []
    for g in groups:
        shf, scf, _ = g["mods"]
        res = _route(g["x"], shf, scf, gain, wrh, wrl, br, tile=g["tile"], rows_per_mod=g["rows_per_mod"],
                     proj=g["proj"])
        x = g["x"]
        if g["proj"] is not None:
            x, res = res[0], res[1:]
        hf, info, meta, cnt = res
        routed.append((x, hf, info, meta, cnt[:, 0].astype(jnp.int32)))
    n_rows = TOP_K * sum(r[0].shape[0] for r in routed)
    gmm_tile = _pick(n_rows, 512)
    work, base = _work_list(sum(r[4] for r in routed), n_rows, gmm_tile)
    base = base.astype(jnp.int32)
    slots = []
    for _, _, _, meta, counts in routed:
        eid = meta[:, 0:TOP_K, :].astype(jnp.int32)
        rank = meta[:, 4:4 + TOP_K, :].astype(jnp.int32)
        start_of = jnp.zeros(eid.shape, jnp.int32)
        for e in range(N_EXPERTS):
            start_of = jnp.where(eid == e, base[e], start_of)
        slots.append(start_of + rank)
        base = base + counts
    xs = _dispatch([r[1] for r in routed], slots, [g["tile"] for g in groups])
    y = _gmm(xs, work, *experts, tile=gmm_tile, layer=layer)
    return [_combine(r[0], g["mods"][2], r[2], y, sl, tile=g["tile"], rows_per_mod=g["rows_per_mod"],
                     final_gain=final_gain)
            for g, r, sl in zip(groups, routed, slots)]


def _rope_turn(blk, c, s):
    return blk * c + pltpu.roll(blk, LANES - QK_ROPE, 1) * s


def _mla_proj_kernel(*refs, sample):
    if sample:
        (x_ref, sh_ref, sc_ref, gain_ref, win_ref, qg_ref, wuq_ref, kvg_ref, c_ref, s_ref,
         ckv_ref, kr_ref, q_ref) = refs
    else:
        (x_ref, sh_ref, sc_ref, gain_ref, win_ref, qg_ref, wuq_ref, kvg_ref, c_ref, s_ref, wuk_ref, wuvt_ref,
         vonet_ref, ct_ref, st_ref, ckv_ref, kr_ref, q_ref, k_ref, v_ref) = refs
    q_lora = qg_ref.shape[1]
    kv_lora = kvg_ref.shape[1]
    h = _rms(x_ref[...], gain_ref[...]) * (1.0 + sc_ref[0]) + sh_ref[0]
    proj = _dot(h.astype(BF16), win_ref[...])
    qn = _rms(proj[:, :q_lora], qg_ref[...])
    ckv = _rms(proj[:, q_lora:q_lora + kv_lora], kvg_ref[...])
    c = c_ref[...]
    s = s_ref[...]
    kf = _rope_turn(proj[:, q_lora + kv_lora:], c, s)
    ckv_ref[...] = ckv
    kr_ref[...] = kf[:, QK_NOPE:QK_NOPE + QK_ROPE]
    qn_b = qn.astype(BF16)
    if sample:
        q = _dot(qn_b, wuq_ref[...])
        for hh in range(MLA_HEADS):
            cols = slice(hh * HEAD_PAD, (hh + 1) * HEAD_PAD)
            q_ref[:, cols] = _rope_turn(q[:, cols], c, s) * SM_SCALE
    else:
        ckv_b = ckv.astype(BF16)
        kn = _dot(ckv_b, wuk_ref[...])
        qt = _dot_nt(wuq_ref[...], qn_b)
        v_ref[0] = (_dot_nt(wuvt_ref[...], ckv_b) + vonet_ref[...]).astype(BF16)
        ct = ct_ref[...]
        st = st_ref[...]
        for hh in range(MLA_HEADS):
            cols = slice(hh * HEAD_PAD, (hh + 1) * HEAD_PAD)
            k_ref[:, cols] = (kn[:, cols] + kf).astype(BF16)
            blk = qt[cols, :]
            turned = jnp.concatenate([blk[QK_ROPE:], blk[:QK_ROPE]], axis=0)
            q_ref[0, cols, :] = ((blk * ct + turned * st) * (SM_SCALE * LOG2E)).astype(BF16)


def _mla_proj(x, sh, sc, gain, win, qg, wuq, kvg, ctab, stab, *, tile, rows_per_mod, tab_tiles, kv_w=None):
    n, d = x.shape
    nt = n // tile
    mrows = sh.shape[1]
    sample = kv_w is None
    kv_lora = kvg.shape[1]

    def const(a):
        nd = a.ndim
        return pl.BlockSpec(a.shape, lambda i: (0,) * nd)

    row = lambda width: pl.BlockSpec((tile, width), lambda i: (i, 0))
    mod_spec = pl.BlockSpec((1, mrows, d), lambda i: ((i * tile) // rows_per_mod, 0, 0))
    tab_spec = pl.BlockSpec((tile, LANES), lambda i: (i % tab_tiles, 0))
    ins = [x, sh, sc, gain, win, qg, wuq, kvg, ctab, stab]
    in_specs = [row(d), mod_spec, mod_spec, const(gain), const(win), const(qg), const(wuq), const(kvg),
                tab_spec, tab_spec]
    qw = MLA_HEADS * HEAD_PAD
    out_shape = [jax.ShapeDtypeStruct((n, kv_lora), F32), jax.ShapeDtypeStruct((n, QK_ROPE), F32)]
    out_specs = [row(kv_lora), row(QK_ROPE)]
    if sample:
        out_shape.append(jax.ShapeDtypeStruct((n, qw), F32))
        out_specs.append(row(qw))
    else:
        wuk, wuvt, vonet, ctab_t, stab_t = kv_w
        seq_len = tab_tiles * tile
        tab_t_spec = pl.BlockSpec((LANES, tile), lambda i: (0, i % tab_tiles))
        by_seq = pl.BlockSpec((1, qw, tile), lambda i: (i // tab_tiles, 0, i % tab_tiles))
        ins += [wuk, wuvt, vonet, ctab_t, stab_t]
        in_specs += [const(wuk), const(wuvt), const(vonet), tab_t_spec, tab_t_spec]
        out_shape += [jax.ShapeDtypeStruct((n // seq_len, qw, seq_len), BF16), jax.ShapeDtypeStruct((n, qw), BF16),
                      jax.ShapeDtypeStruct((n // seq_len, qw, seq_len), BF16)]
        out_specs += [by_seq, row(qw), by_seq]
    return pl.pallas_call(
        functools.partial(_mla_proj_kernel, sample=sample),
        grid=(nt,),
        in_specs=in_specs,
        out_specs=out_specs,
        out_shape=out_shape,
        compiler_params=_cparams(1),
        name="mla_proj",
    )(*ins)


def _attn_kernel(q_ref, k_ref, v_ref, o_ref, sa_ref, sb_ref, m_ref, acc_ref, *, tq, tk):
    i = pl.program_id(2)
    heads = range(2)

    def cols(hh):
        return slice(hh * HEAD_PAD, (hh + 1) * HEAD_PAD)

    def scores(j, dst_ref):
        keys = pl.ds(pl.multiple_of(j * tk, tk), tk)
        for hh in heads:
            dst_ref[hh] = _dot(k_ref[0, keys, cols(hh)], q_ref[0, cols(hh), :])

    def absorb(j, src_ref, masked):
        keys = pl.ds(pl.multiple_of(j * tk, tk), tk)
        if masked:
            keep = lax.broadcasted_iota(jnp.int32, (tk, tq), 0) <= lax.broadcasted_iota(jnp.int32, (tk, tq), 1)
        for hh in heads:
            s = src_ref[hh]
            if masked:
                s = jnp.where(keep, s, -jnp.inf)
            m = m_ref[hh]
            m_new = jnp.maximum(m, jnp.max(s, axis=0, keepdims=True))
            p = jnp.exp2(s - m_new)
            acc_ref[hh] = jnp.exp2(m - m_new) * acc_ref[hh] + _dot(v_ref[0, cols(hh), keys], p.astype(BF16))
            m_ref[hh] = m_new

    m_ref[...] = jnp.full(m_ref.shape, -jnp.inf, F32)
    acc_ref[...] = jnp.zeros(acc_ref.shape, F32)
    scores(0, sa_ref)

    def pair(j):
        scores(j + 1, sb_ref)
        absorb(j, sa_ref, False)
        scores(j + 2, sa_ref)
        absorb(j + 1, sb_ref, False)

    def long_trip(jj, c):
        for q in range(ATTN_PAIRS_PER_TRIP):
            pair(2 * (ATTN_PAIRS_PER_TRIP * jj + q))
        return c

    n_long = i // (2 * ATTN_PAIRS_PER_TRIP)
    lax.fori_loop(0, n_long, long_trip, 0)
    done = 2 * ATTN_PAIRS_PER_TRIP * n_long

    def short_trip(t, c):
        pair(done + 2 * t)
        return c

    lax.fori_loop(0, (i - done) // 2, short_trip, 0)

    @pl.when(i % 2 == 1)
    def _():
        scores(i, sb_ref)
        absorb(i - 1, sa_ref, False)
        absorb(i, sb_ref, True)

    @pl.when(i % 2 == 0)
    def _():
        absorb(i, sa_ref, True)

    out_t = jnp.concatenate([acc_ref[hh][:V_DIM] / acc_ref[hh][V_DIM:V_DIM + 1] for hh in heads], axis=0)
    o_ref[0] = out_t.T.astype(o_ref.dtype)


def _attention(q_t, k, v_t, *, tq, tk):
    assert tq == tk, "one diagonal key tile per query tile"
    b, s, _ = k.shape
    pairs = MLA_HEADS // 2
    return pl.pallas_call(
        functools.partial(_attn_kernel, tq=tq, tk=tk),
        grid=(b, pairs, s // tq),
        in_specs=[pl.BlockSpec((1, 2 * HEAD_PAD, tq), lambda bi, p, i: (bi, p, i)),
                  pl.BlockSpec((1, s, 2 * HEAD_PAD), lambda bi, p, i: (bi, 0, p)),
                  pl.BlockSpec((1, 2 * HEAD_PAD, s), lambda bi, p, i: (bi, p, 0))],
        out_specs=pl.BlockSpec((1, tq, 2 * V_DIM), lambda bi, p, i: (bi, i, p)),
        out_shape=jax.ShapeDtypeStruct((b, s, MLA_HEADS * V_DIM), BF16),
        scratch_shapes=[pltpu.VMEM((2, tk, tq), F32), pltpu.VMEM((2, tk, tq), F32),
                        pltpu.VMEM((2, 1, tq), F32), pltpu.VMEM((2, HEAD_PAD, tq), F32)],
        compiler_params=_cparams(3),
        name="prompt_attn",
    )(q_t, k, v_t)


def _qabs_kernel(q_ref, m_ref, o_ref):
    o_ref[0] = _dot(q_ref[...].astype(BF16), m_ref[0])


def _qabs(q, mats):
    n = q.shape[0]
    width = mats.shape[-1]
    return pl.pallas_call(
        _qabs_kernel,
        grid=(MLA_HEADS,),
        in_specs=[pl.BlockSpec((n, HEAD_PAD), lambda h: (0, h)),
                  pl.BlockSpec((1, HEAD_PAD, width), lambda h: (h, 0, 0))],
        out_specs=pl.BlockSpec((1, n, width), lambda h: (h, 0, 0)),
        out_shape=jax.ShapeDtypeStruct((MLA_HEADS, n, width), F32),
        compiler_params=_cparams(1),
        name="sample_qabs",
    )(q, mats)


def _paged_attn_kernel(pt_ref, q_ref, cn_ref, rn_ref, ck_hbm, kr_hbm, o_ref,
                       ckbuf, krbuf, m_ref, l_ref, acc_ref, sem, *, group, pages, page, n_chunks, t_new, t_pad):
    bg = pl.program_id(0)
    c = pl.program_id(1)
    step = bg * n_chunks + c
    n_steps = pl.num_programs(0) * n_chunks
    slot = step % PAGE_RING
    kv_lora = ckbuf.shape[-1]

    def start_chunk(st):
        bgi, ci, sl = st // n_chunks, st % n_chunks, st % PAGE_RING
        for g in range(group):
            for p in range(pages):
                phys = pt_ref[bgi * group + g, ci * pages + p]
                pltpu.make_async_copy(ck_hbm.at[phys], ckbuf.at[sl, g, pl.ds(p * page, page), :],
                                      sem.at[0, sl]).start()
                pltpu.make_async_copy(kr_hbm.at[phys], krbuf.at[sl, g, p], sem.at[1, sl]).start()

    @pl.when(step == 0)
    def _():
        for ahead in range(PAGE_RING - 1):
            @pl.when(ahead < n_steps)
            def _():
                start_chunk(step + ahead)

    pltpu.make_async_copy(ckbuf.at[slot], ckbuf.at[slot], sem.at[0, slot]).wait()
    pltpu.make_async_copy(krbuf.at[slot], krbuf.at[slot], sem.at[1, slot]).wait()

    @pl.when(c == 0)
    def _():
        m_ref[...] = jnp.full(m_ref.shape, -jnp.inf, F32)
        l_ref[...] = jnp.zeros(l_ref.shape, F32)
        acc_ref[...] = jnp.zeros(acc_ref.shape, F32)

    def absorb(g, s, values):
        m = m_ref[g]
        m_new = jnp.maximum(m, jnp.max(s, axis=1, keepdims=True))
        alpha = jnp.exp(m - m_new)
        p = jnp.exp(s - m_new)
        l_ref[g] = alpha * l_ref[g] + jnp.sum(p, axis=1, keepdims=True)
        acc_ref[g] = alpha * acc_ref[g] + _dot(p.astype(BF16), values)
        m_ref[g] = m_new

    cks, scores = [], []
    for g in range(group):
        qa = q_ref[g]
        q_lat = qa[:, :kv_lora].astype(BF16)
        q_rope = qa[:, kv_lora:kv_lora + QK_ROPE].astype(BF16)
        ck = ckbuf[slot, g].astype(BF16)
        kr_t = jnp.concatenate([krbuf[slot, g, p] for p in range(pages)], axis=1).astype(BF16)
        cks.append(ck)
        scores.append(_dot_nt(q_lat, ck) + _dot(q_rope, kr_t))

    nxt = step + PAGE_RING - 1

    @pl.when(nxt < n_steps)
    def _():
        start_chunk(nxt)

    for g in range(group):
        absorb(g, scores[g], cks[g])

    @pl.when(c == n_chunks - 1)
    def _():
        rows = q_ref.shape[1]
        t_q = lax.broadcasted_iota(jnp.int32, (rows, t_pad), 0) % t_new
        t_k = lax.broadcasted_iota(jnp.int32, (rows, t_pad), 1)
        for g in range(group):
            qa = q_ref[g]
            q_lat = qa[:, :kv_lora].astype(BF16)
            q_rope = qa[:, kv_lora:kv_lora + QK_ROPE].astype(BF16)
            cn = cn_ref[g].astype(BF16)
            s = _dot_nt(q_lat, cn) + _dot_nt(q_rope, rn_ref[g].astype(BF16))
            absorb(g, jnp.where(t_k <= t_q, s, -jnp.inf), cn)
            o_ref[g] = acc_ref[g] / l_ref[g]


def _paged_attn(page_table, qabs, ckv_new, kr_new, cache_ck, cache_kr_t, *, group, pages, t_new):
    b, rows, width = qabs.shape
    n_pages = page_table.shape[1]
    n_chunks = n_pages // pages
    page, kv_lora = cache_ck.shape[1:]
    t_pad = ckv_new.shape[1]
    grid_spec = pltpu.PrefetchScalarGridSpec(
        num_scalar_prefetch=1,
        grid=(b // group, n_chunks),
        in_specs=[pl.BlockSpec((group, rows, width), lambda bi, ci, pt: (bi, 0, 0)),
                  pl.BlockSpec((group, t_pad, kv_lora), lambda bi, ci, pt: (bi, 0, 0)),
                  pl.BlockSpec((group, t_pad, QK_ROPE), lambda bi, ci, pt: (bi, 0, 0)),
                  pl.BlockSpec(memory_space=pl.ANY),
                  pl.BlockSpec(memory_space=pl.ANY)],
        out_specs=pl.BlockSpec((group, rows, kv_lora), lambda bi, ci, pt: (bi, 0, 0)),
        scratch_shapes=[pltpu.VMEM((PAGE_RING, group, pages * page, kv_lora), F32),
                        pltpu.VMEM((PAGE_RING, group, pages, QK_ROPE, page), F32),
                        pltpu.VMEM((group, rows, 1), F32), pltpu.VMEM((group, rows, 1), F32),
                        pltpu.VMEM((group, rows, kv_lora), F32),
                        pltpu.SemaphoreType.DMA((2, PAGE_RING))],
    )
    return pl.pallas_call(
        functools.partial(_paged_attn_kernel, group=group, pages=pages, page=page, n_chunks=n_chunks, t_new=t_new,
                          t_pad=t_pad),
        grid_spec=grid_spec,
        out_shape=jax.ShapeDtypeStruct((b, rows, kv_lora), F32),
        compiler_params=_cparams(2),
        name="paged_attn",
    )(page_table, qabs, ckv_new, kr_new, cache_ck, cache_kr_t)


def _vup_kernel(o_ref, w_ref, out_ref):
    lat = jnp.concatenate([o_ref[0], o_ref[1]], axis=-1).astype(BF16)
    out_ref[...] = _dot(lat, w_ref[0]).astype(out_ref.dtype)


def _vup(o_lat, w_pairs):
    h, n, c = o_lat.shape
    return pl.pallas_call(
        _vup_kernel,
        grid=(h // 2,),
        in_specs=[pl.BlockSpec((2, n, c), lambda p: (p, 0, 0)),
                  pl.BlockSpec((1, 2 * c, 2 * V_DIM), lambda p: (p, 0, 0))],
        out_specs=pl.BlockSpec((n, 2 * V_DIM), lambda p: (0, p)),
        out_shape=jax.ShapeDtypeStruct((n, h * V_DIM), BF16),
        compiler_params=_cparams(1),
        name="sample_vup",
    )(o_lat, w_pairs)


def _block_diag(w):
    g, a, b = w.shape
    out = jnp.zeros((g * a, g * b), w.dtype)
    for i in range(g):
        out = out.at[i * a:(i + 1) * a, i * b:(i + 1) * b].set(w[i])
    return out


def _swap_halves(w):
    half = w.shape[-1] // 2
    return jnp.concatenate([w[..., half:], w[..., :half]], axis=-1)


def _rope_tables(pos):
    half = QK_ROPE // 2
    inv_freq = ROPE_THETA ** (-jnp.arange(half, dtype=F32) / half)
    ang = pos.astype(F32)[:, None] * inv_freq[None, :]
    cos, sin = jnp.cos(ang), jnp.sin(ang)
    n = pos.shape[0]
    ctab = jnp.concatenate([jnp.ones((n, QK_NOPE), F32), cos, cos, jnp.zeros((n, QK_ROPE), F32)], axis=1)
    stab = jnp.concatenate([jnp.zeros((n, QK_NOPE), F32), -sin, sin, jnp.zeros((n, QK_ROPE), F32)], axis=1)
    return ctab, stab


def _router_weights(w_group, b_group, w_expert, b_expert):
    d = w_group.shape[0]
    we = jnp.transpose(w_expert, (1, 0, 2)).reshape(d, N_EXPERTS)
    w = jnp.concatenate([w_group, we, jnp.zeros((d, LANES - MOE_GROUPS - N_EXPERTS), F32)], axis=1)
    bias = jnp.concatenate([b_group, b_expert.reshape(-1), jnp.zeros((LANES - MOE_GROUPS - N_EXPERTS,), F32)])
    hi = w.astype(BF16)
    lo = (w - hi.astype(F32)).astype(BF16)
    return hi, lo, bias.reshape(1, LANES)


def _pick(n, pref):
    t = min(n, pref)
    while n % t:
        t //= 2
    return t


def kernel(x_prompt, x_sample, state_pool, state_conv, state_lru, cache_kv_latent, cache_k_rope, page_table,
           c_prompt, c_sample, ada_w, ada_b, norm_mix, norm_ffn, ab_w_in, ab_w_out, pool_w, pool_scale,
           conv_w, conv_b, lru_w_r, lru_b_r, lru_w_i, lru_b_i, lru_lambda,
           mla_w_in, mla_q_norm, mla_w_uq, mla_kv_norm, mla_w_uk, mla_w_uv, mla_w_out,
           router_w_group, router_b_group, router_w_expert, router_b_expert,
           moe_w_gate, moe_w_up, moe_w_down, final_norm):
    bp, seq, d = x_prompt.shape
    bs, t_new, _ = x_sample.shape
    depth = ada_w.shape[0]
    pw = pool_scale.shape[-1]
    page = cache_kv_latent.shape[2]
    past_len = page_table.shape[1] * page
    q_lora = mla_q_norm.shape[-1]
    kv_lora = mla_kv_norm.shape[-1]

    n_c = bp + bs
    n_c_pad = -(-n_c // 8) * 8
    c_all = jnp.concatenate([c_prompt, c_sample, jnp.zeros((n_c_pad - n_c, d), F32)], axis=0)
    mod = _ada(c_all, ada_w, ada_b)

    def mods(layer, lo, hi):
        return [mod[layer, lo:hi, k * d:(k + 1) * d] for k in range(6)]

    n_p = bp * seq
    n_s = bs * t_new
    tile_p = _pick(seq, 512)
    tile_s = _pick(n_s, 512)

    xp = x_prompt.reshape(n_p, d)
    xs = x_sample.reshape(n_s, d)
    outs = {}

    for layer in range(depth):
        sh_m, sc_m, gt_m, sh_f, sc_f, gt_f = mods(layer, 0, bp)
        sh_ms, sc_ms, gt_ms, sh_fs, sc_fs, gt_fs = mods(layer, bp, bp + bs)
        rep = lambda v: jnp.repeat(v, t_new, axis=0)[None]
        per_seq = lambda v: v[:, None, :]
        rw = (norm_ffn[layer].reshape(1, d),) + _router_weights(
            router_w_group[layer], router_b_group[layer], router_w_expert[layer], router_b_expert[layer])
        experts = (moe_w_gate, moe_w_up, moe_w_down)
        final_gain = final_norm.reshape(1, d) if layer == depth - 1 else None
        ffn_mods_p = (per_seq(sh_f), per_seq(sc_f), per_seq(gt_f))
        ffn_mods_s = (rep(sh_fs), rep(sc_fs), rep(gt_fs))

        if layer % 2 == 0:
            e = layer // 2
            wts = (norm_mix[layer].reshape(1, d), ab_w_in[e].astype(BF16), _block_diag(pool_w[e]).astype(BF16),
                   pool_scale[e].reshape(1, pw), conv_w[e], conv_b[e].reshape(1, pw),
                   jnp.concatenate([_block_diag(lru_w_r[e]), _block_diag(lru_w_i[e])], axis=1).astype(BF16),
                   jnp.concatenate([lru_b_r[e], lru_b_i[e]]).reshape(1, 2 * pw),
                   lru_lambda[e].reshape(1, pw), ab_w_out[e].astype(BF16))
            x1p, pool_p, conv_p, lru_p = _mix0(
                xp.reshape(bp, seq, d), per_seq(sh_m), per_seq(sc_m), per_seq(gt_m),
                jnp.zeros((bp, POOL_HALO - 1, pw), F32), jnp.zeros((bp, CONV_WIDTH - 1, pw), F32),
                jnp.zeros((bp, 1, pw), F32), wts, tt=tile_p, bb=1, start=0)
            tm = lambda a: jnp.swapaxes(a, 0, 1).reshape(1, -1, a.shape[-1])
            x1s, pool_s, conv_s, lru_s = _mix0(
                tm(xs.reshape(bs, t_new, d)), sh_ms[None], sc_ms[None], gt_ms[None],
                tm(state_pool[e]), tm(state_conv[e]), state_lru[e][None], wts, tt=t_new, bb=bs, start=past_len)
            bm = lambda a, n: jnp.swapaxes(a.reshape(n, bs, a.shape[-1]), 0, 1)
            outs.setdefault("pool_p", []).append(pool_p)
            outs.setdefault("pool_s", []).append(bm(pool_s, POOL_HALO - 1))
            outs.setdefault("conv_p", []).append(conv_p)
            outs.setdefault("conv_s", []).append(bm(conv_s, CONV_WIDTH - 1))
            outs.setdefault("lru_p", []).append(lru_p.reshape(bp, pw))
            outs.setdefault("lru_s", []).append(lru_s.reshape(bs, pw))
            xp = x1p.reshape(n_p, d)
            xs = bm(x1s, t_new).reshape(n_s, d)
            proj_p = proj_s = None
        else:
            o = layer // 2
            gain = norm_mix[layer].reshape(1, d)
            w_in = mla_w_in[o]
            w_kr = w_in[:, q_lora + kv_lora:]
            win = jnp.concatenate([w_in[:, :q_lora + kv_lora], jnp.zeros((d, QK_NOPE), F32), w_kr,
                                   _swap_halves(w_kr)], axis=1).astype(BF16)
            wq = mla_w_uq[o].reshape(q_lora, MLA_HEADS, QK_NOPE + QK_ROPE)
            wuq = jnp.concatenate([wq, _swap_halves(wq[..., QK_NOPE:])], axis=-1).reshape(
                q_lora, MLA_HEADS * HEAD_PAD).astype(BF16)
            wuk = jnp.concatenate([mla_w_uk[o], jnp.zeros((kv_lora, MLA_HEADS, HEAD_PAD - QK_NOPE), F32)],
                                  axis=-1).reshape(kv_lora, MLA_HEADS * HEAD_PAD).astype(BF16)
            wuv_t = jnp.concatenate([mla_w_uv[o], jnp.zeros((kv_lora, MLA_HEADS, HEAD_PAD - V_DIM), F32)],
                                    axis=-1).reshape(kv_lora, MLA_HEADS * HEAD_PAD).T.astype(BF16)
            vone_t = jnp.tile((jnp.arange(HEAD_PAD) == V_DIM).astype(F32), MLA_HEADS).reshape(-1, 1)
            qg = mla_q_norm[o].reshape(1, q_lora)
            kvg = mla_kv_norm[o].reshape(1, kv_lora)
            wout = mla_w_out[o].astype(BF16)

            ctab, stab = _rope_tables(jnp.arange(seq, dtype=jnp.int32))
            ckv_p, kr_p, qt_p, k_p, vt_p = _mla_proj(
                xp, per_seq(sh_m), per_seq(sc_m), gain, win, qg, wuq.T, kvg, ctab, stab,
                tile=tile_p, rows_per_mod=seq, tab_tiles=seq // tile_p, kv_w=(wuk, wuv_t, vone_t, ctab.T, stab.T))
            qw = MLA_HEADS * HEAD_PAD
            o_p = _attention(qt_p, k_p.reshape(bp, seq, qw), vt_p, tq=tile_p, tk=tile_p)
            proj_p = (o_p.reshape(n_p, MLA_HEADS * V_DIM), wout, per_seq(gt_m))
            outs.setdefault("lat_p", []).append(ckv_p.reshape(bp, seq, kv_lora))
            outs.setdefault("rope_p", []).append(kr_p.reshape(bp, seq, QK_ROPE))

            pos_s = jnp.tile(past_len + jnp.arange(t_new, dtype=jnp.int32), bs)
            ctab_s, stab_s = _rope_tables(pos_s)
            ckv_s, kr_s, q_s = _mla_proj(
                xs, rep(sh_ms), rep(sc_ms), gain, win, qg, wuq, kvg, ctab_s, stab_s,
                tile=tile_s, rows_per_mod=n_s, tab_tiles=n_s // tile_s)
            aw = kv_lora + HEAD_PAD
            wk_t = jnp.transpose(mla_w_uk[o], (1, 2, 0))
            mats = jnp.zeros((MLA_HEADS, HEAD_PAD, aw), F32)
            mats = mats.at[:, :QK_NOPE, :kv_lora].set(wk_t)
            mats = mats.at[:, QK_NOPE:QK_NOPE + QK_ROPE, kv_lora:kv_lora + QK_ROPE].set(
                jnp.broadcast_to(jnp.eye(QK_ROPE, dtype=F32), (MLA_HEADS, QK_ROPE, QK_ROPE)))
            qabs = _qabs(q_s, mats.astype(BF16))
            qabs = jnp.transpose(qabs.reshape(MLA_HEADS, bs, t_new, aw), (1, 0, 2, 3)).reshape(
                bs, MLA_HEADS * t_new, aw)
            t_pad = 8
            pad_t = lambda a: jnp.pad(a.reshape(bs, t_new, -1), ((0, 0), (0, t_pad - t_new), (0, 0)))
            o_lat = _paged_attn(page_table, qabs, pad_t(ckv_s), pad_t(kr_s), cache_kv_latent[o],
                                jnp.swapaxes(cache_k_rope[o], 1, 2),
                                group=_pick(bs, 4), pages=_pick(page_table.shape[1], 8), t_new=t_new)
            o_lat = jnp.transpose(o_lat.reshape(bs, MLA_HEADS, t_new, kv_lora), (1, 0, 2, 3)).reshape(
                MLA_HEADS, n_s, kv_lora)
            wv = jnp.transpose(mla_w_uv[o], (1, 0, 2))
            w_pairs = jnp.zeros((MLA_HEADS // 2, 2 * kv_lora, 2 * V_DIM), F32)
            w_pairs = w_pairs.at[:, :kv_lora, :V_DIM].set(wv[0::2]).at[:, kv_lora:, V_DIM:].set(wv[1::2])
            o_s = _vup(o_lat, w_pairs.astype(BF16))
            proj_s = (o_s, wout, rep(gt_ms))
            outs.setdefault("lat_s", []).append(ckv_s.reshape(bs, t_new, kv_lora))
            outs.setdefault("rope_s", []).append(kr_s.reshape(bs, t_new, QK_ROPE))

        xp, xs = _moe([dict(x=xp, mods=ffn_mods_p, tile=tile_p, rows_per_mod=seq, proj=proj_p),
                       dict(x=xs, mods=ffn_mods_s, tile=tile_s, rows_per_mod=n_s, proj=proj_s)],
                      rw, experts, layer=layer, final_gain=final_gain)

    st = lambda k: jnp.stack(outs[k])
    return (xp.reshape(bp, seq, d), xs.reshape(bs, t_new, d),
            st("pool_p"), st("pool_s"), st("conv_p"), st("conv_s"), st("lru_p"), st("lru_s"),
            st("lat_p"), st("lat_s"), st("rope_p"), st("rope_s"))
```

```python
import functools

import jax
import jax.numpy as jnp
from jax import lax
from jax.experimental import pallas as pl
from jax.experimental.pallas import tpu as pltpu

F32 = jnp.float32
BF16 = jnp.bfloat16

EPS = 1e-6
POOL_WINDOWS = (2, 4, 8, 16)
POOL_HALO = 16
CONV_WIDTH = 4
CONV_HALO = 8
LRU_C = 8.0
MLA_HEADS = 16
QK_NOPE = 64
QK_ROPE = 32
V_DIM = 64
HEAD_PAD = 128
ROPE_THETA = 10000.0
SM_SCALE = (QK_NOPE + QK_ROPE) ** -0.5
MOE_GROUPS = 4
EXPERTS_PER_GROUP = 8
N_EXPERTS = MOE_GROUPS * EXPERTS_PER_GROUP
TOP_K = 2
LANES = 128
ROW_DMA_UNROLL = 8
PAGE_RING = 3
ATTN_PAIRS_PER_TRIP = 4
LOG2E = 1.4426950408889634
VMEM_LIMIT = 56 * 1024 * 1024


def _cparams(n_axes):
    return pltpu.CompilerParams(dimension_semantics=("arbitrary",) * n_axes,
                                vmem_limit_bytes=VMEM_LIMIT)


def _dot(a, b):
    return jnp.dot(a, b, preferred_element_type=F32)


def _dot_nt(a, b):
    return lax.dot_general(a, b, (((1,), (1,)), ((), ())), preferred_element_type=F32)


def _rms(x, gain):
    return x * lax.rsqrt(jnp.mean(x * x, axis=-1, keepdims=True) + EPS) * gain


def _silu(x):
    return x * jax.nn.sigmoid(x)


def _ada_kernel(c_ref, w_ref, b_ref, o_ref):
    a = _silu(c_ref[...]).astype(BF16)
    o_ref[0] = _dot(a, w_ref[0].astype(BF16)) + b_ref[0]


def _ada(c_all, ada_w, ada_b):
    depth, d, n6 = ada_w.shape
    rows = c_all.shape[0]
    tn = 1024
    return pl.pallas_call(
        _ada_kernel,
        grid=(depth, n6 // tn),
        in_specs=[pl.BlockSpec((rows, d), lambda l, j: (0, 0)),
                  pl.BlockSpec((1, d, tn), lambda l, j: (l, 0, j)),
                  pl.BlockSpec((1, 1, tn), lambda l, j: (l, 0, j))],
        out_specs=pl.BlockSpec((1, rows, tn), lambda l, j: (l, 0, j)),
        out_shape=jax.ShapeDtypeStruct((depth, rows, n6), F32),
        compiler_params=_cparams(2),
        name="ada_mod",
    )(c_all, ada_w, ada_b.reshape(depth, 1, n6))


def _mix0_kernel(x_ref, sh_ref, sc_ref, gt_ref, pool0_ref, conv0_ref, lru0_ref,
                 gain_ref, win_ref, poolw_ref, pscale_ref, convw_ref, convb_ref,
                 wri_ref, bri_ref, lam_ref, wout_ref,
                 x1_ref, pooln_ref, convn_ref, lrun_ref,
                 zp_ref, zc_ref, h_ref, *, tt, bb, start, n_t):
    t = pl.program_id(1)
    rows = tt * bb
    pw = zp_ref.shape[1]
    p0 = POOL_HALO * bb
    c0 = CONV_HALO * bb

    @pl.when(t == 0)
    def _():
        zp_ref[0:bb, :] = jnp.zeros((bb, pw), F32)
        zp_ref[bb:p0, :] = pool0_ref[0]
        zc_ref[0:c0 - (CONV_WIDTH - 1) * bb, :] = jnp.zeros((c0 - (CONV_WIDTH - 1) * bb, pw), F32)
        zc_ref[c0 - (CONV_WIDTH - 1) * bb:c0, :] = conv0_ref[0]
        h_ref[...] = lru0_ref[0]

    def per_row(v):
        return v if bb == 1 else jnp.concatenate([v] * tt, axis=0)

    x = x_ref[0]
    h = _rms(x, gain_ref[...]) * (1.0 + per_row(sc_ref[0])) + per_row(sh_ref[0])
    proj = _dot(h.astype(BF16), win_ref[...])
    u_pool = proj[:, :pw]
    u_x = proj[:, pw:2 * pw]
    u_g = proj[:, 2 * pw:]
    zp_ref[p0:p0 + rows, :] = u_pool
    zc_ref[c0:c0 + rows, :] = u_x

    if bb == 1:
        tix = lax.broadcasted_iota(jnp.int32, (rows, 1), 0)
    else:
        tix = jnp.concatenate([jnp.full((bb, 1), i, jnp.int32) for i in range(tt)], axis=0)
    pos = start + t * tt + tix

    gd = pw // len(POOL_WINDOWS)
    means = []
    if bb == 1 and POOL_WINDOWS == (2, 4, 8, 16):
        sw = zp_ref[0:p0 + rows, :]
        for g, w in enumerate(POOL_WINDOWS):
            sw = sw + pltpu.roll(sw, w // 2, 0)
            cnt = jnp.minimum(pos + 1, w).astype(F32)
            means.append(sw[p0:, :gd] / cnt)
            if g + 1 < len(POOL_WINDOWS):
                sw = sw[:, gd:]
    else:
        for g, w in enumerate(POOL_WINDOWS):
            cols = slice(g * gd, (g + 1) * gd)
            acc = zp_ref[p0:p0 + rows, cols]
            for i in range(1, w):
                acc = acc + zp_ref[p0 - i * bb:p0 - i * bb + rows, cols]
            cnt = jnp.minimum(pos + 1, w).astype(F32)
            means.append(acc / cnt)
    pooled = jnp.concatenate(means, axis=-1) - u_pool
    y_a = _dot(pooled.astype(BF16), poolw_ref[...]) * pscale_ref[...]

    xc = convb_ref[...]
    if bb == 1:
        z_al = zc_ref[0:c0 + rows, :]
        for k in range(CONV_WIDTH):
            back = CONV_WIDTH - 1 - k
            z_k = z_al if back == 0 else pltpu.roll(z_al, back, 0)
            xc = xc + z_k[c0:, :] * convw_ref[k:k + 1, :]
    else:
        for k in range(CONV_WIDTH):
            off = c0 - (CONV_WIDTH - 1 - k) * bb
            xc = xc + zc_ref[off:off + rows, :] * convw_ref[k:k + 1, :]
    pre = _dot(xc.astype(BF16), wri_ref[...]) + bri_ref[...]
    r = jax.nn.sigmoid(pre[:, :pw])
    gi = jax.nn.sigmoid(pre[:, pw:])
    lam = lam_ref[...]
    softplus_neg = jnp.maximum(-lam, 0.0) + jnp.log1p(jnp.exp(-jnp.abs(lam)))
    log_a = -LRU_C * r * softplus_neg
    a = jnp.exp(log_a)
    one_m = 1.0 - a * a
    b = jnp.where(one_m > 0.0, one_m * lax.rsqrt(one_m), 0.0) * gi * xc

    if bb == 1:
        rowi = lax.broadcasted_iota(jnp.int32, (rows, 1), 0)
        s = 1
        while s < rows:
            if s < 8:
                keep = rowi >= s
                b = jnp.where(keep, a * pltpu.roll(b, s, 0) + b, b)
                a = jnp.where(keep, a * pltpu.roll(a, s, 0), a)
            else:
                b = jnp.concatenate([b[:s], a[s:] * b[:rows - s] + b[s:]], axis=0)
                a = jnp.concatenate([a[:s], a[s:] * a[:rows - s]], axis=0)
            s *= 2
        hs = b + a * h_ref[...]
        h_ref[...] = hs[rows - 1:rows, :]
    else:
        hprev = h_ref[...]
        parts = []
        for i in range(tt):
            hprev = a[i * bb:(i + 1) * bb] * hprev + b[i * bb:(i + 1) * bb]
            parts.append(hprev)
        hs = jnp.concatenate(parts, axis=0)
        h_ref[...] = hprev
    y_b = hs * jax.nn.gelu(u_g)

    mix = _dot(y_a.astype(BF16), wout_ref[0:pw, :]) + _dot(y_b.astype(BF16), wout_ref[pw:2 * pw, :])
    x1_ref[0] = x + per_row(gt_ref[0]) * mix

    @pl.when(t == n_t - 1)
    def _():
        pooln_ref[0] = zp_ref[p0 + rows - (POOL_HALO - 1) * bb:p0 + rows, :]
        convn_ref[0] = zc_ref[c0 + rows - (CONV_WIDTH - 1) * bb:c0 + rows, :]
        lrun_ref[0] = h_ref[...]

    if n_t > 1:
        @pl.when(t < n_t - 1)
        def _():
            zp_ref[bb:p0, :] = zp_ref[rows + bb:rows + p0, :]
            zc_ref[c0 - (CONV_WIDTH - 1) * bb:c0, :] = zc_ref[c0 + rows - (CONV_WIDTH - 1) * bb:c0 + rows, :]


def _mix0(x, sh, sc, gt, pool0, conv0, lru0, wts, *, tt, bb, start):
    nbb, tot, d = x.shape
    n_t = tot // (tt * bb)
    rows = tt * bb
    pw = pool0.shape[-1]
    hp = (POOL_HALO - 1) * bb
    hc = (CONV_WIDTH - 1) * bb

    def const(a):
        nd = a.ndim
        return pl.BlockSpec(a.shape, lambda i, j: (0,) * nd)

    def per_b(n_rows, width):
        return pl.BlockSpec((1, n_rows, width), lambda i, j: (i, 0, 0))

    kern = functools.partial(_mix0_kernel, tt=tt, bb=bb, start=start, n_t=n_t)
    return pl.pallas_call(
        kern,
        grid=(nbb, n_t),
        in_specs=[pl.BlockSpec((1, rows, d), lambda i, j: (i, j, 0)),
                  per_b(bb, d), per_b(bb, d), per_b(bb, d),
                  per_b(hp, pw), per_b(hc, pw), per_b(bb, pw)] + [const(w) for w in wts],
        out_specs=[pl.BlockSpec((1, rows, d), lambda i, j: (i, j, 0)),
                   per_b(hp, pw), per_b(hc, pw), per_b(bb, pw)],
        out_shape=[jax.ShapeDtypeStruct((nbb, tot, d), F32),
                   jax.ShapeDtypeStruct((nbb, hp, pw), F32),
                   jax.ShapeDtypeStruct((nbb, hc, pw), F32),
                   jax.ShapeDtypeStruct((nbb, bb, pw), F32)],
        scratch_shapes=[pltpu.VMEM(((POOL_HALO + tt) * bb, pw), F32),
                        pltpu.VMEM(((CONV_HALO + tt) * bb, pw), F32),
                        pltpu.VMEM((bb, pw), F32)],
        compiler_params=_cparams(2),
        name="mix0",
    )(x, sh, sc, gt, pool0, conv0, lru0, *wts)


ROUTE_ROWS = 40
def _route_kernel(*refs, with_proj):
    if with_proj:
        (x_ref, o_ref, wout_ref, gtm_ref, shf_ref, scf_ref, gain_ref, wrh_ref, wrl_ref, br_ref,
         x1_ref, hf_ref, info_ref, meta_ref, cnt_ref, run_ref) = refs
    else:
        (x_ref, shf_ref, scf_ref, gain_ref, wrh_ref, wrl_ref, br_ref,
         hf_ref, info_ref, meta_ref, cnt_ref, run_ref) = refs

    @pl.when(pl.program_id(0) == 0)
    def _():
        run_ref[...] = jnp.zeros(run_ref.shape, F32)

    x = x_ref[...]
    if with_proj:
        x = x + gtm_ref[0] * _dot(o_ref[...], wout_ref[...])
        x1_ref[...] = x
    hf = _rms(x, gain_ref[...]) * (1.0 + scf_ref[0]) + shf_ref[0]
    hf_ref[...] = hf

    hi = hf.astype(BF16)
    lo = (hf - hi.astype(F32)).astype(BF16)
    logits = _dot(hi, wrh_ref[...]) + _dot(lo, wrh_ref[...]) + _dot(hi, wrl_ref[...]) + br_ref[...]

    tq = logits.shape[0]
    lt = logits.T[:ROUTE_ROWS]
    row = lax.broadcasted_iota(jnp.int32, (ROUTE_ROWS, tq), 0).astype(F32)
    neg = -jnp.inf
    big = float(ROUTE_ROWS)
    is_g = row < MOE_GROUPS
    gl = jnp.where(is_g, lt, neg)
    mg = jnp.max(gl, axis=0, keepdims=True)
    gidx = jnp.min(jnp.where(gl == mg, row, big), axis=0, keepdims=True)
    p_group = 1.0 / jnp.sum(jnp.where(is_g, jnp.exp(gl - mg), 0.0), axis=0, keepdims=True)
    first = MOE_GROUPS + gidx * EXPERTS_PER_GROUP
    el = jnp.where((row >= first) & (row < first + EXPERTS_PER_GROUP), lt, neg)
    v1 = jnp.max(el, axis=0, keepdims=True)
    i1 = jnp.min(jnp.where(el == v1, row, big), axis=0, keepdims=True)
    el2 = jnp.where(row == i1, neg, el)
    v2 = jnp.max(el2, axis=0, keepdims=True)
    i2 = jnp.min(jnp.where(el2 == v2, row, big), axis=0, keepdims=True)
    ex = jnp.exp(v2 - v1)
    g1 = p_group / (1.0 + ex)
    g2 = p_group * ex / (1.0 + ex)
    e1 = i1 - MOE_GROUPS
    e2 = i2 - MOE_GROUPS

    erow = lax.broadcasted_iota(jnp.int32, (N_EXPERTS, tq), 0).astype(F32)
    oh1 = jnp.where(erow == e1, 1.0, 0.0)
    oh2 = jnp.where(erow == e2, 1.0, 0.0)
    oh = oh1 + oh2
    ri = lax.broadcasted_iota(jnp.int32, (tq, tq), 0)
    ci = lax.broadcasted_iota(jnp.int32, (tq, tq), 1)
    earlier = jnp.where(ri < ci, 1.0, 0.0).astype(BF16)
    before = _dot(oh.astype(BF16), earlier) + run_ref[...]
    r1 = jnp.sum(before * oh1, axis=0, keepdims=True)
    r2 = jnp.sum(before * oh2, axis=0, keepdims=True)
    run_ref[...] = run_ref[...] + jnp.sum(oh, axis=1, keepdims=True)
    cnt_ref[...] = jnp.broadcast_to(run_ref[...], cnt_ref.shape)

    meta = jnp.concatenate([e1, e2, g1, g2, r1, r2, jnp.zeros((2, tq), F32)], axis=0)
    meta_ref[0] = meta
    info_ref[...] = jnp.concatenate([meta, jnp.zeros((LANES - 8, tq), F32)], axis=0).T


def _route(x, shf, scf, gain, wrh, wrl, br, *, tile, rows_per_mod, proj=None):
    n, d = x.shape
    nt = n // tile
    mrows = shf.shape[1]
    mod_spec = pl.BlockSpec((1, mrows, d), lambda i: ((i * tile) // rows_per_mod, 0, 0))
    row_spec = pl.BlockSpec((tile, d), lambda i: (i, 0))

    def const(a):
        nd = a.ndim
        return pl.BlockSpec(a.shape, lambda i: (0,) * nd)

    info_spec = pl.BlockSpec((tile, LANES), lambda i: (i, 0))
    meta_spec = pl.BlockSpec((1, 8, tile), lambda i: (i, 0, 0))
    cnt_spec = pl.BlockSpec((N_EXPERTS, LANES), lambda i: (0, 0))
    outs_shape = [jax.ShapeDtypeStruct((n, d), F32), jax.ShapeDtypeStruct((n, LANES), F32),
                  jax.ShapeDtypeStruct((nt, 8, tile), F32), jax.ShapeDtypeStruct((N_EXPERTS, LANES), F32)]
    outs_spec = [row_spec, info_spec, meta_spec, cnt_spec]
    if proj is None:
        ins = [x, shf, scf, gain, wrh, wrl, br]
        in_specs = [row_spec, mod_spec, mod_spec, const(gain), const(wrh), const(wrl), const(br)]
    else:
        o, wout, gtm = proj
        ins = [x, o, wout, gtm, shf, scf, gain, wrh, wrl, br]
        in_specs = [row_spec, pl.BlockSpec((tile, o.shape[1]), lambda i: (i, 0)), const(wout), mod_spec,
                    mod_spec, mod_spec, const(gain), const(wrh), const(wrl), const(br)]
        outs_shape = [jax.ShapeDtypeStruct((n, d), F32)] + outs_shape
        outs_spec = [row_spec] + outs_spec
    return pl.pallas_call(
        functools.partial(_route_kernel, with_proj=proj is not None),
        grid=(nt,),
        in_specs=in_specs,
        out_specs=outs_spec,
        out_shape=outs_shape,
        scratch_shapes=[pltpu.VMEM((N_EXPERTS, 1), F32)],
        compiler_params=_cparams(1),
        name="route",
    )(*ins)


def _dispatch_kernel(*refs, tiles, first_steps):
    xs_ref, sem = refs[-2:]
    i = pl.program_id(0)

    def copy_tile(slot_ref, hf_ref, tile):
        def issue(g, c):
            for u in range(ROW_DMA_UNROLL):
                r = g * ROW_DMA_UNROLL + u
                for k in range(TOP_K):
                    s = slot_ref[0, 0, k * tile + r]
                    pltpu.make_async_copy(hf_ref.at[pl.ds(r, 1), :], xs_ref.at[pl.ds(s, 1), :], sem).start()
            return c

        lax.fori_loop(0, tile // ROW_DMA_UNROLL, issue, 0)
        for k in range(TOP_K):
            pltpu.make_async_copy(hf_ref, xs_ref.at[pl.ds(0, tile), :], sem).wait()

    for g, tile in enumerate(tiles):
        lo, hi = first_steps[g], first_steps[g + 1]

        @pl.when((i >= lo) & (i < hi))
        def _(g=g, tile=tile):
            copy_tile(refs[2 * g], refs[2 * g + 1], tile)


def _dispatch(hfs, slots, tiles):
    d = hfs[0].shape[1]
    n_tiles = [hf.shape[0] // t for hf, t in zip(hfs, tiles)]
    first_steps = [sum(n_tiles[:g]) for g in range(len(hfs) + 1)]
    ins, in_specs = [], []
    for g, (hf, sl, tile) in enumerate(zip(hfs, slots, tiles)):
        def blk(i, lo=first_steps[g], last=n_tiles[g] - 1):
            return jnp.clip(i - lo, 0, last)

        ins += [sl.reshape(n_tiles[g], 1, TOP_K * tile), hf]
        in_specs += [pl.BlockSpec((1, 1, TOP_K * tile), lambda i, blk=blk: (blk(i), 0, 0),
                                  memory_space=pltpu.SMEM),
                     pl.BlockSpec((tile, d), lambda i, blk=blk: (blk(i), 0))]
    return pl.pallas_call(
        functools.partial(_dispatch_kernel, tiles=tuple(tiles), first_steps=tuple(first_steps)),
        grid=(first_steps[-1],),
        in_specs=in_specs,
        out_specs=pl.BlockSpec(memory_space=pl.ANY),
        out_shape=jax.ShapeDtypeStruct((TOP_K * sum(hf.shape[0] for hf in hfs), d), F32),
        scratch_shapes=[pltpu.SemaphoreType.DMA(())],
        compiler_params=_cparams(1),
        name="moe_dispatch",
    )(*ins)


def _gmm_kernel(wt_ref, we_ref, wlo_ref, whi_ref, xs_ref, wg_ref, wu_ref, wd_ref, y_ref,
                wgb_ref, wub_ref, wdb_ref):
    w = pl.program_id(0)
    prev = jnp.maximum(w - 1, 0)
    new_expert = (w == 0) | (we_ref[w] != we_ref[prev])
    new_tile = (w == 0) | (wt_ref[w] != wt_ref[prev])

    @pl.when(new_expert)
    def _():
        wgb_ref[...] = wg_ref[0, 0].astype(BF16)
        wub_ref[...] = wu_ref[0, 0].astype(BF16)
        wdb_ref[...] = wd_ref[0, 0].astype(BF16)

    @pl.when(new_tile)
    def _():
        y_ref[...] = jnp.zeros(y_ref.shape, F32)

    lo = wlo_ref[w]
    hi = whi_ref[w]

    @pl.when(hi > lo)
    def _():
        x = xs_ref[...].astype(BF16)
        g = _dot(x, wgb_ref[...])
        u = _dot(x, wub_ref[...])
        yv = _dot((_silu(g) * u).astype(BF16), wdb_ref[...])
        row = lax.broadcasted_iota(jnp.int32, (x.shape[0], 1), 0)
        y_ref[...] = y_ref[...] + jnp.where((row >= lo) & (row < hi), yv, 0.0)


def _gmm(xs, work, w_gate, w_up, w_down, *, tile, layer):
    m, d = xs.shape
    ff = w_gate.shape[-1]
    n_work = work[0].shape[0]
    grid_spec = pltpu.PrefetchScalarGridSpec(
        num_scalar_prefetch=4,
        grid=(n_work,),
        in_specs=[pl.BlockSpec((tile, d), lambda w, wt, we, wlo, whi: (wt[w], 0)),
                  pl.BlockSpec((1, 1, d, ff), lambda w, wt, we, wlo, whi: (layer, we[w], 0, 0)),
                  pl.BlockSpec((1, 1, d, ff), lambda w, wt, we, wlo, whi: (layer, we[w], 0, 0)),
                  pl.BlockSpec((1, 1, ff, d), lambda w, wt, we, wlo, whi: (layer, we[w], 0, 0))],
        out_specs=pl.BlockSpec((tile, d), lambda w, wt, we, wlo, whi: (wt[w], 0)),
        scratch_shapes=[pltpu.VMEM((d, ff), BF16), pltpu.VMEM((d, ff), BF16), pltpu.VMEM((ff, d), BF16)],
    )
    return pl.pallas_call(
        _gmm_kernel,
        grid_spec=grid_spec,
        out_shape=jax.ShapeDtypeStruct((m, d), F32),
        compiler_params=_cparams(1),
        name="moe_gmm",
    )(*work, xs, w_gate, w_up, w_down)


def _work_list(counts, n_slots, tile):
    n_tiles = n_slots // tile
    n_work = n_tiles + N_EXPERTS - 1
    ends = jnp.cumsum(counts)
    starts = ends - counts
    first_tile = starts // tile
    last_tile = jnp.maximum(ends - 1, 0) // tile
    n_items = jnp.where(counts > 0, last_tile - first_tile + 1, 0)
    item_end = jnp.cumsum(n_items)
    item_start = item_end - n_items
    w = jnp.arange(n_work, dtype=jnp.int32)
    used = w < item_end[-1]
    e = jnp.minimum(jnp.sum(w[:, None] >= item_end[None, :], axis=1), N_EXPERTS - 1).astype(jnp.int32)
    is_e = e[:, None] == jnp.arange(N_EXPERTS, dtype=jnp.int32)[None, :]

    def of_e(v):
        return jnp.sum(jnp.where(is_e, v[None, :], 0), axis=1)

    t = of_e(first_tile) + (w - of_e(item_start))
    lo = jnp.maximum(of_e(starts), t * tile) - t * tile
    hi = jnp.minimum(of_e(ends), (t + 1) * tile) - t * tile
    last_e = jnp.max(jnp.where(counts > 0, jnp.arange(N_EXPERTS), 0)).astype(jnp.int32)
    wt = jnp.where(used, t, n_tiles - 1).astype(jnp.int32)
    we = jnp.where(used, e, last_e).astype(jnp.int32)
    wlo = jnp.where(used, lo, 0).astype(jnp.int32)
    whi = jnp.where(used, hi, 0).astype(jnp.int32)
    return (wt, we, wlo, whi), starts


def _combine_kernel(*refs, tile, final):
    if final:
        slot_ref, nslot_ref, x_ref, gt_ref, info_ref, fg_ref, y_ref, out_ref, ybuf, sem = refs
    else:
        slot_ref, nslot_ref, x_ref, gt_ref, info_ref, y_ref, out_ref, ybuf, sem = refs
    i = pl.program_id(0)
    cur = i % 2

    def gather(slots_ref, buf):
        def issue(g, c):
            for u in range(ROW_DMA_UNROLL):
                r = g * ROW_DMA_UNROLL + u
                for k in range(TOP_K):
                    s = slots_ref[0, 0, k * tile + r]
                    pltpu.make_async_copy(y_ref.at[pl.ds(s, 1), :], ybuf.at[buf, k, pl.ds(r, 1), :],
                                          sem.at[buf]).start()
            return c

        lax.fori_loop(0, tile // ROW_DMA_UNROLL, issue, 0)

    @pl.when(i == 0)
    def _():
        gather(slot_ref, cur)

    @pl.when(i + 1 < pl.num_programs(0))
    def _():
        gather(nslot_ref, 1 - cur)

    for k in range(TOP_K):
        pltpu.make_async_copy(y_ref.at[pl.ds(0, tile), :], ybuf.at[cur, k], sem.at[cur]).wait()

    info = info_ref[...]
    ffn = info[:, 2:3] * ybuf[cur, 0] + info[:, 3:4] * ybuf[cur, 1]
    out = x_ref[...] + gt_ref[0] * ffn
    if final:
        out = _rms(out, fg_ref[...])
    out_ref[...] = out


def _combine(x, gt, info, y, slots, *, tile, rows_per_mod, final_gain=None):
    n, d = x.shape
    nt = n // tile
    mrows = gt.shape[1]
    final = final_gain is not None
    slots3 = slots.reshape(nt, 1, TOP_K * tile)
    in_specs = [pl.BlockSpec((1, 1, TOP_K * tile), lambda i: (i, 0, 0), memory_space=pltpu.SMEM),
                pl.BlockSpec((1, 1, TOP_K * tile), lambda i: (jnp.minimum(i + 1, nt - 1), 0, 0),
                             memory_space=pltpu.SMEM),
                pl.BlockSpec((tile, d), lambda i: (i, 0)),
                pl.BlockSpec((1, mrows, d), lambda i: ((i * tile) // rows_per_mod, 0, 0)),
                pl.BlockSpec((tile, LANES), lambda i: (i, 0))]
    ins = [slots3, slots3, x, gt, info]
    if final:
        in_specs.append(pl.BlockSpec(final_gain.shape, lambda i: (0, 0)))
        ins.append(final_gain)
    in_specs.append(pl.BlockSpec(memory_space=pl.ANY))
    ins.append(y)
    return pl.pallas_call(
        functools.partial(_combine_kernel, tile=tile, final=final),
        grid=(nt,),
        in_specs=in_specs,
        out_specs=pl.BlockSpec((tile, d), lambda i: (i, 0)),
        out_shape=jax.ShapeDtypeStruct((n, d), F32),
        scratch_shapes=[pltpu.VMEM((2, TOP_K, tile, d), F32), pltpu.SemaphoreType.DMA((2,))],
        compiler_params=_cparams(1),
        name="moe_combine",
    )(*ins)


def _moe(groups, rw, experts, *, layer, final_gain=None):
    gain, wrh, wrl, br = rw
    routed = []
    for g in groups:
        shf, scf, _ = g["mods"]
        res = _route(g["x"], shf, scf, gain, wrh, wrl, br, tile=g["tile"], rows_per_mod=g["rows_per_mod"],
                     proj=g["proj"])
        x = g["x"]
        if g["proj"] is not None:
            x, res = res[0], res[1:]
        hf, info, meta, cnt = res
        routed.append((x, hf, info, meta, cnt[:, 0].astype(jnp.int32)))
    n_rows = TOP_K * sum(r[0].shape[0] for r in routed)
    gmm_tile = _pick(n_rows, 512)
    work, base = _work_list(sum(r[4] for r in routed), n_rows, gmm_tile)
    base = base.astype(jnp.int32)
    slots = []
    for _, _, _, meta, counts in routed:
        eid = meta[:, 0:TOP_K, :].astype(jnp.int32)
        rank = meta[:, 4:4 + TOP_K, :].astype(jnp.int32)
        start_of = jnp.zeros(eid.shape, jnp.int32)
        for e in range(N_EXPERTS):
            start_of = jnp.where(eid == e, base[e], start_of)
        slots.append(start_of + rank)
        base = base + counts
    xs = _dispatch([r[1] for r in routed], slots, [g["tile"] for g in groups])
    y = _gmm(xs, work, *experts, tile=gmm_tile, layer=layer)
    return [_combine(r[0], g["mods"][2], r[2], y, sl, tile=g["tile"], rows_per_mod=g["rows_per_mod"],
                     final_gain=final_gain)
            for g, r, sl in zip(groups, routed, slots)]


def _rope_turn(blk, c, s):
    return blk * c + pltpu.roll(blk, LANES - QK_ROPE, 1) * s


def _mla_proj_kernel(*refs, sample):
    if sample:
        (x_ref, sh_ref, sc_ref, gain_ref, win_ref, qg_ref, wuq_ref, kvg_ref, c_ref, s_ref,
         ckv_ref, kr_ref, q_ref) = refs
    else:
        (x_ref, sh_ref, sc_ref, gain_ref, win_ref, qg_ref, wuq_ref, kvg_ref, c_ref, s_ref, wuk_ref, wuvt_ref,
         vonet_ref, ct_ref, st_ref, ckv_ref, kr_ref, q_ref, k_ref, v_ref) = refs
    q_lora = qg_ref.shape[1]
    kv_lora = kvg_ref.shape[1]
    h = _rms(x_ref[...], gain_ref[...]) * (1.0 + sc_ref[0]) + sh_ref[0]
    proj = _dot(h.astype(BF16), win_ref[...])
    qn = _rms(proj[:, :q_lora], qg_ref[...])
    ckv = _rms(proj[:, q_lora:q_lora + kv_lora], kvg_ref[...])
    c = c_ref[...]
    s = s_ref[...]
    kf = _rope_turn(proj[:, q_lora + kv_lora:], c, s)
    ckv_ref[...] = ckv
    kr_ref[...] = kf[:, QK_NOPE:QK_NOPE + QK_ROPE]
    qn_b = qn.astype(BF16)
    if sample:
        q = _dot(qn_b, wuq_ref[...])
        for hh in range(MLA_HEADS):
            cols = slice(hh * HEAD_PAD, (hh + 1) * HEAD_PAD)
            q_ref[:, cols] = _rope_turn(q[:, cols], c, s) * SM_SCALE
    else:
        ckv_b = ckv.astype(BF16)
        kn = _dot(ckv_b, wuk_ref[...])
        qt = _dot_nt(wuq_ref[...], qn_b)
        v_ref[0] = (_dot_nt(wuvt_ref[...], ckv_b) + vonet_ref[...]).astype(BF16)
        ct = ct_ref[...]
        st = st_ref[...]
        for hh in range(MLA_HEADS):
            cols = slice(hh * HEAD_PAD, (hh + 1) * HEAD_PAD)
            k_ref[:, cols] = (kn[:, cols] + kf).astype(BF16)
            blk = qt[cols, :]
            turned = jnp.concatenate([blk[QK_ROPE:], blk[:QK_ROPE]], axis=0)
            q_ref[0, cols, :] = ((blk * ct + turned * st) * (SM_SCALE * LOG2E)).astype(BF16)


def _mla_proj(x, sh, sc, gain, win, qg, wuq, kvg, ctab, stab, *, tile, rows_per_mod, tab_tiles, kv_w=None):
    n, d = x.shape
    nt = n // tile
    mrows = sh.shape[1]
    sample = kv_w is None
    kv_lora = kvg.shape[1]

    def const(a):
        nd = a.ndim
        return pl.BlockSpec(a.shape, lambda i: (0,) * nd)

    row = lambda width: pl.BlockSpec((tile, width), lambda i: (i, 0))
    mod_spec = pl.BlockSpec((1, mrows, d), lambda i: ((i * tile) // rows_per_mod, 0, 0))
    tab_spec = pl.BlockSpec((tile, LANES), lambda i: (i % tab_tiles, 0))
    ins = [x, sh, sc, gain, win, qg, wuq, kvg, ctab, stab]
    in_specs = [row(d), mod_spec, mod_spec, const(gain), const(win), const(qg), const(wuq), const(kvg),
                tab_spec, tab_spec]
    qw = MLA_HEADS * HEAD_PAD
    out_shape = [jax.ShapeDtypeStruct((n, kv_lora), F32), jax.ShapeDtypeStruct((n, QK_ROPE), F32)]
    out_specs = [row(kv_lora), row(QK_ROPE)]
    if sample:
        out_shape.append(jax.ShapeDtypeStruct((n, qw), F32))
        out_specs.append(row(qw))
    else:
        wuk, wuvt, vonet, ctab_t, stab_t = kv_w
        seq_len = tab_tiles * tile
        tab_t_spec = pl.BlockSpec((LANES, tile), lambda i: (0, i % tab_tiles))
        by_seq = pl.BlockSpec((1, qw, tile), lambda i: (i // tab_tiles, 0, i % tab_tiles))
        ins += [wuk, wuvt, vonet, ctab_t, stab_t]
        in_specs += [const(wuk), const(wuvt), const(vonet), tab_t_spec, tab_t_spec]
        out_shape += [jax.ShapeDtypeStruct((n // seq_len, qw, seq_len), BF16), jax.ShapeDtypeStruct((n, qw), BF16),
                      jax.ShapeDtypeStruct((n // seq_len, qw, seq_len), BF16)]
        out_specs += [by_seq, row(qw), by_seq]
    return pl.pallas_call(
        functools.partial(_mla_proj_kernel, sample=sample),
        grid=(nt,),
        in_specs=in_specs,
        out_specs=out_specs,
        out_shape=out_shape,
        compiler_params=_cparams(1),
        name="mla_proj",
    )(*ins)


def _attn_kernel(q_ref, k_ref, v_ref, o_ref, sa_ref, sb_ref, m_ref, acc_ref, *, tq, tk):
    i = pl.program_id(2)
    heads = range(2)

    def cols(hh):
        return slice(hh * HEAD_PAD, (hh + 1) * HEAD_PAD)

    def scores(j, dst_ref):
        keys = pl.ds(pl.multiple_of(j * tk, tk), tk)
        for hh in heads:
            dst_ref[hh] = _dot(k_ref[0, keys, cols(hh)], q_ref[0, cols(hh), :])

    def absorb(j, src_ref, masked):
        keys = pl.ds(pl.multiple_of(j * tk, tk), tk)
        if masked:
            keep = lax.broadcasted_iota(jnp.int32, (tk, tq), 0) <= lax.broadcasted_iota(jnp.int32, (tk, tq), 1)
        for hh in heads:
            s = src_ref[hh]
            if masked:
                s = jnp.where(keep, s, -jnp.inf)
            m = m_ref[hh]
            m_new = jnp.maximum(m, jnp.max(s, axis=0, keepdims=True))
            p = jnp.exp2(s - m_new)
            acc_ref[hh] = jnp.exp2(m - m_new) * acc_ref[hh] + _dot(v_ref[0, cols(hh), keys], p.astype(BF16))
            m_ref[hh] = m_new

    m_ref[...] = jnp.full(m_ref.shape, -jnp.inf, F32)
    acc_ref[...] = jnp.zeros(acc_ref.shape, F32)
    scores(0, sa_ref)

    def pair(j):
        scores(j + 1, sb_ref)
        absorb(j, sa_ref, False)
        scores(j + 2, sa_ref)
        absorb(j + 1, sb_ref, False)

    done = 0
    pairs_per_trip = ATTN_PAIRS_PER_TRIP
    while pairs_per_trip >= 1:
        def trip(t, c, start=done, n=pairs_per_trip):
            for q in range(n):
                pair(start + 2 * (n * t + q))
            return c

        trips = (i - done) // (2 * pairs_per_trip)
        lax.fori_loop(0, trips, trip, 0)
        done = done + 2 * pairs_per_trip * trips
        pairs_per_trip //= 2

    @pl.when(i % 2 == 1)
    def _():
        scores(i, sb_ref)
        absorb(i - 1, sa_ref, False)
        absorb(i, sb_ref, True)

    @pl.when(i % 2 == 0)
    def _():
        absorb(i, sa_ref, True)

    out_t = jnp.concatenate([acc_ref[hh][:V_DIM] / acc_ref[hh][V_DIM:V_DIM + 1] for hh in heads], axis=0)
    o_ref[0] = out_t.T.astype(o_ref.dtype)


def _attention(q_t, k, v_t, *, tq, tk):
    assert tq == tk, "one diagonal key tile per query tile"
    b, s, _ = k.shape
    pairs = MLA_HEADS // 2
    return pl.pallas_call(
        functools.partial(_attn_kernel, tq=tq, tk=tk),
        grid=(b, pairs, s // tq),
        in_specs=[pl.BlockSpec((1, 2 * HEAD_PAD, tq), lambda bi, p, i: (bi, p, i)),
                  pl.BlockSpec((1, s, 2 * HEAD_PAD), lambda bi, p, i: (bi, 0, p)),
                  pl.BlockSpec((1, 2 * HEAD_PAD, s), lambda bi, p, i: (bi, p, 0))],
        out_specs=pl.BlockSpec((1, tq, 2 * V_DIM), lambda bi, p, i: (bi, i, p)),
        out_shape=jax.ShapeDtypeStruct((b, s, MLA_HEADS * V_DIM), BF16),
        scratch_shapes=[pltpu.VMEM((2, tk, tq), F32), pltpu.VMEM((2, tk, tq), F32),
                        pltpu.VMEM((2, 1, tq), F32), pltpu.VMEM((2, HEAD_PAD, tq), F32)],
        compiler_params=_cparams(3),
        name="prompt_attn",
    )(q_t, k, v_t)


def _qabs_kernel(q_ref, m_ref, o_ref):
    o_ref[0] = _dot(q_ref[...].astype(BF16), m_ref[0])


def _qabs(q, mats):
    n = q.shape[0]
    width = mats.shape[-1]
    return pl.pallas_call(
        _qabs_kernel,
        grid=(MLA_HEADS,),
        in_specs=[pl.BlockSpec((n, HEAD_PAD), lambda h: (0, h)),
                  pl.BlockSpec((1, HEAD_PAD, width), lambda h: (h, 0, 0))],
        out_specs=pl.BlockSpec((1, n, width), lambda h: (h, 0, 0)),
        out_shape=jax.ShapeDtypeStruct((MLA_HEADS, n, width), F32),
        compiler_params=_cparams(1),
        name="sample_qabs",
    )(q, mats)


def _paged_attn_kernel(pt_ref, q_ref, cn_ref, rn_ref, ck_hbm, kr_hbm, o_ref,
                       ckbuf, krbuf, m_ref, l_ref, acc_ref, sem, *, group, pages, page, n_chunks, t_new, t_pad):
    bg = pl.program_id(0)
    c = pl.program_id(1)
    step = bg * n_chunks + c
    n_steps = pl.num_programs(0) * n_chunks
    slot = step % PAGE_RING
    kv_lora = ckbuf.shape[-1]

    def start_chunk(st):
        bgi, ci, sl = st // n_chunks, st % n_chunks, st % PAGE_RING
        for g in range(group):
            for p in range(pages):
                phys = pt_ref[bgi * group + g, ci * pages + p]
                pltpu.make_async_copy(ck_hbm.at[phys], ckbuf.at[sl, g, pl.ds(p * page, page), :],
                                      sem.at[0, sl]).start()
                pltpu.make_async_copy(kr_hbm.at[phys], krbuf.at[sl, g, p], sem.at[1, sl]).start()

    @pl.when(step == 0)
    def _():
        for ahead in range(PAGE_RING - 1):
            @pl.when(ahead < n_steps)
            def _():
                start_chunk(step + ahead)

    pltpu.make_async_copy(ckbuf.at[slot], ckbuf.at[slot], sem.at[0, slot]).wait()
    pltpu.make_async_copy(krbuf.at[slot], krbuf.at[slot], sem.at[1, slot]).wait()

    @pl.when(c == 0)
    def _():
        m_ref[...] = jnp.full(m_ref.shape, -jnp.inf, F32)
        l_ref[...] = jnp.zeros(l_ref.shape, F32)
        acc_ref[...] = jnp.zeros(acc_ref.shape, F32)

    def absorb(g, s, values):
        m = m_ref[g]
        m_new = jnp.maximum(m, jnp.max(s, axis=1, keepdims=True))
        alpha = jnp.exp(m - m_new)
        p = jnp.exp(s - m_new)
        l_ref[g] = alpha * l_ref[g] + jnp.sum(p, axis=1, keepdims=True)
        acc_ref[g] = alpha * acc_ref[g] + _dot(p.astype(BF16), values)
        m_ref[g] = m_new

    cks, scores = [], []
    for g in range(group):
        qa = q_ref[g]
        q_lat = qa[:, :kv_lora].astype(BF16)
        q_rope = qa[:, kv_lora:kv_lora + QK_ROPE].astype(BF16)
        ck = ckbuf[slot, g].astype(BF16)
        kr_t = jnp.concatenate([krbuf[slot, g, p] for p in range(pages)], axis=1).astype(BF16)
        cks.append(ck)
        scores.append(_dot_nt(q_lat, ck) + _dot(q_rope, kr_t))

    nxt = step + PAGE_RING - 1

    @pl.when(nxt < n_steps)
    def _():
        start_chunk(nxt)

    for g in range(group):
        absorb(g, scores[g], cks[g])

    @pl.when(c == n_chunks - 1)
    def _():
        rows = q_ref.shape[1]
        t_q = lax.broadcasted_iota(jnp.int32, (rows, t_pad), 0) % t_new
        t_k = lax.broadcasted_iota(jnp.int32, (rows, t_pad), 1)
        for g in range(group):
            qa = q_ref[g]
            q_lat = qa[:, :kv_lora].astype(BF16)
            q_rope = qa[:, kv_lora:kv_lora + QK_ROPE].astype(BF16)
            cn = cn_ref[g].astype(BF16)
            s = _dot_nt(q_lat, cn) + _dot_nt(q_rope, rn_ref[g].astype(BF16))
            absorb(g, jnp.where(t_k <= t_q, s, -jnp.inf), cn)
            o_ref[g] = acc_ref[g] / l_ref[g]


def _paged_attn(page_table, qabs, ckv_new, kr_new, cache_ck, cache_kr_t, *, group, pages, t_new):
    b, rows, width = qabs.shape
    n_pages = page_table.shape[1]
    n_chunks = n_pages // pages
    page, kv_lora = cache_ck.shape[1:]
    t_pad = ckv_new.shape[1]
    grid_spec = pltpu.PrefetchScalarGridSpec(
        num_scalar_prefetch=1,
        grid=(b // group, n_chunks),
        in_specs=[pl.BlockSpec((group, rows, width), lambda bi, ci, pt: (bi, 0, 0)),
                  pl.BlockSpec((group, t_pad, kv_lora), lambda bi, ci, pt: (bi, 0, 0)),
                  pl.BlockSpec((group, t_pad, QK_ROPE), lambda bi, ci, pt: (bi, 0, 0)),
                  pl.BlockSpec(memory_space=pl.ANY),
                  pl.BlockSpec(memory_space=pl.ANY)],
        out_specs=pl.BlockSpec((group, rows, kv_lora), lambda bi, ci, pt: (bi, 0, 0)),
        scratch_shapes=[pltpu.VMEM((PAGE_RING, group, pages * page, kv_lora), F32),
                        pltpu.VMEM((PAGE_RING, group, pages, QK_ROPE, page), F32),
                        pltpu.VMEM((group, rows, 1), F32), pltpu.VMEM((group, rows, 1), F32),
                        pltpu.VMEM((group, rows, kv_lora), F32),
                        pltpu.SemaphoreType.DMA((2, PAGE_RING))],
    )
    return pl.pallas_call(
        functools.partial(_paged_attn_kernel, group=group, pages=pages, page=page, n_chunks=n_chunks, t_new=t_new,
                          t_pad=t_pad),
        grid_spec=grid_spec,
        out_shape=jax.ShapeDtypeStruct((b, rows, kv_lora), F32),
        compiler_params=_cparams(2),
        name="paged_attn",
    )(page_table, qabs, ckv_new, kr_new, cache_ck, cache_kr_t)


def _vup_kernel(o_ref, w_ref, out_ref):
    lat = jnp.concatenate([o_ref[0], o_ref[1]], axis=-1).astype(BF16)
    out_ref[...] = _dot(lat, w_ref[0]).astype(out_ref.dtype)


def _vup(o_lat, w_pairs):
    h, n, c = o_lat.shape
    return pl.pallas_call(
        _vup_kernel,
        grid=(h // 2,),
        in_specs=[pl.BlockSpec((2, n, c), lambda p: (p, 0, 0)),
                  pl.BlockSpec((1, 2 * c, 2 * V_DIM), lambda p: (p, 0, 0))],
        out_specs=pl.BlockSpec((n, 2 * V_DIM), lambda p: (0, p)),
        out_shape=jax.ShapeDtypeStruct((n, h * V_DIM), BF16),
        compiler_params=_cparams(1),
        name="sample_vup",
    )(o_lat, w_pairs)


def _block_diag(w):
    g, a, b = w.shape
    out = jnp.zeros((g * a, g * b), w.dtype)
    for i in range(g):
        out = out.at[i * a:(i + 1) * a, i * b:(i + 1) * b].set(w[i])
    return out


def _swap_halves(w):
    half = w.shape[-1] // 2
    return jnp.concatenate([w[..., half:], w[..., :half]], axis=-1)


def _rope_tables(pos):
    half = QK_ROPE // 2
    inv_freq = ROPE_THETA ** (-jnp.arange(half, dtype=F32) / half)
    ang = pos.astype(F32)[:, None] * inv_freq[None, :]
    cos, sin = jnp.cos(ang), jnp.sin(ang)
    n = pos.shape[0]
    ctab = jnp.concatenate([jnp.ones((n, QK_NOPE), F32), cos, cos, jnp.zeros((n, QK_ROPE), F32)], axis=1)
    stab = jnp.concatenate([jnp.zeros((n, QK_NOPE), F32), -sin, sin, jnp.zeros((n, QK_ROPE), F32)], axis=1)
    return ctab, stab


def _router_weights(w_group, b_group, w_expert, b_expert):
    d = w_group.shape[0]
    we = jnp.transpose(w_expert, (1, 0, 2)).reshape(d, N_EXPERTS)
    w = jnp.concatenate([w_group, we, jnp.zeros((d, LANES - MOE_GROUPS - N_EXPERTS), F32)], axis=1)
    bias = jnp.concatenate([b_group, b_expert.reshape(-1), jnp.zeros((LANES - MOE_GROUPS - N_EXPERTS,), F32)])
    hi = w.astype(BF16)
    lo = (w - hi.astype(F32)).astype(BF16)
    return hi, lo, bias.reshape(1, LANES)


def _pick(n, pref):
    t = min(n, pref)
    while n % t:
        t //= 2
    return t


def kernel(x_prompt, x_sample, state_pool, state_conv, state_lru, cache_kv_latent, cache_k_rope, page_table,
           c_prompt, c_sample, ada_w, ada_b, norm_mix, norm_ffn, ab_w_in, ab_w_out, pool_w, pool_scale,
           conv_w, conv_b, lru_w_r, lru_b_r, lru_w_i, lru_b_i, lru_lambda,
           mla_w_in, mla_q_norm, mla_w_uq, mla_kv_norm, mla_w_uk, mla_w_uv, mla_w_out,
           router_w_group, router_b_group, router_w_expert, router_b_expert,
           moe_w_gate, moe_w_up, moe_w_down, final_norm):
    bp, seq, d = x_prompt.shape
    bs, t_new, _ = x_sample.shape
    depth = ada_w.shape[0]
    pw = pool_scale.shape[-1]
    page = cache_kv_latent.shape[2]
    past_len = page_table.shape[1] * page
    q_lora = mla_q_norm.shape[-1]
    kv_lora = mla_kv_norm.shape[-1]

    n_c = bp + bs
    n_c_pad = -(-n_c // 8) * 8
    c_all = jnp.concatenate([c_prompt, c_sample, jnp.zeros((n_c_pad - n_c, d), F32)], axis=0)
    mod = _ada(c_all, ada_w, ada_b)

    def mods(layer, lo, hi):
        return [mod[layer, lo:hi, k * d:(k + 1) * d] for k in range(6)]

    n_p = bp * seq
    n_s = bs * t_new
    tile_p = _pick(seq, 512)
    tile_s = _pick(n_s, 512)

    xp = x_prompt.reshape(n_p, d)
    xs = x_sample.reshape(n_s, d)
    outs = {}

    for layer in range(depth):
        sh_m, sc_m, gt_m, sh_f, sc_f, gt_f = mods(layer, 0, bp)
        sh_ms, sc_ms, gt_ms, sh_fs, sc_fs, gt_fs = mods(layer, bp, bp + bs)
        rep = lambda v: jnp.repeat(v, t_new, axis=0)[None]
        per_seq = lambda v: v[:, None, :]
        rw = (norm_ffn[layer].reshape(1, d),) + _router_weights(
            router_w_group[layer], router_b_group[layer], router_w_expert[layer], router_b_expert[layer])
        experts = (moe_w_gate, moe_w_up, moe_w_down)
        final_gain = final_norm.reshape(1, d) if layer == depth - 1 else None
        ffn_mods_p = (per_seq(sh_f), per_seq(sc_f), per_seq(gt_f))
        ffn_mods_s = (rep(sh_fs), rep(sc_fs), rep(gt_fs))

        if layer % 2 == 0:
            e = layer // 2
            wts = (norm_mix[layer].reshape(1, d), ab_w_in[e].astype(BF16), _block_diag(pool_w[e]).astype(BF16),
                   pool_scale[e].reshape(1, pw), conv_w[e], conv_b[e].reshape(1, pw),
                   jnp.concatenate([_block_diag(lru_w_r[e]), _block_diag(lru_w_i[e])], axis=1).astype(BF16),
                   jnp.concatenate([lru_b_r[e], lru_b_i[e]]).reshape(1, 2 * pw),
                   lru_lambda[e].reshape(1, pw), ab_w_out[e].astype(BF16))
            x1p, pool_p, conv_p, lru_p = _mix0(
                xp.reshape(bp, seq, d), per_seq(sh_m), per_seq(sc_m), per_seq(gt_m),
                jnp.zeros((bp, POOL_HALO - 1, pw), F32), jnp.zeros((bp, CONV_WIDTH - 1, pw), F32),
                jnp.zeros((bp, 1, pw), F32), wts, tt=tile_p, bb=1, start=0)
            tm = lambda a: jnp.swapaxes(a, 0, 1).reshape(1, -1, a.shape[-1])
            x1s, pool_s, conv_s, lru_s = _mix0(
                tm(xs.reshape(bs, t_new, d)), sh_ms[None], sc_ms[None], gt_ms[None],
                tm(state_pool[e]), tm(state_conv[e]), state_lru[e][None], wts, tt=t_new, bb=bs, start=past_len)
            bm = lambda a, n: jnp.swapaxes(a.reshape(n, bs, a.shape[-1]), 0, 1)
            outs.setdefault("pool_p", []).append(pool_p)
            outs.setdefault("pool_s", []).append(bm(pool_s, POOL_HALO - 1))
            outs.setdefault("conv_p", []).append(conv_p)
            outs.setdefault("conv_s", []).append(bm(conv_s, CONV_WIDTH - 1))
            outs.setdefault("lru_p", []).append(lru_p.reshape(bp, pw))
            outs.setdefault("lru_s", []).append(lru_s.reshape(bs, pw))
            xp = x1p.reshape(n_p, d)
            xs = bm(x1s, t_new).reshape(n_s, d)
            proj_p = proj_s = None
        else:
            o = layer // 2
            gain = norm_mix[layer].reshape(1, d)
            w_in = mla_w_in[o]
            w_kr = w_in[:, q_lora + kv_lora:]
            win = jnp.concatenate([w_in[:, :q_lora + kv_lora], jnp.zeros((d, QK_NOPE), F32), w_kr,
                                   _swap_halves(w_kr)], axis=1).astype(BF16)
            wq = mla_w_uq[o].reshape(q_lora, MLA_HEADS, QK_NOPE + QK_ROPE)
            wuq = jnp.concatenate([wq, _swap_halves(wq[..., QK_NOPE:])], axis=-1).reshape(
                q_lora, MLA_HEADS * HEAD_PAD).astype(BF16)
            wuk = jnp.concatenate([mla_w_uk[o], jnp.zeros((kv_lora, MLA_HEADS, HEAD_PAD - QK_NOPE), F32)],
                                  axis=-1).reshape(kv_lora, MLA_HEADS * HEAD_PAD).astype(BF16)
            wuv_t = jnp.concatenate([mla_w_uv[o], jnp.zeros((kv_lora, MLA_HEADS, HEAD_PAD - V_DIM), F32)],
                                    axis=-1).reshape(kv_lora, MLA_HEADS * HEAD_PAD).T.astype(BF16)
            vone_t = jnp.tile((jnp.arange(HEAD_PAD) == V_DIM).astype(F32), MLA_HEADS).reshape(-1, 1)
            qg = mla_q_norm[o].reshape(1, q_lora)
            kvg = mla_kv_norm[o].reshape(1, kv_lora)
            wout = mla_w_out[o].astype(BF16)

            ctab, stab = _rope_tables(jnp.arange(seq, dtype=jnp.int32))
            ckv_p, kr_p, qt_p, k_p, vt_p = _mla_proj(
                xp, per_seq(sh_m), per_seq(sc_m), gain, win, qg, wuq.T, kvg, ctab, stab,
                tile=tile_p, rows_per_mod=seq, tab_tiles=seq // tile_p, kv_w=(wuk, wuv_t, vone_t, ctab.T, stab.T))
            qw = MLA_HEADS * HEAD_PAD
            o_p = _attention(qt_p, k_p.reshape(bp, seq, qw), vt_p, tq=tile_p, tk=tile_p)
            proj_p = (o_p.reshape(n_p, MLA_HEADS * V_DIM), wout, per_seq(gt_m))
            outs.setdefault("lat_p", []).append(ckv_p.reshape(bp, seq, kv_lora))
            outs.setdefault("rope_p", []).append(kr_p.reshape(bp, seq, QK_ROPE))

            pos_s = jnp.tile(past_len + jnp.arange(t_new, dtype=jnp.int32), bs)
            ctab_s, stab_s = _rope_tables(pos_s)
            ckv_s, kr_s, q_s = _mla_proj(
                xs, rep(sh_ms), rep(sc_ms), gain, win, qg, wuq, kvg, ctab_s, stab_s,
                tile=tile_s, rows_per_mod=n_s, tab_tiles=n_s // tile_s)
            aw = kv_lora + HEAD_PAD
            wk_t = jnp.transpose(mla_w_uk[o], (1, 2, 0))
            mats = jnp.zeros((MLA_HEADS, HEAD_PAD, aw), F32)
            mats = mats.at[:, :QK_NOPE, :kv_lora].set(wk_t)
            mats = mats.at[:, QK_NOPE:QK_NOPE + QK_ROPE, kv_lora:kv_lora + QK_ROPE].set(
                jnp.broadcast_to(jnp.eye(QK_ROPE, dtype=F32), (MLA_HEADS, QK_ROPE, QK_ROPE)))
            qabs = _qabs(q_s, mats.astype(BF16))
            qabs = jnp.transpose(qabs.reshape(MLA_HEADS, bs, t_new, aw), (1, 0, 2, 3)).reshape(
                bs, MLA_HEADS * t_new, aw)
            t_pad = 8
            pad_t = lambda a: jnp.pad(a.reshape(bs, t_new, -1), ((0, 0), (0, t_pad - t_new), (0, 0)))
            o_lat = _paged_attn(page_table, qabs, pad_t(ckv_s), pad_t(kr_s), cache_kv_latent[o],
                                jnp.swapaxes(cache_k_rope[o], 1, 2),
                                group=_pick(bs, 4), pages=_pick(page_table.shape[1], 16), t_new=t_new)
            o_lat = jnp.transpose(o_lat.reshape(bs, MLA_HEADS, t_new, kv_lora), (1, 0, 2, 3)).reshape(
                MLA_HEADS, n_s, kv_lora)
            wv = jnp.transpose(mla_w_uv[o], (1, 0, 2))
            w_pairs = jnp.zeros((MLA_HEADS // 2, 2 * kv_lora, 2 * V_DIM), F32)
            w_pairs = w_pairs.at[:, :kv_lora, :V_DIM].set(wv[0::2]).at[:, kv_lora:, V_DIM:].set(wv[1::2])
            o_s = _vup(o_lat, w_pairs.astype(BF16))
            proj_s = (o_s, wout, rep(gt_ms))
            outs.setdefault("lat_s", []).append(ckv_s.reshape(bs, t_new, kv_lora))
            outs.setdefault("rope_s", []).append(kr_s.reshape(bs, t_new, QK_ROPE))

        xp, xs = _moe([dict(x=xp, mods=ffn_mods_p, tile=tile_p, rows_per_mod=seq, proj=proj_p),
                       dict(x=xs, mods=ffn_mods_s, tile=tile_s, rows_per_mod=n_s, proj=proj_s)],
                      rw, experts, layer=layer, final_gain=final_gain)

    st = lambda k: jnp.stack(outs[k])
    return (xp.reshape(bp, seq, d), xs.reshape(bs, t_new, d),
            st("pool_p"), st("pool_s"), st("conv_p"), st("conv_s"), st("lru_p"), st("lru_s"),
            st("lat_p"), st("lat_s"), st("rope_p"), st("rope_s"))
```

```python
import functools

import jax
import jax.numpy as jnp
from jax import lax
from jax.experimental import pallas as pl
from jax.experimental.pallas import tpu as pltpu

F32 = jnp.float32
BF16 = jnp.bfloat16

EPS = 1e-6
POOL_WINDOWS = (2, 4, 8, 16)
POOL_HALO = 16
CONV_WIDTH = 4
CONV_HALO = 8
LRU_C = 8.0
MLA_HEADS = 16
QK_NOPE = 64
QK_ROPE = 32
V_DIM = 64
HEAD_PAD = 128
ROPE_THETA = 10000.0
SM_SCALE = (QK_NOPE + QK_ROPE) ** -0.5
MOE_GROUPS = 4
EXPERTS_PER_GROUP = 8
N_EXPERTS = MOE_GROUPS * EXPERTS_PER_GROUP
TOP_K = 2
LANES = 128
ROW_DMA_UNROLL = 8
PAGE_RING = 3
ATTN_PAIRS_PER_TRIP = 4
LOG2E = 1.4426950408889634
VMEM_LIMIT = 56 * 1024 * 1024


def _cparams(n_axes):
    return pltpu.CompilerParams(dimension_semantics=("arbitrary",) * n_axes,
                                vmem_limit_bytes=VMEM_LIMIT)


def _dot(a, b):
    return jnp.dot(a, b, preferred_element_type=F32)


def _dot_nt(a, b):
    return lax.dot_general(a, b, (((1,), (1,)), ((), ())), preferred_element_type=F32)


def _rms(x, gain):
    return x * lax.rsqrt(jnp.mean(x * x, axis=-1, keepdims=True) + EPS) * gain


def _silu(x):
    return x * jax.nn.sigmoid(x)


def _ada_kernel(c_ref, w_ref, b_ref, o_ref):
    a = _silu(c_ref[...]).astype(BF16)
    o_ref[0] = _dot(a, w_ref[0].astype(BF16)) + b_ref[0]


def _ada(c_all, ada_w, ada_b):
    depth, d, n6 = ada_w.shape
    rows = c_all.shape[0]
    tn = 1024
    return pl.pallas_call(
        _ada_kernel,
        grid=(depth, n6 // tn),
        in_specs=[pl.BlockSpec((rows, d), lambda l, j: (0, 0)),
                  pl.BlockSpec((1, d, tn), lambda l, j: (l, 0, j)),
                  pl.BlockSpec((1, 1, tn), lambda l, j: (l, 0, j))],
        out_specs=pl.BlockSpec((1, rows, tn), lambda l, j: (l, 0, j)),
        out_shape=jax.ShapeDtypeStruct((depth, rows, n6), F32),
        compiler_params=_cparams(2),
        name="ada_mod",
    )(c_all, ada_w, ada_b.reshape(depth, 1, n6))


def _mix0_kernel(x_ref, sh_ref, sc_ref, gt_ref, pool0_ref, conv0_ref, lru0_ref,
                 gain_ref, win_ref, poolw_ref, pscale_ref, convw_ref, convb_ref,
                 wri_ref, bri_ref, lam_ref, wout_ref,
                 x1_ref, pooln_ref, convn_ref, lrun_ref,
                 zp_ref, zc_ref, h_ref, *, tt, bb, start, n_t):
    t = pl.program_id(1)
    rows = tt * bb
    pw = zp_ref.shape[1]
    p0 = POOL_HALO * bb
    c0 = CONV_HALO * bb

    @pl.when(t == 0)
    def _():
        zp_ref[0:bb, :] = jnp.zeros((bb, pw), F32)
        zp_ref[bb:p0, :] = pool0_ref[0]
        zc_ref[0:c0 - (CONV_WIDTH - 1) * bb, :] = jnp.zeros((c0 - (CONV_WIDTH - 1) * bb, pw), F32)
        zc_ref[c0 - (CONV_WIDTH - 1) * bb:c0, :] = conv0_ref[0]
        h_ref[...] = lru0_ref[0]

    def per_row(v):
        return v if bb == 1 else jnp.concatenate([v] * tt, axis=0)

    x = x_ref[0]
    h = _rms(x, gain_ref[...]) * (1.0 + per_row(sc_ref[0])) + per_row(sh_ref[0])
    proj = _dot(h.astype(BF16), win_ref[...])
    u_pool = proj[:, :pw]
    u_x = proj[:, pw:2 * pw]
    u_g = proj[:, 2 * pw:]
    zp_ref[p0:p0 + rows, :] = u_pool
    zc_ref[c0:c0 + rows, :] = u_x

    if bb == 1:
        tix = lax.broadcasted_iota(jnp.int32, (rows, 1), 0)
    else:
        tix = jnp.concatenate([jnp.full((bb, 1), i, jnp.int32) for i in range(tt)], axis=0)
    pos = start + t * tt + tix

    gd = pw // len(POOL_WINDOWS)
    means = []
    if bb == 1 and POOL_WINDOWS == (2, 4, 8, 16):
        sw = zp_ref[0:p0 + rows, :]
        for g, w in enumerate(POOL_WINDOWS):
            sw = sw + pltpu.roll(sw, w // 2, 0)
            cnt = jnp.minimum(pos + 1, w).astype(F32)
            means.append(sw[p0:, :gd] / cnt)
            if g + 1 < len(POOL_WINDOWS):
                sw = sw[:, gd:]
    else:
        for g, w in enumerate(POOL_WINDOWS):
            cols = slice(g * gd, (g + 1) * gd)
            acc = zp_ref[p0:p0 + rows, cols]
            for i in range(1, w):
                acc = acc + zp_ref[p0 - i * bb:p0 - i * bb + rows, cols]
            cnt = jnp.minimum(pos + 1, w).astype(F32)
            means.append(acc / cnt)
    pooled = jnp.concatenate(means, axis=-1) - u_pool
    y_a = _dot(pooled.astype(BF16), poolw_ref[...]) * pscale_ref[...]

    xc = convb_ref[...]
    if bb == 1:
        z_al = zc_ref[0:c0 + rows, :]
        for k in range(CONV_WIDTH):
            back = CONV_WIDTH - 1 - k
            z_k = z_al if back == 0 else pltpu.roll(z_al, back, 0)
            xc = xc + z_k[c0:, :] * convw_ref[k:k + 1, :]
    else:
        for k in range(CONV_WIDTH):
            off = c0 - (CONV_WIDTH - 1 - k) * bb
            xc = xc + zc_ref[off:off + rows, :] * convw_ref[k:k + 1, :]
    pre = _dot(xc.astype(BF16), wri_ref[...]) + bri_ref[...]
    r = jax.nn.sigmoid(pre[:, :pw])
    gi = jax.nn.sigmoid(pre[:, pw:])
    lam = lam_ref[...]
    softplus_neg = jnp.maximum(-lam, 0.0) + jnp.log1p(jnp.exp(-jnp.abs(lam)))
    log_a = -LRU_C * r * softplus_neg
    a = jnp.exp(log_a)
    one_m = 1.0 - a * a
    b = jnp.where(one_m > 0.0, one_m * lax.rsqrt(one_m), 0.0) * gi * xc

    if bb == 1:
        rowi = lax.broadcasted_iota(jnp.int32, (rows, 1), 0)
        s = 1
        while s < rows:
            if s < 8:
                keep = rowi >= s
                b = jnp.where(keep, a * pltpu.roll(b, s, 0) + b, b)
                a = jnp.where(keep, a * pltpu.roll(a, s, 0), a)
            else:
                b = jnp.concatenate([b[:s], a[s:] * b[:rows - s] + b[s:]], axis=0)
                a = jnp.concatenate([a[:s], a[s:] * a[:rows - s]], axis=0)
            s *= 2
        hs = b + a * h_ref[...]
        h_ref[...] = hs[rows - 1:rows, :]
    else:
        hprev = h_ref[...]
        parts = []
        for i in range(tt):
            hprev = a[i * bb:(i + 1) * bb] * hprev + b[i * bb:(i + 1) * bb]
            parts.append(hprev)
        hs = jnp.concatenate(parts, axis=0)
        h_ref[...] = hprev
    y_b = hs * jax.nn.gelu(u_g)

    mix = _dot(y_a.astype(BF16), wout_ref[0:pw, :]) + _dot(y_b.astype(BF16), wout_ref[pw:2 * pw, :])
    x1_ref[0] = x + per_row(gt_ref[0]) * mix

    @pl.when(t == n_t - 1)
    def _():
        pooln_ref[0] = zp_ref[p0 + rows - (POOL_HALO - 1) * bb:p0 + rows, :]
        convn_ref[0] = zc_ref[c0 + rows - (CONV_WIDTH - 1) * bb:c0 + rows, :]
        lrun_ref[0] = h_ref[...]

    if n_t > 1:
        @pl.when(t < n_t - 1)
        def _():
            zp_ref[bb:p0, :] = zp_ref[rows + bb:rows + p0, :]
            zc_ref[c0 - (CONV_WIDTH - 1) * bb:c0, :] = zc_ref[c0 + rows - (CONV_WIDTH - 1) * bb:c0 + rows, :]


def _mix0(x, sh, sc, gt, pool0, conv0, lru0, wts, *, tt, bb, start):
    nbb, tot, d = x.shape
    n_t = tot // (tt * bb)
    rows = tt * bb
    pw = pool0.shape[-1]
    hp = (POOL_HALO - 1) * bb
    hc = (CONV_WIDTH - 1) * bb

    def const(a):
        nd = a.ndim
        return pl.BlockSpec(a.shape, lambda i, j: (0,) * nd)

    def per_b(n_rows, width):
        return pl.BlockSpec((1, n_rows, width), lambda i, j: (i, 0, 0))

    kern = functools.partial(_mix0_kernel, tt=tt, bb=bb, start=start, n_t=n_t)
    return pl.pallas_call(
        kern,
        grid=(nbb, n_t),
        in_specs=[pl.BlockSpec((1, rows, d), lambda i, j: (i, j, 0)),
                  per_b(bb, d), per_b(bb, d), per_b(bb, d),
                  per_b(hp, pw), per_b(hc, pw), per_b(bb, pw)] + [const(w) for w in wts],
        out_specs=[pl.BlockSpec((1, rows, d), lambda i, j: (i, j, 0)),
                   per_b(hp, pw), per_b(hc, pw), per_b(bb, pw)],
        out_shape=[jax.ShapeDtypeStruct((nbb, tot, d), F32),
                   jax.ShapeDtypeStruct((nbb, hp, pw), F32),
                   jax.ShapeDtypeStruct((nbb, hc, pw), F32),
                   jax.ShapeDtypeStruct((nbb, bb, pw), F32)],
        scratch_shapes=[pltpu.VMEM(((POOL_HALO + tt) * bb, pw), F32),
                        pltpu.VMEM(((CONV_HALO + tt) * bb, pw), F32),
                        pltpu.VMEM((bb, pw), F32)],
        compiler_params=_cparams(2),
        name="mix0",
    )(x, sh, sc, gt, pool0, conv0, lru0, *wts)


ROUTE_ROWS = 40
def _route_kernel(*refs, with_proj):
    if with_proj:
        (x_ref, o_ref, wout_ref, gtm_ref, shf_ref, scf_ref, gain_ref, wrh_ref, wrl_ref, br_ref,
         x1_ref, hf_ref, info_ref, meta_ref, cnt_ref, run_ref) = refs
    else:
        (x_ref, shf_ref, scf_ref, gain_ref, wrh_ref, wrl_ref, br_ref,
         hf_ref, info_ref, meta_ref, cnt_ref, run_ref) = refs

    @pl.when(pl.program_id(0) == 0)
    def _():
        run_ref[...] = jnp.zeros(run_ref.shape, F32)

    x = x_ref[...]
    if with_proj:
        x = x + gtm_ref[0] * _dot(o_ref[...], wout_ref[...])
        x1_ref[...] = x
    hf = _rms(x, gain_ref[...]) * (1.0 + scf_ref[0]) + shf_ref[0]
    hf_ref[...] = hf

    hi = hf.astype(BF16)
    lo = (hf - hi.astype(F32)).astype(BF16)
    logits = _dot(hi, wrh_ref[...]) + _dot(lo, wrh_ref[...]) + _dot(hi, wrl_ref[...]) + br_ref[...]

    tq = logits.shape[0]
    lt = logits.T[:ROUTE_ROWS]
    row = lax.broadcasted_iota(jnp.int32, (ROUTE_ROWS, tq), 0).astype(F32)
    neg = -jnp.inf
    big = float(ROUTE_ROWS)
    is_g = row < MOE_GROUPS
    gl = jnp.where(is_g, lt, neg)
    mg = jnp.max(gl, axis=0, keepdims=True)
    gidx = jnp.min(jnp.where(gl == mg, row, big), axis=0, keepdims=True)
    p_group = 1.0 / jnp.sum(jnp.where(is_g, jnp.exp(gl - mg), 0.0), axis=0, keepdims=True)
    first = MOE_GROUPS + gidx * EXPERTS_PER_GROUP
    el = jnp.where((row >= first) & (row < first + EXPERTS_PER_GROUP), lt, neg)
    v1 = jnp.max(el, axis=0, keepdims=True)
    i1 = jnp.min(jnp.where(el == v1, row, big), axis=0, keepdims=True)
    el2 = jnp.where(row == i1, neg, el)
    v2 = jnp.max(el2, axis=0, keepdims=True)
    i2 = jnp.min(jnp.where(el2 == v2, row, big), axis=0, keepdims=True)
    ex = jnp.exp(v2 - v1)
    g1 = p_group / (1.0 + ex)
    g2 = p_group * ex / (1.0 + ex)
    e1 = i1 - MOE_GROUPS
    e2 = i2 - MOE_GROUPS

    erow = lax.broadcasted_iota(jnp.int32, (N_EXPERTS, tq), 0).astype(F32)
    oh1 = jnp.where(erow == e1, 1.0, 0.0)
    oh2 = jnp.where(erow == e2, 1.0, 0.0)
    oh = oh1 + oh2
    ri = lax.broadcasted_iota(jnp.int32, (tq, tq), 0)
    ci = lax.broadcasted_iota(jnp.int32, (tq, tq), 1)
    earlier = jnp.where(ri < ci, 1.0, 0.0).astype(BF16)
    before = _dot(oh.astype(BF16), earlier) + run_ref[...]
    r1 = jnp.sum(before * oh1, axis=0, keepdims=True)
    r2 = jnp.sum(before * oh2, axis=0, keepdims=True)
    run_ref[...] = run_ref[...] + jnp.sum(oh, axis=1, keepdims=True)
    cnt_ref[...] = jnp.broadcast_to(run_ref[...], cnt_ref.shape)

    meta = jnp.concatenate([e1, e2, g1, g2, r1, r2, jnp.zeros((2, tq), F32)], axis=0)
    meta_ref[0] = meta
    info_ref[...] = jnp.concatenate([meta, jnp.zeros((LANES - 8, tq), F32)], axis=0).T


def _route(x, shf, scf, gain, wrh, wrl, br, *, tile, rows_per_mod, proj=None):
    n, d = x.shape
    nt = n // tile
    mrows = shf.shape[1]
    mod_spec = pl.BlockSpec((1, mrows, d), lambda i: ((i * tile) // rows_per_mod, 0, 0))
    row_spec = pl.BlockSpec((tile, d), lambda i: (i, 0))

    def const(a):
        nd = a.ndim
        return pl.BlockSpec(a.shape, lambda i: (0,) * nd)

    info_spec = pl.BlockSpec((tile, LANES), lambda i: (i, 0))
    meta_spec = pl.BlockSpec((1, 8, tile), lambda i: (i, 0, 0))
    cnt_spec = pl.BlockSpec((N_EXPERTS, LANES), lambda i: (0, 0))
    outs_shape = [jax.ShapeDtypeStruct((n, d), F32), jax.ShapeDtypeStruct((n, LANES), F32),
                  jax.ShapeDtypeStruct((nt, 8, tile), F32), jax.ShapeDtypeStruct((N_EXPERTS, LANES), F32)]
    outs_spec = [row_spec, info_spec, meta_spec, cnt_spec]
    if proj is None:
        ins = [x, shf, scf, gain, wrh, wrl, br]
        in_specs = [row_spec, mod_spec, mod_spec, const(gain), const(wrh), const(wrl), const(br)]
    else:
        o, wout, gtm = proj
        ins = [x, o, wout, gtm, shf, scf, gain, wrh, wrl, br]
        in_specs = [row_spec, pl.BlockSpec((tile, o.shape[1]), lambda i: (i, 0)), const(wout), mod_spec,
                    mod_spec, mod_spec, const(gain), const(wrh), const(wrl), const(br)]
        outs_shape = [jax.ShapeDtypeStruct((n, d), F32)] + outs_shape
        outs_spec = [row_spec] + outs_spec
    return pl.pallas_call(
        functools.partial(_route_kernel, with_proj=proj is not None),
        grid=(nt,),
        in_specs=in_specs,
        out_specs=outs_spec,
        out_shape=outs_shape,
        scratch_shapes=[pltpu.VMEM((N_EXPERTS, 1), F32)],
        compiler_params=_cparams(1),
        name="route",
    )(*ins)


def _dispatch_kernel(*refs, tiles, first_steps):
    xs_ref, sem = refs[-2:]
    i = pl.program_id(0)

    def copy_tile(slot_ref, hf_ref, tile):
        def issue(g, c):
            for u in range(ROW_DMA_UNROLL):
                r = g * ROW_DMA_UNROLL + u
                for k in range(TOP_K):
                    s = slot_ref[0, 0, k * tile + r]
                    pltpu.make_async_copy(hf_ref.at[pl.ds(r, 1), :], xs_ref.at[pl.ds(s, 1), :], sem).start()
            return c

        lax.fori_loop(0, tile // ROW_DMA_UNROLL, issue, 0)
        for k in range(TOP_K):
            pltpu.make_async_copy(hf_ref, xs_ref.at[pl.ds(0, tile), :], sem).wait()

    for g, tile in enumerate(tiles):
        lo, hi = first_steps[g], first_steps[g + 1]

        @pl.when((i >= lo) & (i < hi))
        def _(g=g, tile=tile):
            copy_tile(refs[2 * g], refs[2 * g + 1], tile)


def _dispatch(hfs, slots, tiles):
    d = hfs[0].shape[1]
    n_tiles = [hf.shape[0] // t for hf, t in zip(hfs, tiles)]
    first_steps = [sum(n_tiles[:g]) for g in range(len(hfs) + 1)]
    ins, in_specs = [], []
    for g, (hf, sl, tile) in enumerate(zip(hfs, slots, tiles)):
        def blk(i, lo=first_steps[g], last=n_tiles[g] - 1):
            return jnp.clip(i - lo, 0, last)

        ins += [sl.reshape(n_tiles[g], 1, TOP_K * tile), hf]
        in_specs += [pl.BlockSpec((1, 1, TOP_K * tile), lambda i, blk=blk: (blk(i), 0, 0),
                                  memory_space=pltpu.SMEM),
                     pl.BlockSpec((tile, d), lambda i, blk=blk: (blk(i), 0))]
    return pl.pallas_call(
        functools.partial(_dispatch_kernel, tiles=tuple(tiles), first_steps=tuple(first_steps)),
        grid=(first_steps[-1],),
        in_specs=in_specs,
        out_specs=pl.BlockSpec(memory_space=pl.ANY),
        out_shape=jax.ShapeDtypeStruct((TOP_K * sum(hf.shape[0] for hf in hfs), d), F32),
        scratch_shapes=[pltpu.SemaphoreType.DMA(())],
        compiler_params=_cparams(1),
        name="moe_dispatch",
    )(*ins)


def _gmm_kernel(wt_ref, we_ref, wlo_ref, whi_ref, xs_ref, wg_ref, wu_ref, wd_ref, y_ref,
                wgb_ref, wub_ref, wdb_ref):
    w = pl.program_id(0)
    prev = jnp.maximum(w - 1, 0)
    new_expert = (w == 0) | (we_ref[w] != we_ref[prev])
    new_tile = (w == 0) | (wt_ref[w] != wt_ref[prev])

    @pl.when(new_expert)
    def _():
        wgb_ref[...] = wg_ref[0, 0].astype(BF16)
        wub_ref[...] = wu_ref[0, 0].astype(BF16)
        wdb_ref[...] = wd_ref[0, 0].astype(BF16)

    @pl.when(new_tile)
    def _():
        y_ref[...] = jnp.zeros(y_ref.shape, F32)

    lo = wlo_ref[w]
    hi = whi_ref[w]

    @pl.when(hi > lo)
    def _():
        x = xs_ref[...].astype(BF16)
        g = _dot(x, wgb_ref[...])
        u = _dot(x, wub_ref[...])
        yv = _dot((_silu(g) * u).astype(BF16), wdb_ref[...])
        row = lax.broadcasted_iota(jnp.int32, (x.shape[0], 1), 0)
        y_ref[...] = y_ref[...] + jnp.where((row >= lo) & (row < hi), yv, 0.0)


def _gmm(xs, work, w_gate, w_up, w_down, *, tile, layer):
    m, d = xs.shape
    ff = w_gate.shape[-1]
    n_work = work[0].shape[0]
    grid_spec = pltpu.PrefetchScalarGridSpec(
        num_scalar_prefetch=4,
        grid=(n_work,),
        in_specs=[pl.BlockSpec((tile, d), lambda w, wt, we, wlo, whi: (wt[w], 0)),
                  pl.BlockSpec((1, 1, d, ff), lambda w, wt, we, wlo, whi: (layer, we[w], 0, 0)),
                  pl.BlockSpec((1, 1, d, ff), lambda w, wt, we, wlo, whi: (layer, we[w], 0, 0)),
                  pl.BlockSpec((1, 1, ff, d), lambda w, wt, we, wlo, whi: (layer, we[w], 0, 0))],
        out_specs=pl.BlockSpec((tile, d), lambda w, wt, we, wlo, whi: (wt[w], 0)),
        scratch_shapes=[pltpu.VMEM((d, ff), BF16), pltpu.VMEM((d, ff), BF16), pltpu.VMEM((ff, d), BF16)],
    )
    return pl.pallas_call(
        _gmm_kernel,
        grid_spec=grid_spec,
        out_shape=jax.ShapeDtypeStruct((m, d), F32),
        compiler_params=_cparams(1),
        name="moe_gmm",
    )(*work, xs, w_gate, w_up, w_down)


def _work_list(counts, n_slots, tile):
    n_tiles = n_slots // tile
    n_work = n_tiles + N_EXPERTS - 1
    ends = jnp.cumsum(counts)
    starts = ends - counts
    first_tile = starts // tile
    last_tile = jnp.maximum(ends - 1, 0) // tile
    n_items = jnp.where(counts > 0, last_tile - first_tile + 1, 0)
    item_end = jnp.cumsum(n_items)
    item_start = item_end - n_items
    w = jnp.arange(n_work, dtype=jnp.int32)
    used = w < item_end[-1]
    e = jnp.minimum(jnp.sum(w[:, None] >= item_end[None, :], axis=1), N_EXPERTS - 1).astype(jnp.int32)
    is_e = e[:, None] == jnp.arange(N_EXPERTS, dtype=jnp.int32)[None, :]

    def of_e(v):
        return jnp.sum(jnp.where(is_e, v[None, :], 0), axis=1)

    t = of_e(first_tile) + (w - of_e(item_start))
    lo = jnp.maximum(of_e(starts), t * tile) - t * tile
    hi = jnp.minimum(of_e(ends), (t + 1) * tile) - t * tile
    last_e = jnp.max(jnp.where(counts > 0, jnp.arange(N_EXPERTS), 0)).astype(jnp.int32)
    wt = jnp.where(used, t, n_tiles - 1).astype(jnp.int32)
    we = jnp.where(used, e, last_e).astype(jnp.int32)
    wlo = jnp.where(used, lo, 0).astype(jnp.int32)
    whi = jnp.where(used, hi, 0).astype(jnp.int32)
    return (wt, we, wlo, whi), starts


def _combine_kernel(*refs, tile, final):
    if final:
        slot_ref, nslot_ref, x_ref, gt_ref, info_ref, fg_ref, y_ref, out_ref, ybuf, sem = refs
    else:
        slot_ref, nslot_ref, x_ref, gt_ref, info_ref, y_ref, out_ref, ybuf, sem = refs
    i = pl.program_id(0)
    cur = i % 2

    def gather(slots_ref, buf):
        def issue(g, c):
            for u in range(ROW_DMA_UNROLL):
                r = g * ROW_DMA_UNROLL + u
                for k in range(TOP_K):
                    s = slots_ref[0, 0, k * tile + r]
                    pltpu.make_async_copy(y_ref.at[pl.ds(s, 1), :], ybuf.at[buf, k, pl.ds(r, 1), :],
                                          sem.at[buf]).start()
            return c

        lax.fori_loop(0, tile // ROW_DMA_UNROLL, issue, 0)

    @pl.when(i == 0)
    def _():
        gather(slot_ref, cur)

    @pl.when(i + 1 < pl.num_programs(0))
    def _():
        gather(nslot_ref, 1 - cur)

    for k in range(TOP_K):
        pltpu.make_async_copy(y_ref.at[pl.ds(0, tile), :], ybuf.at[cur, k], sem.at[cur]).wait()

    info = info_ref[...]
    ffn = info[:, 2:3] * ybuf[cur, 0] + info[:, 3:4] * ybuf[cur, 1]
    out = x_ref[...] + gt_ref[0] * ffn
    if final:
        out = _rms(out, fg_ref[...])
    out_ref[...] = out


def _combine(x, gt, info, y, slots, *, tile, rows_per_mod, final_gain=None):
    n, d = x.shape
    nt = n // tile
    mrows = gt.shape[1]
    final = final_gain is not None
    slots3 = slots.reshape(nt, 1, TOP_K * tile)
    in_specs = [pl.BlockSpec((1, 1, TOP_K * tile), lambda i: (i, 0, 0), memory_space=pltpu.SMEM),
                pl.BlockSpec((1, 1, TOP_K * tile), lambda i: (jnp.minimum(i + 1, nt - 1), 0, 0),
                             memory_space=pltpu.SMEM),
                pl.BlockSpec((tile, d), lambda i: (i, 0)),
                pl.BlockSpec((1, mrows, d), lambda i: ((i * tile) // rows_per_mod, 0, 0)),
                pl.BlockSpec((tile, LANES), lambda i: (i, 0))]
    ins = [slots3, slots3, x, gt, info]
    if final:
        in_specs.append(pl.BlockSpec(final_gain.shape, lambda i: (0, 0)))
        ins.append(final_gain)
    in_specs.append(pl.BlockSpec(memory_space=pl.ANY))
    ins.append(y)
    return pl.pallas_call(
        functools.partial(_combine_kernel, tile=tile, final=final),
        grid=(nt,),
        in_specs=in_specs,
        out_specs=pl.BlockSpec((tile, d), lambda i: (i, 0)),
        out_shape=jax.ShapeDtypeStruct((n, d), F32),
        scratch_shapes=[pltpu.VMEM((2, TOP_K, tile, d), F32), pltpu.SemaphoreType.DMA((2,))],
        compiler_params=_cparams(1),
        name="moe_combine",
    )(*ins)


def _moe(groups, rw, experts, *, layer, final_gain=None):
    gain, wrh, wrl, br = rw
    routed = []
    for g in groups:
        shf, scf, _ = g["mods"]
        res = _route(g["x"], shf, scf, gain, wrh, wrl, br, tile=g["tile"], rows_per_mod=g["rows_per_mod"],
                     proj=g["proj"])
        x = g["x"]
        if g["proj"] is not None:
            x, res = res[0], res[1:]
        hf, info, meta, cnt = res
        routed.append((x, hf, info, meta, cnt[:, 0].astype(jnp.int32)))
    n_rows = TOP_K * sum(r[0].shape[0] for r in routed)
    gmm_tile = _pick(n_rows, 512)
    work, base = _work_list(sum(r[4] for r in routed), n_rows, gmm_tile)
    base = base.astype(jnp.int32)
    slots = []
    for _, _, _, meta, counts in routed:
        eid = meta[:, 0:TOP_K, :].astype(jnp.int32)
        rank = meta[:, 4:4 + TOP_K, :].astype(jnp.int32)
        start_of = jnp.zeros(eid.shape, jnp.int32)
        for e in range(N_EXPERTS):
            start_of = jnp.where(eid == e, base[e], start_of)
        slots.append(start_of + rank)
        base = base + counts
    xs = _dispatch([r[1] for r in routed], slots, [g["tile"] for g in groups])
    y = _gmm(xs, work, *experts, tile=gmm_tile, layer=layer)
    return [_combine(r[0], g["mods"][2], r[2], y, sl, tile=g["tile"], rows_per_mod=g["rows_per_mod"],
                     final_gain=final_gain)
            for g, r, sl in zip(groups, routed, slots)]


def _rope_turn(blk, c, s):
    return blk * c + pltpu.roll(blk, LANES - QK_ROPE, 1) * s


def _mla_proj_kernel(*refs, sample):
    if sample:
        (x_ref, sh_ref, sc_ref, gain_ref, win_ref, qg_ref, wuq_ref, kvg_ref, c_ref, s_ref,
         ckv_ref, kr_ref, q_ref) = refs
    else:
        (x_ref, sh_ref, sc_ref, gain_ref, win_ref, qg_ref, wuq_ref, kvg_ref, c_ref, s_ref, wuk_ref, wuvt_ref,
         vonet_ref, ct_ref, st_ref, ckv_ref, kr_ref, q_ref, k_ref, v_ref) = refs
    q_lora = qg_ref.shape[1]
    kv_lora = kvg_ref.shape[1]
    h = _rms(x_ref[...], gain_ref[...]) * (1.0 + sc_ref[0]) + sh_ref[0]
    proj = _dot(h.astype(BF16), win_ref[...])
    qn = _rms(proj[:, :q_lora], qg_ref[...])
    ckv = _rms(proj[:, q_lora:q_lora + kv_lora], kvg_ref[...])
    c = c_ref[...]
    s = s_ref[...]
    kf = _rope_turn(proj[:, q_lora + kv_lora:], c, s)
    ckv_ref[...] = ckv
    kr_ref[...] = kf[:, QK_NOPE:QK_NOPE + QK_ROPE]
    qn_b = qn.astype(BF16)
    if sample:
        q = _dot(qn_b, wuq_ref[...])
        for hh in range(MLA_HEADS):
            cols = slice(hh * HEAD_PAD, (hh + 1) * HEAD_PAD)
            q_ref[:, cols] = _rope_turn(q[:, cols], c, s) * SM_SCALE
    else:
        ckv_b = ckv.astype(BF16)
        kn = _dot(ckv_b, wuk_ref[...])
        qt = _dot_nt(wuq_ref[...], qn_b)
        v_ref[0] = (_dot_nt(wuvt_ref[...], ckv_b) + vonet_ref[...]).astype(BF16)
        ct = ct_ref[...]
        st = st_ref[...]
        for hh in range(MLA_HEADS):
            cols = slice(hh * HEAD_PAD, (hh + 1) * HEAD_PAD)
            k_ref[:, cols] = (kn[:, cols] + kf).astype(BF16)
            blk = qt[cols, :]
            turned = jnp.concatenate([blk[QK_ROPE:], blk[:QK_ROPE]], axis=0)
            q_ref[0, cols, :] = ((blk * ct + turned * st) * (SM_SCALE * LOG2E)).astype(BF16)


def _mla_proj(x, sh, sc, gain, win, qg, wuq, kvg, ctab, stab, *, tile, rows_per_mod, tab_tiles, kv_w=None):
    n, d = x.shape
    nt = n // tile
    mrows = sh.shape[1]
    sample = kv_w is None
    kv_lora = kvg.shape[1]

    def const(a):
        nd = a.ndim
        return pl.BlockSpec(a.shape, lambda i: (0,) * nd)

    row = lambda width: pl.BlockSpec((tile, width), lambda i: (i, 0))
    mod_spec = pl.BlockSpec((1, mrows, d), lambda i: ((i * tile) // rows_per_mod, 0, 0))
    tab_spec = pl.BlockSpec((tile, LANES), lambda i: (i % tab_tiles, 0))
    ins = [x, sh, sc, gain, win, qg, wuq, kvg, ctab, stab]
    in_specs = [row(d), mod_spec, mod_spec, const(gain), const(win), const(qg), const(wuq), const(kvg),
                tab_spec, tab_spec]
    qw = MLA_HEADS * HEAD_PAD
    out_shape = [jax.ShapeDtypeStruct((n, kv_lora), F32), jax.ShapeDtypeStruct((n, QK_ROPE), F32)]
    out_specs = [row(kv_lora), row(QK_ROPE)]
    if sample:
        out_shape.append(jax.ShapeDtypeStruct((n, qw), F32))
        out_specs.append(row(qw))
    else:
        wuk, wuvt, vonet, ctab_t, stab_t = kv_w
        seq_len = tab_tiles * tile
        tab_t_spec = pl.BlockSpec((LANES, tile), lambda i: (0, i % tab_tiles))
        by_seq = pl.BlockSpec((1, qw, tile), lambda i: (i // tab_tiles, 0, i % tab_tiles))
        ins += [wuk, wuvt, vonet, ctab_t, stab_t]
        in_specs += [const(wuk), const(wuvt), const(vonet), tab_t_spec, tab_t_spec]
        out_shape += [jax.ShapeDtypeStruct((n // seq_len, qw, seq_len), BF16), jax.ShapeDtypeStruct((n, qw), BF16),
                      jax.ShapeDtypeStruct((n // seq_len, qw, seq_len), BF16)]
        out_specs += [by_seq, row(qw), by_seq]
    return pl.pallas_call(
        functools.partial(_mla_proj_kernel, sample=sample),
        grid=(nt,),
        in_specs=in_specs,
        out_specs=out_specs,
        out_shape=out_shape,
        compiler_params=_cparams(1),
        name="mla_proj",
    )(*ins)


def _attn_kernel(q_ref, k_ref, v_ref, o_ref, sa_ref, sb_ref, m_ref, acc_ref, *, tq, tk, q_tiles):
    heads = range(2)
    for sub in range(q_tiles):
        _attn_query_tile(q_ref, k_ref, v_ref, o_ref, sa_ref, sb_ref, m_ref, acc_ref,
                         i=pl.program_id(2) * q_tiles + sub, lanes=slice(sub * tq, (sub + 1) * tq),
                         heads=heads, tq=tq, tk=tk)


def _attn_query_tile(q_ref, k_ref, v_ref, o_ref, sa_ref, sb_ref, m_ref, acc_ref, *, i, lanes, heads, tq, tk):
    def cols(hh):
        return slice(hh * HEAD_PAD, (hh + 1) * HEAD_PAD)

    def scores(j, dst_ref):
        keys = pl.ds(pl.multiple_of(j * tk, tk), tk)
        for hh in heads:
            dst_ref[hh] = _dot(k_ref[0, keys, cols(hh)], q_ref[0, cols(hh), lanes])

    def absorb(j, src_ref, masked):
        keys = pl.ds(pl.multiple_of(j * tk, tk), tk)
        if masked:
            keep = lax.broadcasted_iota(jnp.int32, (tk, tq), 0) <= lax.broadcasted_iota(jnp.int32, (tk, tq), 1)
        for hh in heads:
            s = src_ref[hh]
            if masked:
                s = jnp.where(keep, s, -jnp.inf)
            m = m_ref[hh]
            m_new = jnp.maximum(m, jnp.max(s, axis=0, keepdims=True))
            p = jnp.exp2(s - m_new)
            acc_ref[hh] = jnp.exp2(m - m_new) * acc_ref[hh] + _dot(v_ref[0, cols(hh), keys], p.astype(BF16))
            m_ref[hh] = m_new

    m_ref[...] = jnp.full(m_ref.shape, -jnp.inf, F32)
    acc_ref[...] = jnp.zeros(acc_ref.shape, F32)
    scores(0, sa_ref)

    def pair(j):
        scores(j + 1, sb_ref)
        absorb(j, sa_ref, False)
        scores(j + 2, sa_ref)
        absorb(j + 1, sb_ref, False)

    done = 0
    pairs_per_trip = ATTN_PAIRS_PER_TRIP
    while pairs_per_trip >= 1:
        def trip(t, c, start=done, n=pairs_per_trip):
            for q in range(n):
                pair(start + 2 * (n * t + q))
            return c

        trips = (i - done) // (2 * pairs_per_trip)
        lax.fori_loop(0, trips, trip, 0)
        done = done + 2 * pairs_per_trip * trips
        pairs_per_trip //= 2

    @pl.when(i % 2 == 1)
    def _():
        scores(i, sb_ref)
        absorb(i - 1, sa_ref, False)
        absorb(i, sb_ref, True)

    @pl.when(i % 2 == 0)
    def _():
        absorb(i, sa_ref, True)

    out_t = jnp.concatenate([acc_ref[hh][:V_DIM] / acc_ref[hh][V_DIM:V_DIM + 1] for hh in heads], axis=0)
    o_ref[0, lanes, :] = out_t.T.astype(o_ref.dtype)


def _attention(q_t, k, v_t, *, tq, tk):
    assert tq == tk, "one diagonal key tile per query tile"
    b, s, _ = k.shape
    pairs = MLA_HEADS // 2
    q_tiles = 2 if (s // tq) % 2 == 0 else 1
    return pl.pallas_call(
        functools.partial(_attn_kernel, tq=tq, tk=tk, q_tiles=q_tiles),
        grid=(b, pairs, s // (tq * q_tiles)),
        in_specs=[pl.BlockSpec((1, 2 * HEAD_PAD, tq * q_tiles), lambda bi, p, i: (bi, p, i)),
                  pl.BlockSpec((1, s, 2 * HEAD_PAD), lambda bi, p, i: (bi, 0, p)),
                  pl.BlockSpec((1, 2 * HEAD_PAD, s), lambda bi, p, i: (bi, p, 0))],
        out_specs=pl.BlockSpec((1, tq * q_tiles, 2 * V_DIM), lambda bi, p, i: (bi, i, p)),
        out_shape=jax.ShapeDtypeStruct((b, s, MLA_HEADS * V_DIM), BF16),
        scratch_shapes=[pltpu.VMEM((2, tk, tq), F32), pltpu.VMEM((2, tk, tq), F32),
                        pltpu.VMEM((2, 1, tq), F32), pltpu.VMEM((2, HEAD_PAD, tq), F32)],
        compiler_params=_cparams(3),
        name="prompt_attn",
    )(q_t, k, v_t)


def _qabs_kernel(q_ref, m_ref, o_ref):
    o_ref[0] = _dot(q_ref[...].astype(BF16), m_ref[0])


def _qabs(q, mats):
    n = q.shape[0]
    width = mats.shape[-1]
    return pl.pallas_call(
        _qabs_kernel,
        grid=(MLA_HEADS,),
        in_specs=[pl.BlockSpec((n, HEAD_PAD), lambda h: (0, h)),
                  pl.BlockSpec((1, HEAD_PAD, width), lambda h: (h, 0, 0))],
        out_specs=pl.BlockSpec((1, n, width), lambda h: (h, 0, 0)),
        out_shape=jax.ShapeDtypeStruct((MLA_HEADS, n, width), F32),
        compiler_params=_cparams(1),
        name="sample_qabs",
    )(q, mats)


def _paged_attn_kernel(pt_ref, q_ref, cn_ref, rn_ref, ck_hbm, kr_hbm, o_ref,
                       ckbuf, krbuf, m_ref, l_ref, acc_ref, sem, *, group, pages, page, n_chunks, t_new, t_pad):
    bg = pl.program_id(0)
    c = pl.program_id(1)
    step = bg * n_chunks + c
    n_steps = pl.num_programs(0) * n_chunks
    slot = step % PAGE_RING
    kv_lora = ckbuf.shape[-1]

    def start_chunk(st):
        bgi, ci, sl = st // n_chunks, st % n_chunks, st % PAGE_RING
        for g in range(group):
            for p in range(pages):
                phys = pt_ref[bgi * group + g, ci * pages + p]
                pltpu.make_async_copy(ck_hbm.at[phys], ckbuf.at[sl, g, pl.ds(p * page, page), :],
                                      sem.at[0, sl]).start()
                pltpu.make_async_copy(kr_hbm.at[phys], krbuf.at[sl, g, p], sem.at[1, sl]).start()

    @pl.when(step == 0)
    def _():
        for ahead in range(PAGE_RING - 1):
            @pl.when(ahead < n_steps)
            def _():
                start_chunk(step + ahead)

    pltpu.make_async_copy(ckbuf.at[slot], ckbuf.at[slot], sem.at[0, slot]).wait()
    pltpu.make_async_copy(krbuf.at[slot], krbuf.at[slot], sem.at[1, slot]).wait()

    @pl.when(c == 0)
    def _():
        m_ref[...] = jnp.full(m_ref.shape, -jnp.inf, F32)
        l_ref[...] = jnp.zeros(l_ref.shape, F32)
        acc_ref[...] = jnp.zeros(acc_ref.shape, F32)

    def absorb(g, s, values):
        m = m_ref[g]
        m_new = jnp.maximum(m, jnp.max(s, axis=1, keepdims=True))
        alpha = jnp.exp(m - m_new)
        p = jnp.exp(s - m_new)
        l_ref[g] = alpha * l_ref[g] + jnp.sum(p, axis=1, keepdims=True)
        acc_ref[g] = alpha * acc_ref[g] + _dot(p.astype(BF16), values)
        m_ref[g] = m_new

    cks, scores = [], []
    for g in range(group):
        qa = q_ref[g]
        q_lat = qa[:, :kv_lora].astype(BF16)
        q_rope = qa[:, kv_lora:kv_lora + QK_ROPE].astype(BF16)
        ck = ckbuf[slot, g].astype(BF16)
        kr_t = jnp.concatenate([krbuf[slot, g, p] for p in range(pages)], axis=1).astype(BF16)
        cks.append(ck)
        scores.append(_dot_nt(q_lat, ck) + _dot(q_rope, kr_t))

    nxt = step + PAGE_RING - 1

    @pl.when(nxt < n_steps)
    def _():
        start_chunk(nxt)

    for g in range(group):
        absorb(g, scores[g], cks[g])

    @pl.when(c == n_chunks - 1)
    def _():
        rows = q_ref.shape[1]
        t_q = lax.broadcasted_iota(jnp.int32, (rows, t_pad), 0) % t_new
        t_k = lax.broadcasted_iota(jnp.int32, (rows, t_pad), 1)
        for g in range(group):
            qa = q_ref[g]
            q_lat = qa[:, :kv_lora].astype(BF16)
            q_rope = qa[:, kv_lora:kv_lora + QK_ROPE].astype(BF16)
            cn = cn_ref[g].astype(BF16)
            s = _dot_nt(q_lat, cn) + _dot_nt(q_rope, rn_ref[g].astype(BF16))
            absorb(g, jnp.where(t_k <= t_q, s, -jnp.inf), cn)
            o_ref[g] = acc_ref[g] / l_ref[g]


def _paged_attn(page_table, qabs, ckv_new, kr_new, cache_ck, cache_kr_t, *, group, pages, t_new):
    b, rows, width = qabs.shape
    n_pages = page_table.shape[1]
    n_chunks = n_pages // pages
    page, kv_lora = cache_ck.shape[1:]
    t_pad = ckv_new.shape[1]
    grid_spec = pltpu.PrefetchScalarGridSpec(
        num_scalar_prefetch=1,
        grid=(b // group, n_chunks),
        in_specs=[pl.BlockSpec((group, rows, width), lambda bi, ci, pt: (bi, 0, 0)),
                  pl.BlockSpec((group, t_pad, kv_lora), lambda bi, ci, pt: (bi, 0, 0)),
                  pl.BlockSpec((group, t_pad, QK_ROPE), lambda bi, ci, pt: (bi, 0, 0)),
                  pl.BlockSpec(memory_space=pl.ANY),
                  pl.BlockSpec(memory_space=pl.ANY)],
        out_specs=pl.BlockSpec((group, rows, kv_lora), lambda bi, ci, pt: (bi, 0, 0)),
        scratch_shapes=[pltpu.VMEM((PAGE_RING, group, pages * page, kv_lora), F32),
                        pltpu.VMEM((PAGE_RING, group, pages, QK_ROPE, page), F32),
                        pltpu.VMEM((group, rows, 1), F32), pltpu.VMEM((group, rows, 1), F32),
                        pltpu.VMEM((group, rows, kv_lora), F32),
                        pltpu.SemaphoreType.DMA((2, PAGE_RING))],
    )
    return pl.pallas_call(
        functools.partial(_paged_attn_kernel, group=group, pages=pages, page=page, n_chunks=n_chunks, t_new=t_new,
                          t_pad=t_pad),
        grid_spec=grid_spec,
        out_shape=jax.ShapeDtypeStruct((b, rows, kv_lora), F32),
        compiler_params=_cparams(2),
        name="paged_attn",
    )(page_table, qabs, ckv_new, kr_new, cache_ck, cache_kr_t)


def _vup_kernel(o_ref, w_ref, out_ref):
    lat = jnp.concatenate([o_ref[0], o_ref[1]], axis=-1).astype(BF16)
    out_ref[...] = _dot(lat, w_ref[0]).astype(out_ref.dtype)


def _vup(o_lat, w_pairs):
    h, n, c = o_lat.shape
    return pl.pallas_call(
        _vup_kernel,
        grid=(h // 2,),
        in_specs=[pl.BlockSpec((2, n, c), lambda p: (p, 0, 0)),
                  pl.BlockSpec((1, 2 * c, 2 * V_DIM), lambda p: (p, 0, 0))],
        out_specs=pl.BlockSpec((n, 2 * V_DIM), lambda p: (0, p)),
        out_shape=jax.ShapeDtypeStruct((n, h * V_DIM), BF16),
        compiler_params=_cparams(1),
        name="sample_vup",
    )(o_lat, w_pairs)


def _block_diag(w):
    g, a, b = w.shape
    out = jnp.zeros((g * a, g * b), w.dtype)
    for i in range(g):
        out = out.at[i * a:(i + 1) * a, i * b:(i + 1) * b].set(w[i])
    return out


def _swap_halves(w):
    half = w.shape[-1] // 2
    return jnp.concatenate([w[..., half:], w[..., :half]], axis=-1)


def _rope_tables(pos):
    half = QK_ROPE // 2
    inv_freq = ROPE_THETA ** (-jnp.arange(half, dtype=F32) / half)
    ang = pos.astype(F32)[:, None] * inv_freq[None, :]
    cos, sin = jnp.cos(ang), jnp.sin(ang)
    n = pos.shape[0]
    ctab = jnp.concatenate([jnp.ones((n, QK_NOPE), F32), cos, cos, jnp.zeros((n, QK_ROPE), F32)], axis=1)
    stab = jnp.concatenate([jnp.zeros((n, QK_NOPE), F32), -sin, sin, jnp.zeros((n, QK_ROPE), F32)], axis=1)
    return ctab, stab


def _router_weights(w_group, b_group, w_expert, b_expert):
    d = w_group.shape[0]
    we = jnp.transpose(w_expert, (1, 0, 2)).reshape(d, N_EXPERTS)
    w = jnp.concatenate([w_group, we, jnp.zeros((d, LANES - MOE_GROUPS - N_EXPERTS), F32)], axis=1)
    bias = jnp.concatenate([b_group, b_expert.reshape(-1), jnp.zeros((LANES - MOE_GROUPS - N_EXPERTS,), F32)])
    hi = w.astype(BF16)
    lo = (w - hi.astype(F32)).astype(BF16)
    return hi, lo, bias.reshape(1, LANES)


def _pick(n, pref):
    t = min(n, pref)
    while n % t:
        t //= 2
    return t


def kernel(x_prompt, x_sample, state_pool, state_conv, state_lru, cache_kv_latent, cache_k_rope, page_table,
           c_prompt, c_sample, ada_w, ada_b, norm_mix, norm_ffn, ab_w_in, ab_w_out, pool_w, pool_scale,
           conv_w, conv_b, lru_w_r, lru_b_r, lru_w_i, lru_b_i, lru_lambda,
           mla_w_in, mla_q_norm, mla_w_uq, mla_kv_norm, mla_w_uk, mla_w_uv, mla_w_out,
           router_w_group, router_b_group, router_w_expert, router_b_expert,
           moe_w_gate, moe_w_up, moe_w_down, final_norm):
    bp, seq, d = x_prompt.shape
    bs, t_new, _ = x_sample.shape
    depth = ada_w.shape[0]
    pw = pool_scale.shape[-1]
    page = cache_kv_latent.shape[2]
    past_len = page_table.shape[1] * page
    q_lora = mla_q_norm.shape[-1]
    kv_lora = mla_kv_norm.shape[-1]

    n_c = bp + bs
    n_c_pad = -(-n_c // 8) * 8
    c_all = jnp.concatenate([c_prompt, c_sample, jnp.zeros((n_c_pad - n_c, d), F32)], axis=0)
    mod = _ada(c_all, ada_w, ada_b)

    def mods(layer, lo, hi):
        return [mod[layer, lo:hi, k * d:(k + 1) * d] for k in range(6)]

    n_p = bp * seq
    n_s = bs * t_new
    tile_p = _pick(seq, 512)
    tile_s = _pick(n_s, 512)

    xp = x_prompt.reshape(n_p, d)
    xs = x_sample.reshape(n_s, d)
    outs = {}

    for layer in range(depth):
        sh_m, sc_m, gt_m, sh_f, sc_f, gt_f = mods(layer, 0, bp)
        sh_ms, sc_ms, gt_ms, sh_fs, sc_fs, gt_fs = mods(layer, bp, bp + bs)
        rep = lambda v: jnp.repeat(v, t_new, axis=0)[None]
        per_seq = lambda v: v[:, None, :]
        rw = (norm_ffn[layer].reshape(1, d),) + _router_weights(
            router_w_group[layer], router_b_group[layer], router_w_expert[layer], router_b_expert[layer])
        experts = (moe_w_gate, moe_w_up, moe_w_down)
        final_gain = final_norm.reshape(1, d) if layer == depth - 1 else None
        ffn_mods_p = (per_seq(sh_f), per_seq(sc_f), per_seq(gt_f))
        ffn_mods_s = (rep(sh_fs), rep(sc_fs), rep(gt_fs))

        if layer % 2 == 0:
            e = layer // 2
            wts = (norm_mix[layer].reshape(1, d), ab_w_in[e].astype(BF16), _block_diag(pool_w[e]).astype(BF16),
                   pool_scale[e].reshape(1, pw), conv_w[e], conv_b[e].reshape(1, pw),
                   jnp.concatenate([_block_diag(lru_w_r[e]), _block_diag(lru_w_i[e])], axis=1).astype(BF16),
                   jnp.concatenate([lru_b_r[e], lru_b_i[e]]).reshape(1, 2 * pw),
                   lru_lambda[e].reshape(1, pw), ab_w_out[e].astype(BF16))
            x1p, pool_p, conv_p, lru_p = _mix0(
                xp.reshape(bp, seq, d), per_seq(sh_m), per_seq(sc_m), per_seq(gt_m),
                jnp.zeros((bp, POOL_HALO - 1, pw), F32), jnp.zeros((bp, CONV_WIDTH - 1, pw), F32),
                jnp.zeros((bp, 1, pw), F32), wts, tt=tile_p, bb=1, start=0)
            tm = lambda a: jnp.swapaxes(a, 0, 1).reshape(1, -1, a.shape[-1])
            x1s, pool_s, conv_s, lru_s = _mix0(
                tm(xs.reshape(bs, t_new, d)), sh_ms[None], sc_ms[None], gt_ms[None],
                tm(state_pool[e]), tm(state_conv[e]), state_lru[e][None], wts, tt=t_new, bb=bs, start=past_len)
            bm = lambda a, n: jnp.swapaxes(a.reshape(n, bs, a.shape[-1]), 0, 1)
            outs.setdefault("pool_p", []).append(pool_p)
            outs.setdefault("pool_s", []).append(bm(pool_s, POOL_HALO - 1))
            outs.setdefault("conv_p", []).append(conv_p)
            outs.setdefault("conv_s", []).append(bm(conv_s, CONV_WIDTH - 1))
            outs.setdefault("lru_p", []).append(lru_p.reshape(bp, pw))
            outs.setdefault("lru_s", []).append(lru_s.reshape(bs, pw))
            xp = x1p.reshape(n_p, d)
            xs = bm(x1s, t_new).reshape(n_s, d)
            proj_p = proj_s = None
        else:
            o = layer // 2
            gain = norm_mix[layer].reshape(1, d)
            w_in = mla_w_in[o]
            w_kr = w_in[:, q_lora + kv_lora:]
            win = jnp.concatenate([w_in[:, :q_lora + kv_lora], jnp.zeros((d, QK_NOPE), F32), w_kr,
                                   _swap_halves(w_kr)], axis=1).astype(BF16)
            wq = mla_w_uq[o].reshape(q_lora, MLA_HEADS, QK_NOPE + QK_ROPE)
            wuq = jnp.concatenate([wq, _swap_halves(wq[..., QK_NOPE:])], axis=-1).reshape(
                q_lora, MLA_HEADS * HEAD_PAD).astype(BF16)
            wuk = jnp.concatenate([mla_w_uk[o], jnp.zeros((kv_lora, MLA_HEADS, HEAD_PAD - QK_NOPE), F32)],
                                  axis=-1).reshape(kv_lora, MLA_HEADS * HEAD_PAD).astype(BF16)
            wuv_t = jnp.concatenate([mla_w_uv[o], jnp.zeros((kv_lora, MLA_HEADS, HEAD_PAD - V_DIM), F32)],
                                    axis=-1).reshape(kv_lora, MLA_HEADS * HEAD_PAD).T.astype(BF16)
            vone_t = jnp.tile((jnp.arange(HEAD_PAD) == V_DIM).astype(F32), MLA_HEADS).reshape(-1, 1)
            qg = mla_q_norm[o].reshape(1, q_lora)
            kvg = mla_kv_norm[o].reshape(1, kv_lora)
            wout = mla_w_out[o].astype(BF16)

            ctab, stab = _rope_tables(jnp.arange(seq, dtype=jnp.int32))
            ckv_p, kr_p, qt_p, k_p, vt_p = _mla_proj(
                xp, per_seq(sh_m), per_seq(sc_m), gain, win, qg, wuq.T, kvg, ctab, stab,
                tile=tile_p, rows_per_mod=seq, tab_tiles=seq // tile_p, kv_w=(wuk, wuv_t, vone_t, ctab.T, stab.T))
            qw = MLA_HEADS * HEAD_PAD
            o_p = _attention(qt_p, k_p.reshape(bp, seq, qw), vt_p, tq=tile_p, tk=tile_p)
            proj_p = (o_p.reshape(n_p, MLA_HEADS * V_DIM), wout, per_seq(gt_m))
            outs.setdefault("lat_p", []).append(ckv_p.reshape(bp, seq, kv_lora))
            outs.setdefault("rope_p", []).append(kr_p.reshape(bp, seq, QK_ROPE))

            pos_s = jnp.tile(past_len + jnp.arange(t_new, dtype=jnp.int32), bs)
            ctab_s, stab_s = _rope_tables(pos_s)
            ckv_s, kr_s, q_s = _mla_proj(
                xs, rep(sh_ms), rep(sc_ms), gain, win, qg, wuq, kvg, ctab_s, stab_s,
                tile=tile_s, rows_per_mod=n_s, tab_tiles=n_s // tile_s)
            aw = kv_lora + HEAD_PAD
            wk_t = jnp.transpose(mla_w_uk[o], (1, 2, 0))
            mats = jnp.zeros((MLA_HEADS, HEAD_PAD, aw), F32)
            mats = mats.at[:, :QK_NOPE, :kv_lora].set(wk_t)
            mats = mats.at[:, QK_NOPE:QK_NOPE + QK_ROPE, kv_lora:kv_lora + QK_ROPE].set(
                jnp.broadcast_to(jnp.eye(QK_ROPE, dtype=F32), (MLA_HEADS, QK_ROPE, QK_ROPE)))
            qabs = _qabs(q_s, mats.astype(BF16))
            qabs = jnp.transpose(qabs.reshape(MLA_HEADS, bs, t_new, aw), (1, 0, 2, 3)).reshape(
                bs, MLA_HEADS * t_new, aw)
            t_pad = 8
            pad_t = lambda a: jnp.pad(a.reshape(bs, t_new, -1), ((0, 0), (0, t_pad - t_new), (0, 0)))
            o_lat = _paged_attn(page_table, qabs, pad_t(ckv_s), pad_t(kr_s), cache_kv_latent[o],
                                jnp.swapaxes(cache_k_rope[o], 1, 2),
                                group=_pick(bs, 4), pages=_pick(page_table.shape[1], 16), t_new=t_new)
            o_lat = jnp.transpose(o_lat.reshape(bs, MLA_HEADS, t_new, kv_lora), (1, 0, 2, 3)).reshape(
                MLA_HEADS, n_s, kv_lora)
            wv = jnp.transpose(mla_w_uv[o], (1, 0, 2))
            w_pairs = jnp.zeros((MLA_HEADS // 2, 2 * kv_lora, 2 * V_DIM), F32)
            w_pairs = w_pairs.at[:, :kv_lora, :V_DIM].set(wv[0::2]).at[:, kv_lora:, V_DIM:].set(wv[1::2])
            o_s = _vup(o_lat, w_pairs.astype(BF16))
            proj_s = (o_s, wout, rep(gt_ms))
            outs.setdefault("lat_s", []).append(ckv_s.reshape(bs, t_new, kv_lora))
            outs.setdefault("rope_s", []).append(kr_s.reshape(bs, t_new, QK_ROPE))

        xp, xs = _moe([dict(x=xp, mods=ffn_mods_p, tile=tile_p, rows_per_mod=seq, proj=proj_p),
                       dict(x=xs, mods=ffn_mods_s, tile=tile_s, rows_per_mod=n_s, proj=proj_s)],
                      rw, experts, layer=layer, final_gain=final_gain)

    st = lambda k: jnp.stack(outs[k])
    return (xp.reshape(bp, seq, d), xs.reshape(bs, t_new, d),
            st("pool_p"), st("pool_s"), st("conv_p"), st("conv_s"), st("lru_p"), st("lru_s"),
            st("lat_p"), st("lat_s"), st("rope_p"), st("rope_s"))
```
